```python
import math
import jax
import jax.numpy as jnp
from jax import lax
import numpy as np

D_MODEL = 1024
BATCH = 8
SEQ = 4096
DEPTH = 2

M_HEADS = 4
M_HEAD_DIM = 128
M_WIDTH = M_HEADS * M_HEAD_DIM
M_CHUNK = 128
CONV_WIDTH = 4
A_HEADS = 8
A_HEAD_DIM = 64
A_WIDTH = A_HEADS * A_HEAD_DIM
MOBA_BLOCK = 256
MOBA_TOPK = 3
Q_CHUNK = 16
REL_BUCKETS = 32
REL_MAX_DIST = 1024
D_FF_DENSE = 2816
N_EXPERTS = 8
TOP_K = 2
D_FF_EXPERT = 3584
N_DENSE = (DEPTH + 1) // 2
N_MOE = DEPTH // 2
EPS = 1e-6

IN_SIZES = (M_WIDTH, M_WIDTH, M_WIDTH, M_WIDTH, M_HEADS, M_HEADS, A_WIDTH, A_WIDTH, A_WIDTH, D_MODEL, D_MODEL)
IN_WIDTH = sum(IN_SIZES)

kernel_name = 'hybrid_mlstm_moba_moe_block'


def rms_norm(x, g):
    x32 = x.astype(jnp.float32)
    y = x32 * lax.rsqrt(jnp.mean(x32 * x32, axis=-1, keepdims=True) + EPS)
    return (y * g.astype(jnp.float32)).astype(x.dtype)


def split_columns(z):
    cuts = []
    acc = 0
    for s in IN_SIZES[:-1]:
        acc += s
        cuts.append(acc)
    return jnp.split(z, cuts, axis=-1)


def causal_depthwise_conv(x, w, b):
    c = x.shape[-1]
    y = lax.conv_general_dilated(x, w[:, None, :].astype(x.dtype), window_strides=(1,),
                                 padding=[(CONV_WIDTH - 1, 0)],
                                 dimension_numbers=('NWC', 'WIO', 'NWC'),
                                 feature_group_count=c)
    return y + b.astype(x.dtype)


def t5_bucket(dist):
    n = jnp.maximum(dist, 0)
    max_exact = REL_BUCKETS // 2
    log_ratio = jnp.log(jnp.maximum(n, max_exact).astype(jnp.float32) / max_exact) / math.log(REL_MAX_DIST / max_exact)
    large = max_exact + (log_ratio * (REL_BUCKETS - max_exact)).astype(jnp.int32)
    large = jnp.minimum(large, REL_BUCKETS - 1)
    return jnp.where(n < max_exact, n, large)


def gather_blocks(blocks, idx):
    return jax.vmap(jax.vmap(lambda bl, ix: bl[ix]))(blocks, idx)


def mlstm_chunkwise(q, k, v, i_pre, f_pre):
    B, S, H, DH = q.shape
    L = M_CHUNK
    NC = S // L
    f32 = jnp.float32

    def chunk(a):
        a = a.astype(f32).reshape((B, NC, L, H) + a.shape[3:])
        return jnp.moveaxis(a, (1, 3), (0, 2))

    qc = chunk(q)
    kc = chunk(k) * (DH ** -0.5)
    vc = chunk(v)
    log_f = chunk(jax.nn.log_sigmoid(f_pre.astype(f32)))
    log_i = chunk(i_pre)
    bcum = jnp.cumsum(log_f, axis=-1)
    causal = jnp.tril(jnp.ones((L, L), dtype=bool))

    def step(carry, xs):
        C, n, m = carry
        qq, kk, vv, bb, ii = xs
        dlog = jnp.where(causal, bb[..., :, None] - bb[..., None, :] + ii[..., None, :], -jnp.inf)
        a = bb + m[..., None]
        mt = jnp.maximum(a, jnp.max(dlog, axis=-1))
        w_intra = jnp.exp(dlog - mt[..., None])
        w_state = jnp.exp(a - mt)
        s = jnp.einsum('bhtd,bhsd->bhts', qq, kk) * w_intra
        num = jnp.einsum('bhts,bhse->bhte', s, vv) + w_state[..., None] * jnp.einsum('bhtd,bhde->bhte', qq, C)
        den = jnp.sum(s, axis=-1) + w_state * jnp.einsum('bhtd,bhd->bht', qq, n)
        h = num / jnp.maximum(jnp.abs(den), jnp.exp(-mt))[..., None]
        b_last = bb[..., -1]
        g = b_last[..., None] - bb + ii
        m_new = jnp.maximum(b_last + m, jnp.max(g, axis=-1))
        decay = jnp.exp(b_last + m - m_new)
        ws = jnp.exp(g - m_new[..., None])
        C = decay[..., None, None] * C + jnp.einsum('bhs,bhsd,bhse->bhde', ws, kk, vv)
        n = decay[..., None] * n + jnp.einsum('bhs,bhsd->bhd', ws, kk)
        return (C, n, m_new), h

    init = (jnp.zeros((B, H, DH, DH), f32), jnp.zeros((B, H, DH), f32), jnp.zeros((B, H), f32))
    _, h = lax.scan(step, init, (qc, kc, vc, bcum, log_i))
    return jnp.moveaxis(h, (0, 2), (1, 3)).reshape(B, S, H, DH)


def moba_attention(q, k, v, rel_bias):
    B, H, S, DA = q.shape
    BS = MOBA_BLOCK
    NB = -(-S // BS)
    SP = NB * BS
    f32 = jnp.float32
    pad = [(0, 0), (0, 0), (0, SP - S), (0, 0)]
    q = jnp.pad(q, pad)
    k = jnp.pad(k, pad)
    v = jnp.pad(v, pad)
    kb = k.reshape(B, H, NB, BS, DA)
    vb = v.reshape(B, H, NB, BS, DA)
    k_mean = jnp.mean(kb.astype(f32), axis=3)
    gate = jnp.einsum('bhsd,bhnd->bhsn', q.astype(f32), k_mean)
    q_blk = jnp.arange(SP) // BS
    past = jnp.arange(NB)[None, :] < q_blk[:, None]
    gate = jnp.where(past, gate, -jnp.inf)
    n_sel = min(MOBA_TOPK, NB)
    _, sel = lax.top_k(gate, n_sel)
    sel_ok = sel < q_blk[:, None]
    nqc = SP // Q_CHUNK

    def by_chunk(a):
        return jnp.moveaxis(a.reshape((B, H, nqc, Q_CHUNK) + a.shape[3:]), 2, 0)

    bias_hb = rel_bias.T
    head_ix = jnp.arange(H)[None, :, None, None, None]
    scale = DA ** -0.5

    def attend(args):
        qq, ss, ok, start = args
        qpos = start + jnp.arange(Q_CHUNK)
        blk = start // BS
        own_k = lax.dynamic_index_in_dim(kb, blk, axis=2, keepdims=False)
        own_v = lax.dynamic_index_in_dim(vb, blk, axis=2, keepdims=False)
        own_pos = blk * BS + jnp.arange(BS)
        sel_k = gather_blocks(kb, ss)
        sel_v = gather_blocks(vb, ss)
        sel_pos = ss[..., None] * BS + jnp.arange(BS)
        l_own = jnp.einsum('bhqd,bhkd->bhqk', qq, own_k).astype(f32) * scale \
            + bias_hb[:, t5_bucket(qpos[:, None] - own_pos[None, :])]
        l_own = jnp.where(own_pos[None, :] <= qpos[:, None], l_own, -jnp.inf)
        l_sel = jnp.einsum('bhqd,bhqnkd->bhqnk', qq, sel_k).astype(f32) * scale \
            + bias_hb[head_ix, t5_bucket(qpos[None, None, :, None, None] - sel_pos)]
        l_sel = jnp.where(ok[..., None], l_sel, -jnp.inf)
        logits = jnp.concatenate([l_own, l_sel.reshape(B, H, Q_CHUNK, n_sel * BS)], axis=-1)
        p = jax.nn.softmax(logits, axis=-1).astype(v.dtype)
        p_own = p[..., :BS]
        p_sel = p[..., BS:].reshape(B, H, Q_CHUNK, n_sel, BS)
        return jnp.einsum('bhqk,bhkd->bhqd', p_own, own_v) + jnp.einsum('bhqnk,bhqnkd->bhqd', p_sel, sel_v)

    starts = jnp.arange(nqc, dtype=jnp.int32) * Q_CHUNK
    out = lax.map(attend, (by_chunk(q), by_chunk(sel), by_chunk(sel_ok), starts))
    out = jnp.moveaxis(out, 0, 2).reshape(B, H, SP, DA)
    return out[:, :, :S]


def hybrid_mixer(h, rel_bias, w_in, conv_w, conv_b, igate_b, fgate_b, mlstm_norm_g,
                 q_norm_g, k_norm_g, w_branch_a, w_branch_b, w_out):
    B, S, _ = h.shape
    z = jnp.einsum('bsd,dp->bsp', h, w_in)
    mq, mk, mv, mo, mi, mf, aq, ak, av, ga, gb = split_columns(z)
    qk = jax.nn.silu(causal_depthwise_conv(jnp.concatenate([mq, mk], axis=-1), conv_w, conv_b))
    mq, mk = jnp.split(qk, 2, axis=-1)

    def heads_m(t):
        return t.reshape(B, S, M_HEADS, M_HEAD_DIM)

    h_cell = mlstm_chunkwise(heads_m(mq), heads_m(mk), heads_m(mv),
                             mi.astype(jnp.float32) + igate_b, mf.astype(jnp.float32) + fgate_b)
    h_cell = jax.nn.sigmoid(heads_m(mo).astype(jnp.float32)) * h_cell
    h_a = rms_norm(h_cell, mlstm_norm_g.reshape(M_HEADS, M_HEAD_DIM)).astype(h.dtype).reshape(B, S, M_WIDTH)
    y_a = h_a @ w_branch_a
    def heads_a(t):
        return t.reshape(B, S, A_HEADS, A_HEAD_DIM).transpose(0, 2, 1, 3)

    qa = rms_norm(heads_a(aq), q_norm_g)
    ka = rms_norm(heads_a(ak), k_norm_g)
    o_b = moba_attention(qa, ka, heads_a(av), rel_bias)
    y_b = o_b.transpose(0, 2, 1, 3).reshape(B, S, A_WIDTH) @ w_branch_b
    y = jax.nn.sigmoid(ga) * y_a + jax.nn.sigmoid(gb) * y_b
    return y @ w_out


def swiglu(h, wg, wu, wd):
    return (jax.nn.silu(h @ wg) * (h @ wu)) @ wd


def moe_swiglu(h, router_w, wg, wu, wd):
    logits = (h @ router_w).astype(jnp.float32)
    top_val, top_idx = lax.top_k(logits, TOP_K)
    top_w = jax.nn.softmax(top_val, axis=-1)
    gates = jnp.sum(jax.nn.one_hot(top_idx, N_EXPERTS, dtype=jnp.float32) * top_w[..., None], axis=-2).astype(h.dtype)
    y = jnp.zeros_like(h)
    for e in range(N_EXPERTS):
        y = y + gates[..., e:e + 1] * swiglu(h, wg[e], wu[e], wd[e])
    return y


def setup_inputs(seed: int = 0) -> dict:
    key = jax.random.key(seed)
    ks = jax.random.split(key, 24)
    f32 = jnp.float32

    def w(k, shape, fan_in):
        return jax.random.normal(k, shape, f32) * (fan_in ** -0.5)

    def gain(k, shape):
        return 1.0 + 0.05 * jax.random.normal(k, shape, f32)

    return {
        'x': jax.random.normal(ks[0], (BATCH, SEQ, D_MODEL), f32),
        'rel_bias': 0.1 * jax.random.normal(ks[1], (REL_BUCKETS, A_HEADS), f32),
        'mix_norm_g': gain(ks[2], (DEPTH, D_MODEL)),
        'w_in': w(ks[3], (DEPTH, D_MODEL, IN_WIDTH), D_MODEL),
        'conv_w': w(ks[4], (DEPTH, CONV_WIDTH, 2 * M_WIDTH), CONV_WIDTH),
        'conv_b': 0.02 * jax.random.normal(ks[5], (DEPTH, 2 * M_WIDTH), f32),
        'igate_b': 0.1 * jax.random.normal(ks[6], (DEPTH, M_HEADS), f32),
        'fgate_b': 3.0 + 3.0 * jax.random.uniform(ks[7], (DEPTH, M_HEADS), f32),
        'mlstm_norm_g': gain(ks[8], (DEPTH, M_WIDTH)),
        'q_norm_g': gain(ks[9], (DEPTH, A_HEAD_DIM)),
        'k_norm_g': gain(ks[10], (DEPTH, A_HEAD_DIM)),
        'w_branch_a': w(ks[11], (DEPTH, M_WIDTH, D_MODEL), M_WIDTH),
        'w_branch_b': w(ks[12], (DEPTH, A_WIDTH, D_MODEL), A_WIDTH),
        'w_out': w(ks[13], (DEPTH, D_MODEL, D_MODEL), D_MODEL),
        'ffn_norm_g': gain(ks[14], (DEPTH, D_MODEL)),
        'dense_w_gate': w(ks[15], (N_DENSE, D_MODEL, D_FF_DENSE), D_MODEL),
        'dense_w_up': w(ks[16], (N_DENSE, D_MODEL, D_FF_DENSE), D_MODEL),
        'dense_w_down': w(ks[17], (N_DENSE, D_FF_DENSE, D_MODEL), D_FF_DENSE),
        'router_w': w(ks[18], (N_MOE, D_MODEL, N_EXPERTS), D_MODEL),
        'expert_w_gate': w(ks[19], (N_MOE, N_EXPERTS, D_MODEL, D_FF_EXPERT), D_MODEL),
        'expert_w_up': w(ks[20], (N_MOE, N_EXPERTS, D_MODEL, D_FF_EXPERT), D_MODEL),
        'expert_w_down': w(ks[21], (N_MOE, N_EXPERTS, D_FF_EXPERT, D_MODEL), D_FF_EXPERT),
    }


def reference(x, rel_bias, mix_norm_g, w_in, conv_w, conv_b, igate_b, fgate_b, mlstm_norm_g,
              q_norm_g, k_norm_g, w_branch_a, w_branch_b, w_out, ffn_norm_g,
              dense_w_gate, dense_w_up, dense_w_down, router_w,
              expert_w_gate, expert_w_up, expert_w_down):
    for layer in range(DEPTH):
        h = rms_norm(x, mix_norm_g[layer])
        x = x + hybrid_mixer(h, rel_bias, w_in[layer], conv_w[layer], conv_b[layer], igate_b[layer],
                             fgate_b[layer], mlstm_norm_g[layer], q_norm_g[layer], k_norm_g[layer],
                             w_branch_a[layer], w_branch_b[layer], w_out[layer])
        h = rms_norm(x, ffn_norm_g[layer])
        j = layer // 2
        if layer % 2 == 0:
            x = x + swiglu(h, dense_w_gate[j], dense_w_up[j], dense_w_down[j])
        else:
            x = x + moe_swiglu(h, router_w[j], expert_w_gate[j], expert_w_up[j], expert_w_down[j])
    return x
```

```python
import functools
import math

import jax
import jax.numpy as jnp
from jax import lax
from jax.experimental import pallas as pl
from jax.experimental.pallas import tpu as pltpu

F32 = jnp.float32
BF16 = jnp.bfloat16
HIGHEST = lax.Precision.HIGHEST

D_MODEL = 1024
M_HEADS = 4
M_HEAD_DIM = 128
M_WIDTH = M_HEADS * M_HEAD_DIM
CONV_WIDTH = 4
A_HEADS = 8
A_HEAD_DIM = 64
A_WIDTH = A_HEADS * A_HEAD_DIM
MOBA_BLOCK = 256
MOBA_TOPK = 3
REL_BUCKETS = 32
REL_MAX_DIST = 1024
N_EXPERTS = 8
EPS = 1e-6

LANES = 128
SUBLANES = 8
NEG = -1e30
VMEM_LIMIT = 56 * 1024 * 1024

C_GA = 0
C_GB = D_MODEL
C_MQ = 2 * D_MODEL
C_MK = C_MQ + M_WIDTH
C_MV = C_MK + M_WIDTH
C_MO = C_MV + M_WIDTH
C_AK = C_MO + M_WIDTH
NAT_WIDTH = C_AK + A_WIDTH
N_BIAS_TILES = 6

PROJ_TM = 512
MLSTM_T = 512
MLSTM_CHUNK = 128
MERGE_TM = 512
FFN_TM = 512
FFN_CHUNK = 256
ROUTER_TM = 512
MOE_TM = 1024
MOE_TF = 896
SCATTER_R = 256


def _params(*sem):
    return pltpu.CompilerParams(dimension_semantics=sem, vmem_limit_bytes=VMEM_LIMIT)


def _const_spec(shape):
    nd = len(shape)
    return pl.BlockSpec(shape, lambda *_: (0,) * nd)


def _rms(x, g):
    return x * lax.rsqrt(jnp.mean(x * x, axis=-1, keepdims=True) + EPS) * g


def _in_proj_kernel(x_ref, g_ref, wn_ref, wt_ref, wg_ref, z_ref, zt_ref, gt_ref):
    hn = _rms(x_ref[...], g_ref[...]).astype(BF16)
    for c in range(0, NAT_WIDTH, 512):
        z_ref[:, c:c + 512] = jnp.dot(hn, wn_ref[:, c:c + 512], preferred_element_type=F32).astype(BF16)
    for c in range(0, 2 * A_WIDTH, 256):
        zt_ref[c:c + 256, :] = lax.dot_general(
            wt_ref[c:c + 256, :], hn, (((1,), (1,)), ((), ())), preferred_element_type=F32).astype(BF16)
    gt_ref[...] = jnp.dot(hn, wg_ref[...], preferred_element_type=F32)


def _in_proj(x2, g, wn, wt, wg):
    n = x2.shape[0]
    tm = PROJ_TM
    return pl.pallas_call(
        _in_proj_kernel,
        grid=(n // tm,),
        in_specs=[
            pl.BlockSpec((tm, D_MODEL), lambda i: (i, 0)),
            _const_spec((1, D_MODEL)),
            _const_spec((D_MODEL, NAT_WIDTH)),
            _const_spec((2 * A_WIDTH, D_MODEL)),
            _const_spec((D_MODEL, LANES)),
        ],
        out_specs=[
            pl.BlockSpec((tm, NAT_WIDTH), lambda i: (i, 0)),
            pl.BlockSpec((2 * A_WIDTH, tm), lambda i: (0, i)),
            pl.BlockSpec((tm, LANES), lambda i: (i, 0)),
        ],
        out_shape=[
            jax.ShapeDtypeStruct((n, NAT_WIDTH), BF16),
            jax.ShapeDtypeStruct((2 * A_WIDTH, n), BF16),
            jax.ShapeDtypeStruct((n, LANES), F32),
        ],
        compiler_params=_params("arbitrary"),
        name="in_proj",
    )(x2, g, wn, wt, wg)


def _mlstm_kernel(zq_ref, zk_ref, zv_ref, zo_ref, gt_ref, cw_ref, cb_ref, gb_ref, ng_ref, o_ref,
                  qk_buf, q_s, k_s, c_s, n_s, m_s, *, t_blk, chunk):
    L = chunk
    DH = M_HEAD_DIM

    @pl.when(pl.program_id(1) == 0)
    def _init():
        qk_buf[0:SUBLANES, :] = jnp.zeros((SUBLANES, 2 * M_WIDTH), F32)
        c_s[...] = jnp.zeros_like(c_s)
        n_s[...] = jnp.zeros_like(n_s)
        m_s[...] = jnp.zeros_like(m_s)

    qk_buf[SUBLANES:SUBLANES + t_blk, 0:M_WIDTH] = zq_ref[...].astype(F32)
    qk_buf[SUBLANES:SUBLANES + t_blk, M_WIDTH:] = zk_ref[...].astype(F32)
    acc = cb_ref[...] + cw_ref[CONV_WIDTH - 1:CONV_WIDTH, :] * qk_buf[SUBLANES:SUBLANES + t_blk, :]
    for j in range(CONV_WIDTH - 1):
        off = SUBLANES - (CONV_WIDTH - 1) + j
        acc = acc + cw_ref[j:j + 1, :] * qk_buf[off:off + t_blk, :]
    qk = acc * jax.nn.sigmoid(acc)
    qk_buf[0:SUBLANES, :] = qk_buf[t_blk:t_blk + SUBLANES, :]
    q_s[...] = qk[:, :M_WIDTH].astype(BF16)
    k_s[...] = qk[:, M_WIDTH:] * (DH ** -0.5)

    row = lax.broadcasted_iota(jnp.int32, (L, L), 0)
    col = lax.broadcasted_iota(jnp.int32, (L, L), 1)
    causal = col <= row
    tri = causal.astype(F32)
    lane = lax.broadcasted_iota(jnp.int32, (L, LANES), 1)

    def chunk_body(c, carry):
        r0 = pl.multiple_of(c * L, L)
        g_pre = gt_ref[pl.ds(r0, L), :] + gb_ref[...]
        log_f = jnp.minimum(g_pre, 0.0) - jnp.log1p(jnp.exp(-jnp.abs(g_pre)))
        bcum = jnp.dot(tri, log_f, precision=HIGHEST, preferred_element_type=F32)
        gm = jnp.where(lane < M_HEADS, g_pre, bcum)
        gm_t = gm.T
        for h in range(M_HEADS):
            hs = slice(h * DH, (h + 1) * DH)
            bb_col = gm[:, M_HEADS + h:M_HEADS + h + 1]
            ii_col = gm[:, h:h + 1]
            bb_row = gm_t[M_HEADS + h:M_HEADS + h + 1, :]
            ii_row = gm_t[h:h + 1, :]
            m_old = m_s[h:h + 1, 0:1]
            dlog = jnp.where(causal, bb_col - bb_row + ii_row, -jnp.inf)
            a = bb_col + m_old
            mt = jnp.maximum(a, jnp.max(dlog, axis=-1, keepdims=True))
            w_intra = jnp.exp(dlog - mt)
            w_state = jnp.exp(a - mt)
            qh = q_s[pl.ds(r0, L), hs]
            kf = k_s[pl.ds(r0, L), hs]
            vh = zv_ref[pl.ds(r0, L), hs]
            s = lax.dot_general(qh, kf.astype(BF16), (((1,), (1,)), ((), ())),
                                preferred_element_type=F32) * w_intra
            c_old = c_s[h]
            n_old = n_s[h:h + 1, :]
            num = jnp.dot(s.astype(BF16), vh, preferred_element_type=F32) \
                + w_state * jnp.dot(qh, c_old.astype(BF16), preferred_element_type=F32)
            den = jnp.sum(s, axis=-1, keepdims=True) \
                + w_state * jnp.sum(qh.astype(F32) * n_old, axis=-1, keepdims=True)
            h_t = num / jnp.maximum(jnp.abs(den), jnp.exp(-mt))
            b_last = bb_col[L - 1:L, :]
            g_col = b_last - bb_col + ii_col
            m_new = jnp.maximum(b_last + m_old, jnp.max(g_col, axis=0, keepdims=True))
            decay = jnp.exp(b_last + m_old - m_new)
            kw = kf * jnp.exp(g_col - m_new)
            c_s[h] = decay * c_old + jnp.dot(kw.T.astype(BF16), vh, preferred_element_type=F32)
            n_s[h:h + 1, :] = decay * n_old + jnp.sum(kw, axis=0, keepdims=True)
            m_s[h:h + 1, :] = jnp.broadcast_to(m_new, (1, LANES))
            hc = jax.nn.sigmoid(zo_ref[pl.ds(r0, L), hs].astype(F32)) * h_t
            o_ref[pl.ds(r0, L), hs] = _rms(hc, ng_ref[:, hs]).astype(BF16)
        return carry

    lax.fori_loop(0, t_blk // L, chunk_body, 0)


def _mlstm(z, gates, conv_w, conv_b, gate_b, norm_g, batch, seq):
    n = batch * seq
    t = MLSTM_T
    nt = seq // t
    row_blk = lambda b, s: b * nt + s
    zspec = lambda cb: pl.BlockSpec((t, M_WIDTH), lambda b, s: (row_blk(b, s), cb))
    return pl.pallas_call(
        functools.partial(_mlstm_kernel, t_blk=t, chunk=MLSTM_CHUNK),
        grid=(batch, nt),
        in_specs=[
            zspec(C_MQ // M_WIDTH), zspec(C_MK // M_WIDTH), zspec(C_MV // M_WIDTH), zspec(C_MO // M_WIDTH),
            pl.BlockSpec((t, LANES), lambda b, s: (row_blk(b, s), 0)),
            _const_spec((CONV_WIDTH, 2 * M_WIDTH)),
            _const_spec((1, 2 * M_WIDTH)),
            _const_spec((1, LANES)),
            _const_spec((1, M_WIDTH)),
        ],
        out_specs=pl.BlockSpec((t, M_WIDTH), lambda b, s: (row_blk(b, s), 0)),
        out_shape=jax.ShapeDtypeStruct((n, M_WIDTH), BF16),
        scratch_shapes=[
            pltpu.VMEM((t + SUBLANES, 2 * M_WIDTH), F32),
            pltpu.VMEM((t, M_WIDTH), BF16),
            pltpu.VMEM((t, M_WIDTH), F32),
            pltpu.VMEM((M_HEADS, M_HEAD_DIM, M_HEAD_DIM), F32),
            pltpu.VMEM((SUBLANES, M_HEAD_DIM), F32),
            pltpu.VMEM((SUBLANES, LANES), F32),
        ],
        compiler_params=_params("arbitrary", "arbitrary"),
        name="mlstm",
    )(z, z, z, z, gates, conv_w, conv_b, gate_b, norm_g)


def _moba_kernel(qt_ref, k_ref, vt_ref, bias_ref, qg_ref, kg_ref, o_ref,
                 kn_s, vt_s, kmean_s, selb_s, *, n_blocks):
    BS = MOBA_BLOCK
    DA = A_HEAD_DIM
    qb = pl.program_id(2)
    lane_k = lax.broadcasted_iota(jnp.int32, (BS, LANES), 1)
    head0_k = lane_k < DA

    @pl.when(qb == 0)
    def _prep():
        for j in range(n_blocks):
            kf = k_ref[j * BS:(j + 1) * BS, :].astype(F32)
            k2 = kf * kf
            s0 = jnp.sum(jnp.where(head0_k, k2, 0.0), axis=-1, keepdims=True)
            s1 = jnp.sum(jnp.where(head0_k, 0.0, k2), axis=-1, keepdims=True)
            inv = jnp.where(head0_k, lax.rsqrt(s0 / DA + EPS), lax.rsqrt(s1 / DA + EPS))
            kn = kf * inv * kg_ref[...]
            kn_s[j * BS:(j + 1) * BS, :] = kn.astype(BF16)
            kmean_s[j:j + 1, :] = jnp.mean(kn, axis=0, keepdims=True)
            vt_s[j] = vt_ref[:, j * BS:(j + 1) * BS]

    qf = qt_ref[...].astype(F32)
    tq = qf.shape[1]
    sub_q = lax.broadcasted_iota(jnp.int32, qf.shape, 0)
    head0_q = sub_q < DA
    q2 = qf * qf
    ss0 = jnp.sum(jnp.where(head0_q, q2, 0.0), axis=0, keepdims=True)
    ss1 = jnp.sum(jnp.where(head0_q, 0.0, q2), axis=0, keepdims=True)
    qn = qf * jnp.where(head0_q, lax.rsqrt(ss0 / DA + EPS), lax.rsqrt(ss1 / DA + EPS)) * qg_ref[...]

    blk = lax.broadcasted_iota(jnp.int32, (n_blocks, tq), 0)
    past = blk < qb
    lane_m = lax.broadcasted_iota(jnp.int32, (n_blocks, LANES), 1)
    kmean = kmean_s[...]
    q_heads = []
    for h in range(2):
        hmask_m = (lane_m < DA) if h == 0 else (lane_m >= DA)
        gate = jnp.dot(jnp.where(hmask_m, kmean, 0.0), qn, precision=HIGHEST, preferred_element_type=F32)
        gate = jnp.where(past, gate, -jnp.inf)
        rank = jnp.zeros((n_blocks, tq), jnp.int32)
        for j2 in range(n_blocks):
            other = gate[j2:j2 + 1, :]
            beats = (other > gate) | ((other == gate) & (j2 < blk))
            rank = rank + beats.astype(jnp.int32)
        sel = past & (rank < MOBA_TOPK)
        selb_s[h] = jnp.where(sel, 0.0, NEG)
        hmask_q = head0_q if h == 0 else jnp.logical_not(head0_q)
        q_heads.append(jnp.where(hmask_q, qn * (DA ** -0.5), 0.0).astype(BF16))

    def scores(h, j, tile):
        kj = kn_s[pl.ds(pl.multiple_of(j * BS, BS), BS), :]
        st = jnp.dot(kj, q_heads[h], preferred_element_type=F32)
        return st + bias_ref[h, tile]

    def pv(h, j, p):
        full = jnp.dot(vt_s[j], p.astype(BF16), preferred_element_type=F32)
        return full[h * DA:(h + 1) * DA, :]

    state = []
    for h in range(2):
        st = scores(h, qb, 0)
        m = jnp.max(st, axis=0, keepdims=True)
        p = jnp.exp(st - m)
        state += [m, jnp.sum(p, axis=0, keepdims=True), pv(h, qb, p)]

    def past_body(j, state):
        tile = jnp.minimum(qb - j, N_BIAS_TILES - 1)
        new = []
        for h in range(2):
            m_old, l_old, acc_old = state[3 * h:3 * h + 3]
            st = scores(h, j, tile) + selb_s[h, pl.ds(j, 1), :]
            m_new = jnp.maximum(m_old, jnp.max(st, axis=0, keepdims=True))
            alpha = jnp.exp(m_old - m_new)
            p = jnp.exp(st - m_new)
            new += [m_new, alpha * l_old + jnp.sum(p, axis=0, keepdims=True), alpha * acc_old + pv(h, j, p)]
        return tuple(new)

    state = lax.fori_loop(0, qb, past_body, tuple(state))
    out_t = jnp.concatenate([state[2] / state[1], state[5] / state[4]], axis=0)
    o_ref[...] = out_t.T.astype(BF16)


def _moba(z, zt, bias_tiles, qg_col, kg_row, batch, seq):
    n = batch * seq
    nb = seq // MOBA_BLOCK
    tq = MOBA_BLOCK
    hp = A_HEADS // 2
    w2 = 2 * A_HEAD_DIM
    return pl.pallas_call(
        functools.partial(_moba_kernel, n_blocks=nb),
        grid=(batch, hp, nb),
        in_specs=[
            pl.BlockSpec((w2, tq), lambda b, p, q: (p, b * nb + q)),
            pl.BlockSpec((seq, w2), lambda b, p, q: (b, C_AK // w2 + p)),
            pl.BlockSpec((w2, seq), lambda b, p, q: (hp + p, b)),
            pl.BlockSpec((2, N_BIAS_TILES, MOBA_BLOCK, MOBA_BLOCK), lambda b, p, q: (p, 0, 0, 0)),
            _const_spec((w2, 1)),
            _const_spec((1, w2)),
        ],
        out_specs=pl.BlockSpec((tq, w2), lambda b, p, q: (b * nb + q, p)),
        out_shape=jax.ShapeDtypeStruct((n, A_WIDTH), BF16),
        scratch_shapes=[
            pltpu.VMEM((seq, w2), BF16),
            pltpu.VMEM((nb, w2, MOBA_BLOCK), BF16),
            pltpu.VMEM((nb, w2), F32),
            pltpu.VMEM((2, nb, tq), F32),
        ],
        compiler_params=_params("arbitrary", "arbitrary", "arbitrary"),
        name="moba",
    )(zt, z, zt, bias_tiles, qg_col, kg_row)


def _t5_bucket(dist):
    n = jnp.maximum(dist, 0)
    max_exact = REL_BUCKETS // 2
    log_ratio = jnp.log(jnp.maximum(n, max_exact).astype(F32) / max_exact) / math.log(REL_MAX_DIST / max_exact)
    large = max_exact + (log_ratio * (REL_BUCKETS - max_exact)).astype(jnp.int32)
    large = jnp.minimum(large, REL_BUCKETS - 1)
    return jnp.where(n < max_exact, n, large)


def _moba_bias_tiles(rel_bias):
    assert (N_BIAS_TILES - 1) * MOBA_BLOCK - (MOBA_BLOCK - 1) >= REL_MAX_DIST
    tk = jnp.arange(MOBA_BLOCK)[:, None]
    tq = jnp.arange(MOBA_BLOCK)[None, :]
    tiles = []
    for diff in range(N_BIAS_TILES):
        b = rel_bias.T[:, _t5_bucket(diff * MOBA_BLOCK + tq - tk)]
        if diff == 0:
            b = jnp.where(tk <= tq, b, NEG)
        tiles.append(b)
    return jnp.stack(tiles, axis=1).astype(F32)


def _merge_kernel(x_ref, ha_ref, ob_ref, ga_ref, gb_ref, wa_ref, wb_ref, wo_ref, o_ref):
    ya = jnp.dot(ha_ref[...], wa_ref[...], preferred_element_type=F32)
    yb = jnp.dot(ob_ref[...], wb_ref[...], preferred_element_type=F32)
    y = jax.nn.sigmoid(ga_ref[...].astype(F32)) * ya + jax.nn.sigmoid(gb_ref[...].astype(F32)) * yb
    o_ref[...] = x_ref[...] + jnp.dot(y.astype(BF16), wo_ref[...], preferred_element_type=F32)


def _merge(x2, ha, ob, z, wa, wb, wo):
    n = x2.shape[0]
    tm = MERGE_TM
    return pl.pallas_call(
        _merge_kernel,
        grid=(n // tm,),
        in_specs=[
            pl.BlockSpec((tm, D_MODEL), lambda i: (i, 0)),
            pl.BlockSpec((tm, M_WIDTH), lambda i: (i, 0)),
            pl.BlockSpec((tm, A_WIDTH), lambda i: (i, 0)),
            pl.BlockSpec((tm, D_MODEL), lambda i: (i, C_GA // D_MODEL)),
            pl.BlockSpec((tm, D_MODEL), lambda i: (i, C_GB // D_MODEL)),
            _const_spec((M_WIDTH, D_MODEL)),
            _const_spec((A_WIDTH, D_MODEL)),
            _const_spec((D_MODEL, D_MODEL)),
        ],
        out_specs=pl.BlockSpec((tm, D_MODEL), lambda i: (i, 0)),
        out_shape=jax.ShapeDtypeStruct((n, D_MODEL), F32),
        compiler_params=_params("arbitrary"),
        name="merge",
    )(x2, ha, ob, z, z, wa, wb, wo)


def _ffn_kernel(x_ref, g_ref, wg_ref, wu_ref, wd_ref, o_ref, *, d_ff):
    x = x_ref[...]
    hn = _rms(x, g_ref[...]).astype(BF16)
    acc = x
    for c in range(0, d_ff, FFN_CHUNK):
        g = jnp.dot(hn, wg_ref[:, c:c + FFN_CHUNK], preferred_element_type=F32)
        u = jnp.dot(hn, wu_ref[:, c:c + FFN_CHUNK], preferred_element_type=F32)
        a = (g * jax.nn.sigmoid(g) * u).astype(BF16)
        acc = acc + jnp.dot(a, wd_ref[c:c + FFN_CHUNK, :], preferred_element_type=F32)
    o_ref[...] = acc


def _ffn(x2, g, wg, wu, wd):
    n = x2.shape[0]
    d_ff = wg.shape[1]
    assert d_ff % FFN_CHUNK == 0
    tm = FFN_TM
    return pl.pallas_call(
        functools.partial(_ffn_kernel, d_ff=d_ff),
        grid=(n // tm,),
        in_specs=[
            pl.BlockSpec((tm, D_MODEL), lambda i: (i, 0)),
            _const_spec((1, D_MODEL)),
            _const_spec((D_MODEL, d_ff)),
            _const_spec((D_MODEL, d_ff)),
            _const_spec((d_ff, D_MODEL)),
        ],
        out_specs=pl.BlockSpec((tm, D_MODEL), lambda i: (i, 0)),
        out_shape=jax.ShapeDtypeStruct((n, D_MODEL), F32),
        compiler_params=_params("arbitrary"),
        name="ffn_dense",
    )(x2, g, wg, wu, wd)


META_W0, META_W1, META_E0, META_E1, META_R0, META_R1 = range(6)


def _router_kernel(x_ref, g_ref, rw_ref, h_ref, meta_ref, cnt_ref, carry_s):
    @pl.when(pl.program_id(0) == 0)
    def _init():
        carry_s[...] = jnp.zeros_like(carry_s)

    h = _rms(x_ref[...], g_ref[...])
    h_ref[...] = h
    tm = h.shape[0]
    logits = jnp.dot(h, rw_ref[...], precision=HIGHEST, preferred_element_type=F32)
    lane = lax.broadcasted_iota(jnp.int32, (tm, LANES), 1)
    lg = jnp.where(lane < N_EXPERTS, logits, -jnp.inf)
    m1 = jnp.max(lg, axis=-1, keepdims=True)
    i1 = jnp.min(jnp.where(lg == m1, lane, LANES), axis=-1, keepdims=True)
    lg2 = jnp.where(lane == i1, -jnp.inf, lg)
    m2 = jnp.max(lg2, axis=-1, keepdims=True)
    i2 = jnp.min(jnp.where(lg2 == m2, lane, LANES), axis=-1, keepdims=True)
    e = jnp.exp(m2 - m1)
    w1 = 1.0 / (1.0 + e)
    w2 = e / (1.0 + e)
    hit1 = lane == i1
    hit2 = lane == i2
    onehot = jnp.where(hit1 | hit2, 1.0, 0.0)
    r = lax.broadcasted_iota(jnp.int32, (tm, tm), 0)
    c = lax.broadcasted_iota(jnp.int32, (tm, tm), 1)
    before = jnp.where(c < r, 1.0, 0.0).astype(BF16)
    carry = carry_s[0:1, :]
    pref = jnp.dot(before, onehot.astype(BF16), preferred_element_type=F32) + carry
    r1 = jnp.sum(jnp.where(hit1, pref, 0.0), axis=-1, keepdims=True)
    r2 = jnp.sum(jnp.where(hit2, pref, 0.0), axis=-1, keepdims=True)
    new_carry = carry + jnp.sum(onehot, axis=0, keepdims=True)
    carry_s[...] = jnp.broadcast_to(new_carry, carry_s.shape)
    cnt_ref[...] = jnp.broadcast_to(new_carry, cnt_ref.shape)
    meta = jnp.zeros((tm, LANES), F32)
    for idx, val in ((META_W0, w1), (META_W1, w2), (META_E0, i1.astype(F32)), (META_E1, i2.astype(F32)),
                     (META_R0, r1), (META_R1, r2)):
        meta = jnp.where(lane == idx, val, meta)
    meta_ref[...] = meta


def _router(x2, g, rw):
    n = x2.shape[0]
    tm = ROUTER_TM
    return pl.pallas_call(
        _router_kernel,
        grid=(n // tm,),
        in_specs=[
            pl.BlockSpec((tm, D_MODEL), lambda i: (i, 0)),
            _const_spec((1, D_MODEL)),
            _const_spec((D_MODEL, LANES)),
        ],
        out_specs=[
            pl.BlockSpec((tm, D_MODEL), lambda i: (i, 0)),
            pl.BlockSpec((tm, LANES), lambda i: (i, 0)),
            _const_spec((SUBLANES, LANES)),
        ],
        out_shape=[
            jax.ShapeDtypeStruct((n, D_MODEL), F32),
            jax.ShapeDtypeStruct((n, LANES), F32),
            jax.ShapeDtypeStruct((SUBLANES, LANES), F32),
        ],
        scratch_shapes=[pltpu.VMEM((SUBLANES, LANES), F32)],
        compiler_params=_params("arbitrary"),
        name="moe_router",
    )(x2, g, rw)


def _row_copy(src_ref, src_row, dst_ref, dst_row, sem):
    return pltpu.make_async_copy(src_ref.at[pl.ds(src_row, 1)], dst_ref.at[pl.ds(dst_row, 1)], sem)


def _scatter_kernel(pos_ref, h_ref, xs_in_ref, xs_ref, sem, *, rows):
    del xs_in_ref

    def issue(r, carry):
        _row_copy(h_ref, r, xs_ref, pos_ref[0, 0, r], sem).start()
        _row_copy(h_ref, r, xs_ref, pos_ref[0, 0, rows + r], sem).start()
        return carry

    lax.fori_loop(0, rows, issue, 0)

    def drain(r, carry):
        _row_copy(h_ref, r, xs_ref, pos_ref[0, 0, r], sem).wait()
        _row_copy(h_ref, r, xs_ref, pos_ref[0, 0, rows + r], sem).wait()
        return carry

    lax.fori_loop(0, rows, drain, 0)


def _scatter(pos, h, xs_zero):
    n = h.shape[0]
    r = SCATTER_R
    return pl.pallas_call(
        functools.partial(_scatter_kernel, rows=r),
        grid=(n // r,),
        in_specs=[
            pl.BlockSpec((1, 1, 2 * r), lambda i: (i, 0, 0), memory_space=pltpu.SMEM),
            pl.BlockSpec((r, D_MODEL), lambda i: (i, 0)),
            pl.BlockSpec(memory_space=pl.ANY),
        ],
        out_specs=pl.BlockSpec(memory_space=pl.ANY),
        out_shape=jax.ShapeDtypeStruct(xs_zero.shape, F32),
        scratch_shapes=[pltpu.SemaphoreType.DMA(())],
        input_output_aliases={2: 0},
        compiler_params=_params("arbitrary"),
        name="moe_scatter",
    )(pos, h, xs_zero)


def _moe_kernel(blk_ref, exp_ref, valid_ref, xs_ref, wg_ref, wu_ref, wd_ref, y_ref, xb_s, acc_s):
    del blk_ref, exp_ref
    i = pl.program_id(0)
    f = pl.program_id(1)

    @pl.when(valid_ref[i] == 1)
    def _tile():
        @pl.when(f == 0)
        def _cast():
            xb_s[...] = xs_ref[...].astype(BF16)

        xb = xb_s[...]
        g = jnp.dot(xb, wg_ref[0], preferred_element_type=F32)
        u = jnp.dot(xb, wu_ref[0], preferred_element_type=F32)
        a = (g * jax.nn.sigmoid(g) * u).astype(BF16)
        d = jnp.dot(a, wd_ref[0], preferred_element_type=F32)

        @pl.when(f == 0)
        def _first():
            acc_s[...] = d

        @pl.when(f > 0)
        def _rest():
            acc_s[...] += d

        @pl.when(f == pl.num_programs(1) - 1)
        def _out():
            y_ref[...] = acc_s[...]


def _moe_experts(tile_blk, tile_exp, tile_valid, xs, wg, wu, wd):
    rows = xs.shape[0]
    d_ff = wg.shape[2]
    tm, tf = MOE_TM, MOE_TF
    assert rows % tm == 0 and d_ff % tf == 0
    grid_spec = pltpu.PrefetchScalarGridSpec(
        num_scalar_prefetch=3,
        grid=(rows // tm, d_ff // tf),
        in_specs=[
            pl.BlockSpec((tm, D_MODEL), lambda i, f, blk, ex, va: (blk[i], 0)),
            pl.BlockSpec((1, D_MODEL, tf), lambda i, f, blk, ex, va: (ex[i], 0, f)),
            pl.BlockSpec((1, D_MODEL, tf), lambda i, f, blk, ex, va: (ex[i], 0, f)),
            pl.BlockSpec((1, tf, D_MODEL), lambda i, f, blk, ex, va: (ex[i], f, 0)),
        ],
        out_specs=pl.BlockSpec((tm, D_MODEL), lambda i, f, blk, ex, va: (blk[i], 0)),
        scratch_shapes=[pltpu.VMEM((tm, D_MODEL), BF16), pltpu.VMEM((tm, D_MODEL), F32)],
    )
    return pl.pallas_call(
        _moe_kernel,
        grid_spec=grid_spec,
        out_shape=jax.ShapeDtypeStruct((rows, D_MODEL), F32),
        input_output_aliases={3: 0},
        compiler_params=_params("arbitrary", "arbitrary"),
        name="moe_experts",
    )(tile_blk, tile_exp, tile_valid, xs, wg, wu, wd)


def _combine_kernel(pos_ref, x_ref, meta_ref, ys_ref, o_ref, ybuf, sem, *, rows):
    def issue(r, carry):
        _row_copy(ys_ref, pos_ref[0, 0, r], ybuf, r, sem).start()
        _row_copy(ys_ref, pos_ref[0, 0, rows + r], ybuf, rows + r, sem).start()
        return carry

    lax.fori_loop(0, rows, issue, 0)

    def drain(r, carry):
        _row_copy(ys_ref, pos_ref[0, 0, r], ybuf, r, sem).wait()
        _row_copy(ys_ref, pos_ref[0, 0, rows + r], ybuf, rows + r, sem).wait()
        return carry

    lax.fori_loop(0, rows, drain, 0)
    meta = meta_ref[...]
    w0 = meta[:, META_W0:META_W0 + 1]
    w1 = meta[:, META_W1:META_W1 + 1]
    o_ref[...] = x_ref[...] + w0 * ybuf[0:rows, :] + w1 * ybuf[rows:2 * rows, :]


def _combine(pos, x2, meta, ys):
    n = x2.shape[0]
    r = SCATTER_R
    return pl.pallas_call(
        functools.partial(_combine_kernel, rows=r),
        grid=(n // r,),
        in_specs=[
            pl.BlockSpec((1, 1, 2 * r), lambda i: (i, 0, 0), memory_space=pltpu.SMEM),
            pl.BlockSpec((r, D_MODEL), lambda i: (i, 0)),
            pl.BlockSpec((r, LANES), lambda i: (i, 0)),
            pl.BlockSpec(memory_space=pl.ANY),
        ],
        out_specs=pl.BlockSpec((r, D_MODEL), lambda i: (i, 0)),
        out_shape=jax.ShapeDtypeStruct((n, D_MODEL), F32),
        scratch_shapes=[pltpu.VMEM((2 * r, D_MODEL), F32), pltpu.SemaphoreType.DMA(())],
        compiler_params=_params("arbitrary"),
        name="moe_combine",
    )(pos, x2, meta, ys)


def _moe(x2, g, router_w, wg, wu, wd):
    n = x2.shape[0]
    tm = MOE_TM
    rw = jnp.zeros((D_MODEL, LANES), F32).at[:, :N_EXPERTS].set(router_w)
    h, meta, cnt = _router(x2, g, rw)
    counts = cnt[0, :N_EXPERTS].astype(jnp.int32)
    padded = ((counts + tm - 1) // tm) * tm
    ends = jnp.cumsum(padded)
    offs = ends - padded
    n_tiles = (2 * n) // tm + N_EXPERTS
    e0 = meta[:, META_E0].astype(jnp.int32)
    e1 = meta[:, META_E1].astype(jnp.int32)
    pos0 = offs[e0] + meta[:, META_R0].astype(jnp.int32)
    pos1 = offs[e1] + meta[:, META_R1].astype(jnp.int32)
    r = SCATTER_R
    pos = jnp.concatenate([pos0.reshape(n // r, 1, r), pos1.reshape(n // r, 1, r)], axis=-1)
    starts = jnp.arange(n_tiles, dtype=jnp.int32) * tm
    n_valid = ends[-1] // tm
    tile_valid = (starts < ends[-1]).astype(jnp.int32)
    tile_blk = jnp.minimum(jnp.arange(n_tiles, dtype=jnp.int32), n_valid - 1)
    tile_exp = jnp.sum((tile_blk[:, None] * tm >= ends[None, :]).astype(jnp.int32), axis=1)
    xs = _scatter(pos, h, jnp.zeros((n_tiles * tm, D_MODEL), F32))
    ys = _moe_experts(tile_blk, tile_exp, tile_valid, xs, wg, wu, wd)
    return _combine(pos, x2, meta, ys)


def _mixer(x2, batch, seq, bias_tiles, norm_g, w_in, conv_w, conv_b, igate_b, fgate_b, mlstm_norm_g,
           q_norm_g, k_norm_g, w_branch_a, w_branch_b, w_out):
    sizes = (M_WIDTH, M_WIDTH, M_WIDTH, M_WIDTH, M_HEADS, M_HEADS, A_WIDTH, A_WIDTH, A_WIDTH, D_MODEL, D_MODEL)
    cuts = [0]
    for s in sizes:
        cuts.append(cuts[-1] + s)
    mq, mk, mv, mo, mi, mf, aq, ak, av, ga, gb = [w_in[:, cuts[i]:cuts[i + 1]] for i in range(len(sizes))]
    wn = jnp.concatenate([ga, gb, mq, mk, mv, mo, ak], axis=1).astype(BF16)
    wt = jnp.concatenate([aq, av], axis=1).T.astype(BF16)
    wgate = jnp.zeros((D_MODEL, LANES), F32).at[:, :M_HEADS].set(mi).at[:, M_HEADS:2 * M_HEADS].set(mf).astype(BF16)
    gate_b = jnp.zeros((1, LANES), F32).at[0, :M_HEADS].set(igate_b).at[0, M_HEADS:2 * M_HEADS].set(fgate_b)
    z, zt, gates = _in_proj(x2, norm_g.reshape(1, D_MODEL), wn, wt, wgate)
    ha = _mlstm(z, gates, conv_w, conv_b.reshape(1, -1), gate_b, mlstm_norm_g.reshape(1, -1), batch, seq)
    qg_col = jnp.tile(q_norm_g, 2).reshape(2 * A_HEAD_DIM, 1)
    kg_row = jnp.tile(k_norm_g, 2).reshape(1, 2 * A_HEAD_DIM)
    ob = _moba(z, zt, bias_tiles, qg_col, kg_row, batch, seq)
    return _merge(x2, ha, ob, z, w_branch_a.astype(BF16), w_branch_b.astype(BF16), w_out.astype(BF16))


def kernel(x, rel_bias, mix_norm_g, w_in, conv_w, conv_b, igate_b, fgate_b, mlstm_norm_g, q_norm_g, k_norm_g,
           w_branch_a, w_branch_b, w_out, ffn_norm_g, dense_w_gate, dense_w_up, dense_w_down, router_w,
           expert_w_gate, expert_w_up, expert_w_down):
    batch, seq, d = x.shape
    depth = w_in.shape[0]
    assert d == D_MODEL and seq % MLSTM_T == 0 and seq % MOBA_BLOCK == 0
    assert (batch * seq) % MOE_TM == 0
    x2 = x.reshape(batch * seq, d)
    bias_tiles = _moba_bias_tiles(rel_bias)
    for layer in range(depth):
        x2 = _mixer(x2, batch, seq, bias_tiles, mix_norm_g[layer], w_in[layer], conv_w[layer], conv_b[layer],
                    igate_b[layer], fgate_b[layer], mlstm_norm_g[layer], q_norm_g[layer], k_norm_g[layer],
                    w_branch_a[layer], w_branch_b[layer], w_out[layer])
        g = ffn_norm_g[layer].reshape(1, d)
        j = layer // 2
        if layer % 2 == 0:
            x2 = _ffn(x2, g, dense_w_gate[j].astype(BF16), dense_w_up[j].astype(BF16), dense_w_down[j].astype(BF16))
        else:
            x2 = _moe(x2, g, router_w[j], expert_w_gate[j].astype(BF16), expert_w_up[j].astype(BF16),
                      expert_w_down[j].astype(BF16))
    return x2.reshape(batch, seq, d)
```

```python
import functools
import math

import jax
import jax.numpy as jnp
from jax import lax
from jax.experimental import pallas as pl
from jax.experimental.pallas import tpu as pltpu

F32 = jnp.float32
BF16 = jnp.bfloat16
HIGHEST = lax.Precision.HIGHEST

D_MODEL = 1024
M_HEADS = 4
M_HEAD_DIM = 128
M_WIDTH = M_HEADS * M_HEAD_DIM
CONV_WIDTH = 4
A_HEADS = 8
A_HEAD_DIM = 64
A_WIDTH = A_HEADS * A_HEAD_DIM
MOBA_BLOCK = 256
MOBA_TOPK = 3
REL_BUCKETS = 32
REL_MAX_DIST = 1024
N_EXPERTS = 8
EPS = 1e-6

LANES = 128
SUBLANES = 8
NEG = -1e30
VMEM_LIMIT = 56 * 1024 * 1024

C_GA = 0
C_GB = D_MODEL
C_MQ = 2 * D_MODEL
C_MK = C_MQ + M_WIDTH
C_MV = C_MK + M_WIDTH
C_MO = C_MV + M_WIDTH
C_AK = C_MO + M_WIDTH
NAT_WIDTH = C_AK + A_WIDTH
N_BIAS_TILES = 6

PROJ_TM = 512
MLSTM_T = 512
MLSTM_CHUNK = 128
MERGE_TM = 512
FFN_TM = 512
FFN_CHUNK = 256
ROUTER_TM = 512
MOE_TM = 1024
MOE_TF = 896
SCATTER_R = 256


def _params(*sem):
    return pltpu.CompilerParams(dimension_semantics=sem, vmem_limit_bytes=VMEM_LIMIT)


def _const_spec(shape):
    nd = len(shape)
    return pl.BlockSpec(shape, lambda *_: (0,) * nd)


def _rms(x, g):
    return x * lax.rsqrt(jnp.mean(x * x, axis=-1, keepdims=True) + EPS) * g


def _in_proj_kernel(x_ref, g_ref, wn_ref, wt_ref, wg_ref, z_ref, zt_ref, gt_ref):
    hn = _rms(x_ref[...], g_ref[...]).astype(BF16)
    for c in range(0, NAT_WIDTH, 512):
        z_ref[:, c:c + 512] = jnp.dot(hn, wn_ref[:, c:c + 512], preferred_element_type=F32).astype(BF16)
    for c in range(0, 2 * A_WIDTH, 256):
        zt_ref[c:c + 256, :] = lax.dot_general(
            wt_ref[c:c + 256, :], hn, (((1,), (1,)), ((), ())), preferred_element_type=F32).astype(BF16)
    gt_ref[...] = jnp.dot(hn, wg_ref[...], preferred_element_type=F32)


def _in_proj(x2, g, wn, wt, wg):
    n = x2.shape[0]
    tm = PROJ_TM
    return pl.pallas_call(
        _in_proj_kernel,
        grid=(n // tm,),
        in_specs=[
            pl.BlockSpec((tm, D_MODEL), lambda i: (i, 0)),
            _const_spec((1, D_MODEL)),
            _const_spec((D_MODEL, NAT_WIDTH)),
            _const_spec((2 * A_WIDTH, D_MODEL)),
            _const_spec((D_MODEL, LANES)),
        ],
        out_specs=[
            pl.BlockSpec((tm, NAT_WIDTH), lambda i: (i, 0)),
            pl.BlockSpec((2 * A_WIDTH, tm), lambda i: (0, i)),
            pl.BlockSpec((tm, LANES), lambda i: (i, 0)),
        ],
        out_shape=[
            jax.ShapeDtypeStruct((n, NAT_WIDTH), BF16),
            jax.ShapeDtypeStruct((2 * A_WIDTH, n), BF16),
            jax.ShapeDtypeStruct((n, LANES), F32),
        ],
        compiler_params=_params("arbitrary"),
        name="in_proj",
    )(x2, g, wn, wt, wg)


def _mlstm_kernel(zq_ref, zk_ref, zv_ref, zo_ref, gt_ref, cw_ref, cb_ref, gb_ref, ng_ref, o_ref,
                  qk_buf, q_s, k_s, c_s, n_s, m_s, *, t_blk, chunk):
    L = chunk
    DH = M_HEAD_DIM

    @pl.when(pl.program_id(1) == 0)
    def _init():
        qk_buf[0:SUBLANES, :] = jnp.zeros((SUBLANES, 2 * M_WIDTH), F32)
        c_s[...] = jnp.zeros_like(c_s)
        n_s[...] = jnp.zeros_like(n_s)
        m_s[...] = jnp.zeros_like(m_s)

    qk_buf[SUBLANES:SUBLANES + t_blk, 0:M_WIDTH] = zq_ref[...].astype(F32)
    qk_buf[SUBLANES:SUBLANES + t_blk, M_WIDTH:] = zk_ref[...].astype(F32)
    acc = cb_ref[...] + cw_ref[CONV_WIDTH - 1:CONV_WIDTH, :] * qk_buf[SUBLANES:SUBLANES + t_blk, :]
    for j in range(CONV_WIDTH - 1):
        off = SUBLANES - (CONV_WIDTH - 1) + j
        acc = acc + cw_ref[j:j + 1, :] * qk_buf[off:off + t_blk, :]
    qk = acc * jax.nn.sigmoid(acc)
    qk_buf[0:SUBLANES, :] = qk_buf[t_blk:t_blk + SUBLANES, :]
    q_s[...] = qk[:, :M_WIDTH].astype(BF16)
    k_s[...] = qk[:, M_WIDTH:] * (DH ** -0.5)

    row = lax.broadcasted_iota(jnp.int32, (L, L), 0)
    col = lax.broadcasted_iota(jnp.int32, (L, L), 1)
    causal = col <= row
    tri = causal.astype(F32)
    lane = lax.broadcasted_iota(jnp.int32, (L, LANES), 1)

    def chunk_body(c, carry):
        r0 = pl.multiple_of(c * L, L)
        g_pre = gt_ref[pl.ds(r0, L), :] + gb_ref[...]
        log_f = jnp.minimum(g_pre, 0.0) - jnp.log1p(jnp.exp(-jnp.abs(g_pre)))
        bcum = jnp.dot(tri, log_f, precision=HIGHEST, preferred_element_type=F32)
        gm = jnp.where(lane < M_HEADS, g_pre, bcum)
        gm_t = gm.T
        for h in range(M_HEADS):
            hs = slice(h * DH, (h + 1) * DH)
            bb_col = gm[:, M_HEADS + h:M_HEADS + h + 1]
            ii_col = gm[:, h:h + 1]
            bb_row = gm_t[M_HEADS + h:M_HEADS + h + 1, :]
            ii_row = gm_t[h:h + 1, :]
            m_old = m_s[h:h + 1, 0:1]
            dlog = jnp.where(causal, bb_col - bb_row + ii_row, -jnp.inf)
            a = bb_col + m_old
            mt = jnp.maximum(a, jnp.max(dlog, axis=-1, keepdims=True))
            w_intra = jnp.exp(dlog - mt)
            w_state = jnp.exp(a - mt)
            qh = q_s[pl.ds(r0, L), hs]
            kf = k_s[pl.ds(r0, L), hs]
            vh = zv_ref[pl.ds(r0, L), hs]
            s = lax.dot_general(qh, kf.astype(BF16), (((1,), (1,)), ((), ())),
                                preferred_element_type=F32) * w_intra
            c_old = c_s[h]
            n_old = n_s[h:h + 1, :]
            num = jnp.dot(s.astype(BF16), vh, preferred_element_type=F32) \
                + w_state * jnp.dot(qh, c_old.astype(BF16), preferred_element_type=F32)
            den = jnp.sum(s, axis=-1, keepdims=True) \
                + w_state * jnp.sum(qh.astype(F32) * n_old, axis=-1, keepdims=True)
            h_t = num / jnp.maximum(jnp.abs(den), jnp.exp(-mt))
            b_last = bb_col[L - 1:L, :]
            g_col = b_last - bb_col + ii_col
            m_new = jnp.maximum(b_last + m_old, jnp.max(g_col, axis=0, keepdims=True))
            decay = jnp.exp(b_last + m_old - m_new)
            kw = kf * jnp.exp(g_col - m_new)
            c_s[h] = decay * c_old + jnp.dot(kw.T.astype(BF16), vh, preferred_element_type=F32)
            n_s[h:h + 1, :] = decay * n_old + jnp.sum(kw, axis=0, keepdims=True)
            m_s[h:h + 1, :] = jnp.broadcast_to(m_new, (1, LANES))
            hc = jax.nn.sigmoid(zo_ref[pl.ds(r0, L), hs].astype(F32)) * h_t
            o_ref[pl.ds(r0, L), hs] = _rms(hc, ng_ref[:, hs]).astype(BF16)
        return carry

    lax.fori_loop(0, t_blk // L, chunk_body, 0)


def _mlstm(z, gates, conv_w, conv_b, gate_b, norm_g, batch, seq):
    n = batch * seq
    t = MLSTM_T
    nt = seq // t
    row_blk = lambda b, s: b * nt + s
    zspec = lambda cb: pl.BlockSpec((t, M_WIDTH), lambda b, s: (row_blk(b, s), cb))
    return pl.pallas_call(
        functools.partial(_mlstm_kernel, t_blk=t, chunk=MLSTM_CHUNK),
        grid=(batch, nt),
        in_specs=[
            zspec(C_MQ // M_WIDTH), zspec(C_MK // M_WIDTH), zspec(C_MV // M_WIDTH), zspec(C_MO // M_WIDTH),
            pl.BlockSpec((t, LANES), lambda b, s: (row_blk(b, s), 0)),
            _const_spec((CONV_WIDTH, 2 * M_WIDTH)),
            _const_spec((1, 2 * M_WIDTH)),
            _const_spec((1, LANES)),
            _const_spec((1, M_WIDTH)),
        ],
        out_specs=pl.BlockSpec((t, M_WIDTH), lambda b, s: (row_blk(b, s), 0)),
        out_shape=jax.ShapeDtypeStruct((n, M_WIDTH), BF16),
        scratch_shapes=[
            pltpu.VMEM((t + SUBLANES, 2 * M_WIDTH), F32),
            pltpu.VMEM((t, M_WIDTH), BF16),
            pltpu.VMEM((t, M_WIDTH), F32),
            pltpu.VMEM((M_HEADS, M_HEAD_DIM, M_HEAD_DIM), F32),
            pltpu.VMEM((SUBLANES, M_HEAD_DIM), F32),
            pltpu.VMEM((SUBLANES, LANES), F32),
        ],
        compiler_params=_params("arbitrary", "arbitrary"),
        name="mlstm",
    )(z, z, z, z, gates, conv_w, conv_b, gate_b, norm_g)


MOBA_V_ROWS = A_HEAD_DIM + 16
MOBA_ROWB_ROWS = 24
FAST_SOFTMAX_MIN_DENOM = 1e-25


def _moba_kernel(qt_ref, k_ref, vt_ref, bias_ref, bound_ref, qg_ref, kg_ref, o_ref,
                 kn_s, vt_s, kmean_s, selb_s, rowb_s, *, n_blocks):
    BS = MOBA_BLOCK
    DA = A_HEAD_DIM
    qb = pl.program_id(2)
    lane_k = lax.broadcasted_iota(jnp.int32, (BS, LANES), 1)
    head0_k = lane_k < DA

    @pl.when(qb == 0)
    def _prep():
        ones_row = jnp.where(lax.broadcasted_iota(jnp.int32, (16, BS), 0) == 0, 1.0, 0.0).astype(BF16)
        for j in range(n_blocks):
            kf = k_ref[j * BS:(j + 1) * BS, :].astype(F32)
            k2 = kf * kf
            s0 = jnp.sum(jnp.where(head0_k, k2, 0.0), axis=-1, keepdims=True)
            s1 = jnp.sum(jnp.where(head0_k, 0.0, k2), axis=-1, keepdims=True)
            inv = jnp.where(head0_k, lax.rsqrt(s0 / DA + EPS), lax.rsqrt(s1 / DA + EPS))
            kn = kf * inv * kg_ref[...]
            kn_s[j * BS:(j + 1) * BS, :] = kn.astype(BF16)
            kmean_s[j:j + 1, :] = jnp.mean(kn, axis=0, keepdims=True)
            for h in range(2):
                vt_s[j, h, 0:DA, :] = vt_ref[h * DA:(h + 1) * DA, j * BS:(j + 1) * BS]
                vt_s[j, h, DA:MOBA_V_ROWS, :] = ones_row

    qf = qt_ref[...].astype(F32)
    tq = qf.shape[1]
    sub_q = lax.broadcasted_iota(jnp.int32, qf.shape, 0)
    head0_q = sub_q < DA
    q2 = qf * qf
    ss0 = jnp.sum(jnp.where(head0_q, q2, 0.0), axis=0, keepdims=True)
    ss1 = jnp.sum(jnp.where(head0_q, 0.0, q2), axis=0, keepdims=True)
    qn = qf * jnp.where(head0_q, lax.rsqrt(ss0 / DA + EPS), lax.rsqrt(ss1 / DA + EPS)) * qg_ref[...]

    blk = lax.broadcasted_iota(jnp.int32, (n_blocks, tq), 0)
    past = blk < qb
    lane_m = lax.broadcasted_iota(jnp.int32, (n_blocks, LANES), 1)
    kmean = kmean_s[...]
    q_heads = []
    for h in range(2):
        hmask_m = (lane_m < DA) if h == 0 else (lane_m >= DA)
        gate = jnp.dot(jnp.where(hmask_m, kmean, 0.0), qn, precision=HIGHEST, preferred_element_type=F32)
        gate = jnp.where(past, gate, -jnp.inf)
        rank = jnp.zeros((n_blocks, tq), jnp.int32)
        for j2 in range(n_blocks):
            other = gate[j2:j2 + 1, :]
            beats = (other > gate) | ((other == gate) & (j2 < blk))
            rank = rank + beats.astype(jnp.int32)
        sel = past & (rank < MOBA_TOPK)
        selb_s[h] = jnp.where(sel, 0.0, NEG)
        hmask_q = head0_q if h == 0 else jnp.logical_not(head0_q)
        q_heads.append(jnp.where(hmask_q, qn * (DA ** -0.5), 0.0).astype(BF16))

    def scores(h, j, tile):
        kj = kn_s[pl.ds(pl.multiple_of(j * BS, BS), BS), :]
        st = jnp.dot(kj, q_heads[h], preferred_element_type=F32)
        return st + bias_ref[h, tile]

    def pv(h, j, p):
        full = jnp.dot(vt_s[j, h], p, preferred_element_type=F32)
        return full[0:DA, :], full[DA:DA + 1, :]

    bound = [bound_ref[h, :, 0:1] for h in range(2)]
    for h in range(2):
        rowb_s[h, 0:1, :] = jnp.broadcast_to(-bound[h], (1, tq))
        rowb_s[h, 1:n_blocks + 1, :] = selb_s[h] - bound[h]
        rowb_s[h, n_blocks + 1:, :] = jnp.full((MOBA_ROWB_ROWS - n_blocks - 1, tq), NEG, F32)

    def item_block(k):
        return jnp.where(k <= 0, qb, jnp.minimum(k - 1, n_blocks - 1))

    def item_tile(k):
        return jnp.where(k <= 0, 0, jnp.clip(qb - (k - 1), 0, N_BIAS_TILES - 1))

    def fixed_stabiliser():
        def item_scores(i):
            j1 = item_block(i)
            t1 = item_tile(i)
            return [scores(h, j1, t1) + rowb_s[h, pl.ds(i, 1), :] for h in range(2)]

        def body(i, carry):
            st, p, acc, l = carry
            j3 = item_block(i - 2)
            new_acc, new_l = [], []
            for h in range(2):
                a, s = pv(h, j3, p[h])
                new_acc.append(acc[h] + a)
                new_l.append(l[h] + s)
            new_p = [jnp.exp(st[h]).astype(BF16) for h in range(2)]
            return item_scores(i), new_p, new_acc, new_l

        p0 = [jnp.exp(st).astype(BF16) for st in item_scores(0)]
        init = (item_scores(1), p0, [jnp.zeros((DA, tq), F32)] * 2, [jnp.zeros((1, tq), F32)] * 2)
        _, _, acc, l = lax.fori_loop(2, qb + 3, body, init)
        return jnp.concatenate([acc[0] / l[0], acc[1] / l[1]], axis=0), jnp.minimum(jnp.min(l[0]), jnp.min(l[1]))

    def online():
        state = []
        for h in range(2):
            st = scores(h, qb, 0)
            m = jnp.max(st, axis=0, keepdims=True)
            acc, l = pv(h, qb, jnp.exp(st - m).astype(BF16))
            state += [m, l, acc]

        def past_body(j, state):
            tile = jnp.minimum(qb - j, N_BIAS_TILES - 1)
            new = []
            for h in range(2):
                m_old, l_old, acc_old = state[3 * h:3 * h + 3]
                st = scores(h, j, tile) + selb_s[h, pl.ds(j, 1), :]
                m_new = jnp.maximum(m_old, jnp.max(st, axis=0, keepdims=True))
                alpha = jnp.exp(m_old - m_new)
                acc, l = pv(h, j, jnp.exp(st - m_new).astype(BF16))
                new += [m_new, alpha * l_old + l, alpha * acc_old + acc]
            return tuple(new)

        state = lax.fori_loop(0, qb, past_body, tuple(state))
        return jnp.concatenate([state[2] / state[1], state[5] / state[4]], axis=0)

    out_fast, l_min = fixed_stabiliser()
    out_t = lax.cond(l_min >= FAST_SOFTMAX_MIN_DENOM, lambda: out_fast, online)
    o_ref[...] = out_t.T.astype(BF16)


def _moba_logit_bound(rel_bias, q_norm_g, k_norm_g):
    qk = A_HEAD_DIM * jnp.max(jnp.abs(q_norm_g)) * jnp.max(jnp.abs(k_norm_g)) * (A_HEAD_DIM ** -0.5) * 1.02
    b = qk + jnp.max(rel_bias, axis=0)
    return jnp.broadcast_to(b[:, None, None], (A_HEADS, 1, LANES)).astype(F32)


def _moba(z, zt, bias_tiles, bound, qg_col, kg_row, batch, seq):
    n = batch * seq
    nb = seq // MOBA_BLOCK
    tq = MOBA_BLOCK
    hp = A_HEADS // 2
    w2 = 2 * A_HEAD_DIM
    return pl.pallas_call(
        functools.partial(_moba_kernel, n_blocks=nb),
        grid=(batch, hp, nb),
        in_specs=[
            pl.BlockSpec((w2, tq), lambda b, p, q: (p, b * nb + q)),
            pl.BlockSpec((seq, w2), lambda b, p, q: (b, C_AK // w2 + p)),
            pl.BlockSpec((w2, seq), lambda b, p, q: (hp + p, b)),
            pl.BlockSpec((2, N_BIAS_TILES, MOBA_BLOCK, MOBA_BLOCK), lambda b, p, q: (p, 0, 0, 0)),
            pl.BlockSpec((2, 1, LANES), lambda b, p, q: (p, 0, 0)),
            _const_spec((w2, 1)),
            _const_spec((1, w2)),
        ],
        out_specs=pl.BlockSpec((tq, w2), lambda b, p, q: (b * nb + q, p)),
        out_shape=jax.ShapeDtypeStruct((n, A_WIDTH), BF16),
        scratch_shapes=[
            pltpu.VMEM((seq, w2), BF16),
            pltpu.VMEM((nb, 2, MOBA_V_ROWS, MOBA_BLOCK), BF16),
            pltpu.VMEM((nb, w2), F32),
            pltpu.VMEM((2, nb, tq), F32),
            pltpu.VMEM((2, MOBA_ROWB_ROWS, tq), F32),
        ],
        compiler_params=_params("arbitrary", "arbitrary", "arbitrary"),
        name="moba",
    )(zt, z, zt, bias_tiles, bound, qg_col, kg_row)


def _t5_bucket(dist):
    n = jnp.maximum(dist, 0)
    max_exact = REL_BUCKETS // 2
    log_ratio = jnp.log(jnp.maximum(n, max_exact).astype(F32) / max_exact) / math.log(REL_MAX_DIST / max_exact)
    large = max_exact + (log_ratio * (REL_BUCKETS - max_exact)).astype(jnp.int32)
    large = jnp.minimum(large, REL_BUCKETS - 1)
    return jnp.where(n < max_exact, n, large)


def _bias_tiles_kernel(rb_ref, bucket_ref, o_ref):
    bucket = bucket_ref[0]
    hit = [bucket == b for b in range(REL_BUCKETS)]
    tk = lax.broadcasted_iota(jnp.int32, bucket.shape, 0)
    tq = lax.broadcasted_iota(jnp.int32, bucket.shape, 1)
    masked = (tk > tq) & (pl.program_id(0) == 0)
    for h in range(A_HEADS):
        acc = jnp.zeros(bucket.shape, F32)
        for b in range(REL_BUCKETS):
            acc = jnp.where(hit[b], rb_ref[b, h], acc)
        o_ref[h, 0] = jnp.where(masked, NEG, acc)


def _moba_bias_tiles(rel_bias):
    assert (N_BIAS_TILES - 1) * MOBA_BLOCK - (MOBA_BLOCK - 1) >= REL_MAX_DIST
    tk = jnp.arange(MOBA_BLOCK)[None, :, None]
    tq = jnp.arange(MOBA_BLOCK)[None, None, :]
    diff = jnp.arange(N_BIAS_TILES)[:, None, None]
    bucket = _t5_bucket(diff * MOBA_BLOCK + tq - tk).astype(jnp.int32)
    return pl.pallas_call(
        _bias_tiles_kernel,
        grid=(N_BIAS_TILES,),
        in_specs=[
            pl.BlockSpec(memory_space=pltpu.SMEM),
            pl.BlockSpec((1, MOBA_BLOCK, MOBA_BLOCK), lambda t: (t, 0, 0)),
        ],
        out_specs=pl.BlockSpec((A_HEADS, 1, MOBA_BLOCK, MOBA_BLOCK), lambda t: (0, t, 0, 0)),
        out_shape=jax.ShapeDtypeStruct((A_HEADS, N_BIAS_TILES, MOBA_BLOCK, MOBA_BLOCK), F32),
        compiler_params=_params("arbitrary"),
        name="moba_bias_tiles",
    )(rel_bias.astype(F32), bucket)


def _merge_kernel(x_ref, ha_ref, ob_ref, ga_ref, gb_ref, wa_ref, wb_ref, wo_ref, o_ref):
    ya = jnp.dot(ha_ref[...], wa_ref[...], preferred_element_type=F32)
    yb = jnp.dot(ob_ref[...], wb_ref[...], preferred_element_type=F32)
    y = jax.nn.sigmoid(ga_ref[...].astype(F32)) * ya + jax.nn.sigmoid(gb_ref[...].astype(F32)) * yb
    o_ref[...] = x_ref[...] + jnp.dot(y.astype(BF16), wo_ref[...], preferred_element_type=F32)


def _merge(x2, ha, ob, z, wa, wb, wo):
    n = x2.shape[0]
    tm = MERGE_TM
    return pl.pallas_call(
        _merge_kernel,
        grid=(n // tm,),
        in_specs=[
            pl.BlockSpec((tm, D_MODEL), lambda i: (i, 0)),
            pl.BlockSpec((tm, M_WIDTH), lambda i: (i, 0)),
            pl.BlockSpec((tm, A_WIDTH), lambda i: (i, 0)),
            pl.BlockSpec((tm, D_MODEL), lambda i: (i, C_GA // D_MODEL)),
            pl.BlockSpec((tm, D_MODEL), lambda i: (i, C_GB // D_MODEL)),
            _const_spec((M_WIDTH, D_MODEL)),
            _const_spec((A_WIDTH, D_MODEL)),
            _const_spec((D_MODEL, D_MODEL)),
        ],
        out_specs=pl.BlockSpec((tm, D_MODEL), lambda i: (i, 0)),
        out_shape=jax.ShapeDtypeStruct((n, D_MODEL), F32),
        compiler_params=_params("arbitrary"),
        name="merge",
    )(x2, ha, ob, z, z, wa, wb, wo)


def _ffn_kernel(x_ref, g_ref, wg_ref, wu_ref, wd_ref, o_ref, *, d_ff):
    x = x_ref[...]
    hn = _rms(x, g_ref[...]).astype(BF16)
    acc = x
    for c in range(0, d_ff, FFN_CHUNK):
        g = jnp.dot(hn, wg_ref[:, c:c + FFN_CHUNK], preferred_element_type=F32)
        u = jnp.dot(hn, wu_ref[:, c:c + FFN_CHUNK], preferred_element_type=F32)
        a = (g * jax.nn.sigmoid(g) * u).astype(BF16)
        acc = acc + jnp.dot(a, wd_ref[c:c + FFN_CHUNK, :], preferred_element_type=F32)
    o_ref[...] = acc


def _ffn(x2, g, wg, wu, wd):
    n = x2.shape[0]
    d_ff = wg.shape[1]
    assert d_ff % FFN_CHUNK == 0
    tm = FFN_TM
    return pl.pallas_call(
        functools.partial(_ffn_kernel, d_ff=d_ff),
        grid=(n // tm,),
        in_specs=[
            pl.BlockSpec((tm, D_MODEL), lambda i: (i, 0)),
            _const_spec((1, D_MODEL)),
            _const_spec((D_MODEL, d_ff)),
            _const_spec((D_MODEL, d_ff)),
            _const_spec((d_ff, D_MODEL)),
        ],
        out_specs=pl.BlockSpec((tm, D_MODEL), lambda i: (i, 0)),
        out_shape=jax.ShapeDtypeStruct((n, D_MODEL), F32),
        compiler_params=_params("arbitrary"),
        name="ffn_dense",
    )(x2, g, wg, wu, wd)


META_W0, META_W1, META_E0, META_E1, META_R0, META_R1 = range(6)


def _router_kernel(x_ref, g_ref, rw_ref, h_ref, meta_ref, cnt_ref, carry_s):
    @pl.when(pl.program_id(0) == 0)
    def _init():
        carry_s[...] = jnp.zeros_like(carry_s)

    h = _rms(x_ref[...], g_ref[...])
    h_ref[...] = h
    tm = h.shape[0]
    logits = jnp.dot(h, rw_ref[...], precision=HIGHEST, preferred_element_type=F32)
    lane = lax.broadcasted_iota(jnp.int32, (tm, LANES), 1)
    lg = jnp.where(lane < N_EXPERTS, logits, -jnp.inf)
    m1 = jnp.max(lg, axis=-1, keepdims=True)
    i1 = jnp.min(jnp.where(lg == m1, lane, LANES), axis=-1, keepdims=True)
    lg2 = jnp.where(lane == i1, -jnp.inf, lg)
    m2 = jnp.max(lg2, axis=-1, keepdims=True)
    i2 = jnp.min(jnp.where(lg2 == m2, lane, LANES), axis=-1, keepdims=True)
    e = jnp.exp(m2 - m1)
    w1 = 1.0 / (1.0 + e)
    w2 = e / (1.0 + e)
    hit1 = lane == i1
    hit2 = lane == i2
    onehot = jnp.where(hit1 | hit2, 1.0, 0.0)
    r = lax.broadcasted_iota(jnp.int32, (tm, tm), 0)
    c = lax.broadcasted_iota(jnp.int32, (tm, tm), 1)
    before = jnp.where(c < r, 1.0, 0.0).astype(BF16)
    carry = carry_s[0:1, :]
    pref = jnp.dot(before, onehot.astype(BF16), preferred_element_type=F32) + carry
    r1 = jnp.sum(jnp.where(hit1, pref, 0.0), axis=-1, keepdims=True)
    r2 = jnp.sum(jnp.where(hit2, pref, 0.0), axis=-1, keepdims=True)
    new_carry = carry + jnp.sum(onehot, axis=0, keepdims=True)
    carry_s[...] = jnp.broadcast_to(new_carry, carry_s.shape)
    cnt_ref[...] = jnp.broadcast_to(new_carry, cnt_ref.shape)
    meta = jnp.zeros((tm, LANES), F32)
    for idx, val in ((META_W0, w1), (META_W1, w2), (META_E0, i1.astype(F32)), (META_E1, i2.astype(F32)),
                     (META_R0, r1), (META_R1, r2)):
        meta = jnp.where(lane == idx, val, meta)
    meta_ref[...] = meta


def _router(x2, g, rw):
    n = x2.shape[0]
    tm = ROUTER_TM
    return pl.pallas_call(
        _router_kernel,
        grid=(n // tm,),
        in_specs=[
            pl.BlockSpec((tm, D_MODEL), lambda i: (i, 0)),
            _const_spec((1, D_MODEL)),
            _const_spec((D_MODEL, LANES)),
        ],
        out_specs=[
            pl.BlockSpec((tm, D_MODEL), lambda i: (i, 0)),
            pl.BlockSpec((tm, LANES), lambda i: (i, 0)),
            _const_spec((SUBLANES, LANES)),
        ],
        out_shape=[
            jax.ShapeDtypeStruct((n, D_MODEL), F32),
            jax.ShapeDtypeStruct((n, LANES), F32),
            jax.ShapeDtypeStruct((SUBLANES, LANES), F32),
        ],
        scratch_shapes=[pltpu.VMEM((SUBLANES, LANES), F32)],
        compiler_params=_params("arbitrary"),
        name="moe_router",
    )(x2, g, rw)


def _row_copy(src_ref, src_row, dst_ref, dst_row, sem):
    return pltpu.make_async_copy(src_ref.at[pl.ds(src_row, 1)], dst_ref.at[pl.ds(dst_row, 1)], sem)


def _scatter_kernel(pos_ref, h_ref, xs_in_ref, xs_ref, sem, *, rows):
    del xs_in_ref

    def issue(r, carry):
        _row_copy(h_ref, r, xs_ref, pos_ref[0, 0, r], sem).start(priority=0)
        _row_copy(h_ref, r, xs_ref, pos_ref[0, 0, rows + r], sem).start(priority=1)
        return carry

    lax.fori_loop(0, rows, issue, 0)
    for _ in range(2):
        pltpu.make_async_copy(h_ref, xs_ref.at[pl.ds(0, rows)], sem).wait()


def _scatter(pos, h, xs_zero):
    n = h.shape[0]
    r = SCATTER_R
    return pl.pallas_call(
        functools.partial(_scatter_kernel, rows=r),
        grid=(n // r,),
        in_specs=[
            pl.BlockSpec((1, 1, 2 * r), lambda i: (i, 0, 0), memory_space=pltpu.SMEM),
            pl.BlockSpec((r, D_MODEL), lambda i: (i, 0)),
            pl.BlockSpec(memory_space=pl.ANY),
        ],
        out_specs=pl.BlockSpec(memory_space=pl.ANY),
        out_shape=jax.ShapeDtypeStruct(xs_zero.shape, F32),
        scratch_shapes=[pltpu.SemaphoreType.DMA(())],
        input_output_aliases={2: 0},
        compiler_params=_params("arbitrary"),
        name="moe_scatter",
    )(pos, h, xs_zero)


def _moe_kernel(blk_ref, exp_ref, valid_ref, xs_ref, wg_ref, wu_ref, wd_ref, y_ref, xb_s, acc_s):
    del blk_ref, exp_ref
    i = pl.program_id(0)
    f = pl.program_id(1)

    @pl.when(valid_ref[i] == 1)
    def _tile():
        @pl.when(f == 0)
        def _cast():
            xb_s[...] = xs_ref[...].astype(BF16)

        xb = xb_s[...]
        g = jnp.dot(xb, wg_ref[0], preferred_element_type=F32)
        u = jnp.dot(xb, wu_ref[0], preferred_element_type=F32)
        a = (g * jax.nn.sigmoid(g) * u).astype(BF16)
        d = jnp.dot(a, wd_ref[0], preferred_element_type=F32)

        @pl.when(f == 0)
        def _first():
            acc_s[...] = d

        @pl.when(f > 0)
        def _rest():
            acc_s[...] += d

        @pl.when(f == pl.num_programs(1) - 1)
        def _out():
            y_ref[...] = acc_s[...]


def _moe_experts(tile_blk, tile_exp, tile_valid, xs, wg, wu, wd):
    rows = xs.shape[0]
    d_ff = wg.shape[2]
    tm, tf = MOE_TM, MOE_TF
    assert rows % tm == 0 and d_ff % tf == 0
    grid_spec = pltpu.PrefetchScalarGridSpec(
        num_scalar_prefetch=3,
        grid=(rows // tm, d_ff // tf),
        in_specs=[
            pl.BlockSpec((tm, D_MODEL), lambda i, f, blk, ex, va: (blk[i], 0)),
            pl.BlockSpec((1, D_MODEL, tf), lambda i, f, blk, ex, va: (ex[i], 0, f)),
            pl.BlockSpec((1, D_MODEL, tf), lambda i, f, blk, ex, va: (ex[i], 0, f)),
            pl.BlockSpec((1, tf, D_MODEL), lambda i, f, blk, ex, va: (ex[i], f, 0)),
        ],
        out_specs=pl.BlockSpec((tm, D_MODEL), lambda i, f, blk, ex, va: (blk[i], 0)),
        scratch_shapes=[pltpu.VMEM((tm, D_MODEL), BF16), pltpu.VMEM((tm, D_MODEL), F32)],
    )
    return pl.pallas_call(
        _moe_kernel,
        grid_spec=grid_spec,
        out_shape=jax.ShapeDtypeStruct((rows, D_MODEL), F32),
        input_output_aliases={3: 0},
        compiler_params=_params("arbitrary", "arbitrary"),
        name="moe_experts",
    )(tile_blk, tile_exp, tile_valid, xs, wg, wu, wd)


def _combine_kernel(pos_ref, x_ref, meta_ref, ys_ref, o_ref, ybuf, sem, *, rows):
    def issue(r, carry):
        _row_copy(ys_ref, pos_ref[0, 0, r], ybuf, r, sem).start(priority=0)
        _row_copy(ys_ref, pos_ref[0, 0, rows + r], ybuf, rows + r, sem).start(priority=1)
        return carry

    lax.fori_loop(0, rows, issue, 0)
    pltpu.make_async_copy(ys_ref.at[pl.ds(0, 2 * rows)], ybuf, sem).wait()
    meta = meta_ref[...]
    w0 = meta[:, META_W0:META_W0 + 1]
    w1 = meta[:, META_W1:META_W1 + 1]
    o_ref[...] = x_ref[...] + w0 * ybuf[0:rows, :] + w1 * ybuf[rows:2 * rows, :]


def _combine(pos, x2, meta, ys):
    n = x2.shape[0]
    r = SCATTER_R
    return pl.pallas_call(
        functools.partial(_combine_kernel, rows=r),
        grid=(n // r,),
        in_specs=[
            pl.BlockSpec((1, 1, 2 * r), lambda i: (i, 0, 0), memory_space=pltpu.SMEM),
            pl.BlockSpec((r, D_MODEL), lambda i: (i, 0)),
            pl.BlockSpec((r, LANES), lambda i: (i, 0)),
            pl.BlockSpec(memory_space=pl.ANY),
        ],
        out_specs=pl.BlockSpec((r, D_MODEL), lambda i: (i, 0)),
        out_shape=jax.ShapeDtypeStruct((n, D_MODEL), F32),
        scratch_shapes=[pltpu.VMEM((2 * r, D_MODEL), F32), pltpu.SemaphoreType.DMA(())],
        compiler_params=_params("arbitrary"),
        name="moe_combine",
    )(pos, x2, meta, ys)


def _moe(x2, g, router_w, wg, wu, wd):
    n = x2.shape[0]
    tm = MOE_TM
    rw = jnp.zeros((D_MODEL, LANES), F32).at[:, :N_EXPERTS].set(router_w)
    h, meta, cnt = _router(x2, g, rw)
    counts = cnt[0, :N_EXPERTS].astype(jnp.int32)
    padded = ((counts + tm - 1) // tm) * tm
    ends = jnp.cumsum(padded)
    offs = ends - padded
    n_tiles = (2 * n) // tm + N_EXPERTS
    e0 = meta[:, META_E0].astype(jnp.int32)
    e1 = meta[:, META_E1].astype(jnp.int32)
    pos0 = offs[e0] + meta[:, META_R0].astype(jnp.int32)
    pos1 = offs[e1] + meta[:, META_R1].astype(jnp.int32)
    r = SCATTER_R
    pos = jnp.concatenate([pos0.reshape(n // r, 1, r), pos1.reshape(n // r, 1, r)], axis=-1)
    starts = jnp.arange(n_tiles, dtype=jnp.int32) * tm
    n_valid = ends[-1] // tm
    tile_valid = (starts < ends[-1]).astype(jnp.int32)
    tile_blk = jnp.minimum(jnp.arange(n_tiles, dtype=jnp.int32), n_valid - 1)
    tile_exp = jnp.sum((tile_blk[:, None] * tm >= ends[None, :]).astype(jnp.int32), axis=1)
    xs = _scatter(pos, h, jnp.zeros((n_tiles * tm, D_MODEL), F32))
    ys = _moe_experts(tile_blk, tile_exp, tile_valid, xs, wg, wu, wd)
    return _combine(pos, x2, meta, ys)


def _mixer(x2, batch, seq, rel_bias, bias_tiles, norm_g, w_in, conv_w, conv_b, igate_b, fgate_b, mlstm_norm_g,
           q_norm_g, k_norm_g, w_branch_a, w_branch_b, w_out):
    sizes = (M_WIDTH, M_WIDTH, M_WIDTH, M_WIDTH, M_HEADS, M_HEADS, A_WIDTH, A_WIDTH, A_WIDTH, D_MODEL, D_MODEL)
    cuts = [0]
    for s in sizes:
        cuts.append(cuts[-1] + s)
    mq, mk, mv, mo, mi, mf, aq, ak, av, ga, gb = [w_in[:, cuts[i]:cuts[i + 1]] for i in range(len(sizes))]
    wn = jnp.concatenate([ga, gb, mq, mk, mv, mo, ak], axis=1).astype(BF16)
    wt = jnp.concatenate([aq, av], axis=1).T.astype(BF16)
    wgate = jnp.zeros((D_MODEL, LANES), F32).at[:, :M_HEADS].set(mi).at[:, M_HEADS:2 * M_HEADS].set(mf).astype(BF16)
    gate_b = jnp.zeros((1, LANES), F32).at[0, :M_HEADS].set(igate_b).at[0, M_HEADS:2 * M_HEADS].set(fgate_b)
    z, zt, gates = _in_proj(x2, norm_g.reshape(1, D_MODEL), wn, wt, wgate)
    ha = _mlstm(z, gates, conv_w, conv_b.reshape(1, -1), gate_b, mlstm_norm_g.reshape(1, -1), batch, seq)
    qg_col = jnp.tile(q_norm_g, 2).reshape(2 * A_HEAD_DIM, 1)
    kg_row = jnp.tile(k_norm_g, 2).reshape(1, 2 * A_HEAD_DIM)
    ob = _moba(z, zt, bias_tiles, _moba_logit_bound(rel_bias, q_norm_g, k_norm_g), qg_col, kg_row, batch, seq)
    return _merge(x2, ha, ob, z, w_branch_a.astype(BF16), w_branch_b.astype(BF16), w_out.astype(BF16))


def kernel(x, rel_bias, mix_norm_g, w_in, conv_w, conv_b, igate_b, fgate_b, mlstm_norm_g, q_norm_g, k_norm_g,
           w_branch_a, w_branch_b, w_out, ffn_norm_g, dense_w_gate, dense_w_up, dense_w_down, router_w,
           expert_w_gate, expert_w_up, expert_w_down):
    batch, seq, d = x.shape
    depth = w_in.shape[0]
    assert d == D_MODEL and seq % MLSTM_T == 0 and seq % MOBA_BLOCK == 0
    assert (batch * seq) % MOE_TM == 0
    x2 = x.reshape(batch * seq, d)
    bias_tiles = _moba_bias_tiles(rel_bias)
    for layer in range(depth):
        x2 = _mixer(x2, batch, seq, rel_bias, bias_tiles, mix_norm_g[layer], w_in[layer], conv_w[layer], conv_b[layer],
                    igate_b[layer], fgate_b[layer], mlstm_norm_g[layer], q_norm_g[layer], k_norm_g[layer],
                    w_branch_a[layer], w_branch_b[layer], w_out[layer])
        g = ffn_norm_g[layer].reshape(1, d)
        j = layer // 2
        if layer % 2 == 0:
            x2 = _ffn(x2, g, dense_w_gate[j].astype(BF16), dense_w_up[j].astype(BF16), dense_w_down[j].astype(BF16))
        else:
            x2 = _moe(x2, g, router_w[j], expert_w_gate[j].astype(BF16), expert_w_up[j].astype(BF16),
                      expert_w_down[j].astype(BF16))
    return x2.reshape(batch, seq, d)
```

```python
import functools
import math

import jax
import jax.numpy as jnp
from jax import lax
from jax.experimental import pallas as pl
from jax.experimental.pallas import tpu as pltpu

F32 = jnp.float32
BF16 = jnp.bfloat16
HIGHEST = lax.Precision.HIGHEST

D_MODEL = 1024
M_HEADS = 4
M_HEAD_DIM = 128
M_WIDTH = M_HEADS * M_HEAD_DIM
CONV_WIDTH = 4
A_HEADS = 8
A_HEAD_DIM = 64
A_WIDTH = A_HEADS * A_HEAD_DIM
MOBA_BLOCK = 256
MOBA_TOPK = 3
REL_BUCKETS = 32
REL_MAX_DIST = 1024
N_EXPERTS = 8
EPS = 1e-6

LANES = 128
SUBLANES = 8
NEG = -1e30
VMEM_LIMIT = 56 * 1024 * 1024

C_GA = 0
C_GB = D_MODEL
C_MQ = 2 * D_MODEL
C_MK = C_MQ + M_WIDTH
C_MV = C_MK + M_WIDTH
C_MO = C_MV + M_WIDTH
C_AK = C_MO + M_WIDTH
NAT_WIDTH = C_AK + A_WIDTH
N_BIAS_TILES = 6

PROJ_TM = 512
MLSTM_T = 512
MLSTM_CHUNK = 128
MERGE_TM = 512
FFN_TM = 512
FFN_CHUNK = 256
ROUTER_TM = 512
MOE_TM = 1024
MOE_TF = 896
SCATTER_R = 256


def _params(*sem):
    return pltpu.CompilerParams(dimension_semantics=sem, vmem_limit_bytes=VMEM_LIMIT)


def _const_spec(shape):
    nd = len(shape)
    return pl.BlockSpec(shape, lambda *_: (0,) * nd)


def _rms(x, g):
    return x * lax.rsqrt(jnp.mean(x * x, axis=-1, keepdims=True) + EPS) * g


def _in_proj_kernel(x_ref, g_ref, wn_ref, wt_ref, wg_ref, z_ref, zt_ref, gt_ref):
    hn = _rms(x_ref[...], g_ref[...]).astype(BF16)
    for c in range(0, NAT_WIDTH, 512):
        z_ref[:, c:c + 512] = jnp.dot(hn, wn_ref[:, c:c + 512], preferred_element_type=F32).astype(BF16)
    for c in range(0, 2 * A_WIDTH, 256):
        zt_ref[c:c + 256, :] = lax.dot_general(
            wt_ref[c:c + 256, :], hn, (((1,), (1,)), ((), ())), preferred_element_type=F32).astype(BF16)
    gt_ref[...] = jnp.dot(hn, wg_ref[...], preferred_element_type=F32)


def _in_proj(x2, g, wn, wt, wg):
    n = x2.shape[0]
    tm = PROJ_TM
    return pl.pallas_call(
        _in_proj_kernel,
        grid=(n // tm,),
        in_specs=[
            pl.BlockSpec((tm, D_MODEL), lambda i: (i, 0)),
            _const_spec((1, D_MODEL)),
            _const_spec((D_MODEL, NAT_WIDTH)),
            _const_spec((2 * A_WIDTH, D_MODEL)),
            _const_spec((D_MODEL, LANES)),
        ],
        out_specs=[
            pl.BlockSpec((tm, NAT_WIDTH), lambda i: (i, 0)),
            pl.BlockSpec((2 * A_WIDTH, tm), lambda i: (0, i)),
            pl.BlockSpec((tm, LANES), lambda i: (i, 0)),
        ],
        out_shape=[
            jax.ShapeDtypeStruct((n, NAT_WIDTH), BF16),
            jax.ShapeDtypeStruct((2 * A_WIDTH, n), BF16),
            jax.ShapeDtypeStruct((n, LANES), F32),
        ],
        compiler_params=_params("arbitrary"),
        name="in_proj",
    )(x2, g, wn, wt, wg)


def _mlstm_kernel(zq_ref, zk_ref, zv_ref, zo_ref, gt_ref, cw_ref, cb_ref, gb_ref, ng_ref, o_ref,
                  qk_buf, q_s, k_s, c_s, m_s, *, t_blk, chunk):
    L = chunk
    DH = M_HEAD_DIM
    assert L == LANES and DH == LANES

    @pl.when(pl.program_id(1) == 0)
    def _init():
        qk_buf[0:SUBLANES, :] = jnp.zeros((SUBLANES, 2 * M_WIDTH), F32)
        c_s[...] = jnp.zeros_like(c_s)
        m_s[...] = jnp.zeros_like(m_s)

    qk_buf[SUBLANES:SUBLANES + t_blk, 0:M_WIDTH] = zq_ref[...].astype(F32)
    qk_buf[SUBLANES:SUBLANES + t_blk, M_WIDTH:] = zk_ref[...].astype(F32)
    acc = cb_ref[...] + cw_ref[CONV_WIDTH - 1:CONV_WIDTH, :] * qk_buf[SUBLANES:SUBLANES + t_blk, :]
    for j in range(CONV_WIDTH - 1):
        off = SUBLANES - (CONV_WIDTH - 1) + j
        acc = acc + cw_ref[j:j + 1, :] * qk_buf[off:off + t_blk, :]
    qk = acc * jax.nn.sigmoid(acc)
    qk_buf[0:SUBLANES, :] = qk_buf[t_blk:t_blk + SUBLANES, :]
    q_s[...] = qk[:, :M_WIDTH].astype(BF16)
    k_s[...] = qk[:, M_WIDTH:] * (DH ** -0.5)

    row = lax.broadcasted_iota(jnp.int32, (L, L), 0)
    col = lax.broadcasted_iota(jnp.int32, (L, L), 1)
    causal = col <= row
    tri = causal.astype(F32)
    lane = lax.broadcasted_iota(jnp.int32, (L, LANES), 1)
    ones_blk = jnp.ones((L, LANES), BF16)

    def chunk_body(c, carry):
        r0 = pl.multiple_of(c * L, L)
        g_pre = gt_ref[pl.ds(r0, L), :] + gb_ref[...]
        log_f = jnp.minimum(g_pre, 0.0) - jnp.log1p(jnp.exp(-jnp.abs(g_pre)))
        bcum = jnp.dot(tri, log_f, precision=HIGHEST, preferred_element_type=F32)
        gm = jnp.where(lane < M_HEADS, g_pre, bcum)
        gm_t = gm.T
        for h in range(M_HEADS):
            hs = slice(h * DH, (h + 1) * DH)
            bb = jnp.broadcast_to(gm[:, M_HEADS + h:M_HEADS + h + 1], (L, LANES))
            ii = jnp.broadcast_to(gm[:, h:h + 1], (L, LANES))
            bb_row = gm_t[M_HEADS + h:M_HEADS + h + 1, :]
            ii_row = gm_t[h:h + 1, :]
            m_old = m_s[h:h + 1, :]
            dlog = jnp.where(causal, bb - (bb_row - ii_row), -jnp.inf)
            a = bb + m_old
            mt = jnp.maximum(a, jnp.broadcast_to(jnp.max(dlog, axis=-1, keepdims=True), (L, LANES)))
            w_intra = jnp.exp(dlog - mt)
            w_state = jnp.exp(a - mt)
            qh = q_s[pl.ds(r0, L), hs]
            kf = k_s[pl.ds(r0, L), hs]
            v_ext = jnp.concatenate([zv_ref[pl.ds(r0, L), hs], ones_blk], axis=1)
            s = lax.dot_general(qh, kf.astype(BF16), (((1,), (1,)), ((), ())),
                                preferred_element_type=F32) * w_intra
            c_old = c_s[h]
            intra = jnp.dot(s.astype(BF16), v_ext, preferred_element_type=F32)
            inter = jnp.dot(qh, c_old.astype(BF16), preferred_element_type=F32)
            num = intra[:, :DH] + w_state * inter[:, :DH]
            den = intra[:, DH:] + w_state * inter[:, DH:]
            h_t = num / jnp.maximum(jnp.abs(den), jnp.exp(-mt))
            b_last = bb[L - 1:L, :]
            g = b_last - bb + ii
            m_new = jnp.maximum(b_last + m_old, jnp.max(g, axis=0, keepdims=True))
            decay = jnp.exp(b_last + m_old - m_new)
            kw = kf * jnp.exp(g - m_new)
            c_s[h] = jnp.concatenate([decay, decay], axis=1) * c_old \
                + jnp.dot(kw.T.astype(BF16), v_ext, preferred_element_type=F32)
            m_s[h:h + 1, :] = m_new
            hc = jax.nn.sigmoid(zo_ref[pl.ds(r0, L), hs].astype(F32)) * h_t
            o_ref[pl.ds(r0, L), hs] = _rms(hc, ng_ref[:, hs]).astype(BF16)
        return carry

    lax.fori_loop(0, t_blk // L, chunk_body, 0)


def _mlstm(z, gates, conv_w, conv_b, gate_b, norm_g, batch, seq):
    n = batch * seq
    t = MLSTM_T
    nt = seq // t
    row_blk = lambda b, s: b * nt + s
    zspec = lambda cb: pl.BlockSpec((t, M_WIDTH), lambda b, s: (row_blk(b, s), cb))
    return pl.pallas_call(
        functools.partial(_mlstm_kernel, t_blk=t, chunk=MLSTM_CHUNK),
        grid=(batch, nt),
        in_specs=[
            zspec(C_MQ // M_WIDTH), zspec(C_MK // M_WIDTH), zspec(C_MV // M_WIDTH), zspec(C_MO // M_WIDTH),
            pl.BlockSpec((t, LANES), lambda b, s: (row_blk(b, s), 0)),
            _const_spec((CONV_WIDTH, 2 * M_WIDTH)),
            _const_spec((1, 2 * M_WIDTH)),
            _const_spec((1, LANES)),
            _const_spec((1, M_WIDTH)),
        ],
        out_specs=pl.BlockSpec((t, M_WIDTH), lambda b, s: (row_blk(b, s), 0)),
        out_shape=jax.ShapeDtypeStruct((n, M_WIDTH), BF16),
        scratch_shapes=[
            pltpu.VMEM((t + SUBLANES, 2 * M_WIDTH), F32),
            pltpu.VMEM((t, M_WIDTH), BF16),
            pltpu.VMEM((t, M_WIDTH), F32),
            pltpu.VMEM((M_HEADS, M_HEAD_DIM, M_HEAD_DIM + LANES), F32),
            pltpu.VMEM((SUBLANES, LANES), F32),
        ],
        compiler_params=_params("arbitrary", "arbitrary"),
        name="mlstm",
    )(z, z, z, z, gates, conv_w, conv_b, gate_b, norm_g)


MOBA_V_ROWS = A_HEAD_DIM + 16
MOBA_ROWB_ROWS = 24
MOBA_GROUP = 8
LOG2E = math.log2(math.e)
FAST_SOFTMAX_MIN_DENOM = 1e-25


def _moba_items(n_blocks):
    items = []
    for qb in range(n_blocks):
        items.append((qb, qb, 0, 0))
        for j in range(qb):
            items.append((qb, j, min(qb - j, N_BIAS_TILES - 1), j + 1))
    n_groups = -(-len(items) // MOBA_GROUP)
    noop = (n_blocks - 1, 0, 0, MOBA_ROWB_ROWS - 1)
    items += [noop] * (n_groups * MOBA_GROUP + 2 - len(items))
    return n_groups, [jnp.asarray([it[c] for it in items], jnp.int32) for c in range(4)]


def _moba_kernel(it_q, it_blk, it_tile, it_row, qt_ref, k_ref, vt_ref, bias_ref, bound_ref, qg_ref, kg_ref, o_ref,
                 kn_s, vt_s, kmean_s, qh_s, rowb_s, acc_s, st_a, st_b, p_a, p_b, *, n_blocks, n_groups):
    BS = MOBA_BLOCK
    DA = A_HEAD_DIM
    seq = n_blocks * BS
    lane_k = lax.broadcasted_iota(jnp.int32, (BS, LANES), 1)
    head0_k = lane_k < DA

    ones_row = jnp.where(lax.broadcasted_iota(jnp.int32, (16, BS), 0) == 0, 1.0, 0.0).astype(BF16)
    for j in range(n_blocks):
        kf = k_ref[j * BS:(j + 1) * BS, :].astype(F32)
        k2 = kf * kf
        s0 = jnp.sum(jnp.where(head0_k, k2, 0.0), axis=-1, keepdims=True)
        s1 = jnp.sum(jnp.where(head0_k, 0.0, k2), axis=-1, keepdims=True)
        inv = jnp.where(head0_k, lax.rsqrt(s0 / DA + EPS), lax.rsqrt(s1 / DA + EPS))
        kn = kf * inv * kg_ref[...]
        kn_s[j * BS:(j + 1) * BS, :] = kn.astype(BF16)
        kmean_s[j:j + 1, :] = jnp.mean(kn, axis=0, keepdims=True)
        for h in range(2):
            vt_s[j, h, 0:DA, :] = vt_ref[h * DA:(h + 1) * DA, j * BS:(j + 1) * BS]
            vt_s[j, h, DA:MOBA_V_ROWS, :] = ones_row

    qf = qt_ref[...].astype(F32)
    sub_q = lax.broadcasted_iota(jnp.int32, qf.shape, 0)
    head0_q = sub_q < DA
    q2 = qf * qf
    ss0 = jnp.sum(jnp.where(head0_q, q2, 0.0), axis=0, keepdims=True)
    ss1 = jnp.sum(jnp.where(head0_q, 0.0, q2), axis=0, keepdims=True)
    qn = qf * jnp.where(head0_q, lax.rsqrt(ss0 / DA + EPS), lax.rsqrt(ss1 / DA + EPS)) * qg_ref[...]

    blk = lax.broadcasted_iota(jnp.int32, (n_blocks, seq), 0)
    past = blk < lax.broadcasted_iota(jnp.int32, (n_blocks, seq), 1) // BS
    lane_m = lax.broadcasted_iota(jnp.int32, (n_blocks, LANES), 1)
    kmean = kmean_s[...]
    for h in range(2):
        hmask_m = (lane_m < DA) if h == 0 else (lane_m >= DA)
        gate = jnp.dot(jnp.where(hmask_m, kmean, 0.0), qn, precision=HIGHEST, preferred_element_type=F32)
        gate = jnp.where(past, gate, -jnp.inf)
        rank = jnp.zeros((n_blocks, seq), jnp.int32)
        for j2 in range(n_blocks):
            other = gate[j2:j2 + 1, :]
            beats = (other > gate) | ((other == gate) & (j2 < blk))
            rank = rank + beats.astype(jnp.int32)
        bound = bound_ref[h, :, 0:1]
        selb = jnp.where(past & (rank < MOBA_TOPK), 0.0, NEG) - bound
        hmask_q = head0_q if h == 0 else jnp.logical_not(head0_q)
        qh = jnp.where(hmask_q, qn * (DA ** -0.5 * LOG2E), 0.0).astype(BF16)
        for qb in range(n_blocks):
            cols = slice(qb * BS, (qb + 1) * BS)
            qh_s[qb, h] = qh[:, cols]
            rowb_s[qb, h, 0:1, :] = jnp.broadcast_to(-bound, (1, BS))
            rowb_s[qb, h, 1:n_blocks + 1, :] = selb[:, cols]
            rowb_s[qb, h, n_blocks + 1:, :] = jnp.full((MOBA_ROWB_ROWS - n_blocks - 1, BS), NEG, F32)
    acc_s[...] = jnp.zeros_like(acc_s)

    def scores(h, qb, j, tile, row):
        kj = kn_s[pl.ds(pl.multiple_of(j * BS, BS), BS), :]
        st = jnp.dot(kj, qh_s[qb, h], preferred_element_type=F32)
        return st + bias_ref[h, tile] + rowb_s[qb, h, pl.ds(row, 1), :]

    def pv(h, j, p):
        return jnp.dot(vt_s[j, h], p, preferred_element_type=F32)

    def stage_scores(i, st_ref):
        for h in range(2):
            st_ref[h] = scores(h, it_q[i], it_blk[i], it_tile[i], it_row[i])

    def stage_exp(st_ref, p_ref):
        for h in range(2):
            p_ref[h] = jnp.exp2(st_ref[h]).astype(BF16)

    def stage_pv(i, p_ref):
        for h in range(2):
            acc_s[it_q[i], h] += pv(h, it_blk[i], p_ref[h])

    def group(m, carry):
        for u in range(0, MOBA_GROUP, 2):
            i = MOBA_GROUP * m + u
            stage_pv(i, p_a)
            stage_scores(i + 2, st_a)
            stage_exp(st_b, p_b)
            stage_pv(i + 1, p_b)
            stage_scores(i + 3, st_b)
            stage_exp(st_a, p_a)
        return carry

    stage_scores(0, st_a)
    stage_exp(st_a, p_a)
    stage_scores(1, st_b)
    lax.fori_loop(0, n_groups, group, 0)

    def finish(qb, l_min):
        outs = []
        for h in range(2):
            acc = acc_s[qb, h]
            l = acc[DA:DA + 1, :]
            outs.append(acc[0:DA, :] / l)
            l_min = jnp.minimum(l_min, jnp.min(l))
        o_ref[pl.ds(pl.multiple_of(qb * BS, BS), BS), :] = jnp.concatenate(outs, axis=0).T.astype(BF16)
        return l_min

    l_min = lax.fori_loop(0, n_blocks, finish, jnp.float32(jnp.inf))

    def online(qb, carry):
        state = []
        for h in range(2):
            st = scores(h, qb, qb, 0, 0)
            m = jnp.max(st, axis=0, keepdims=True)
            full = pv(h, qb, jnp.exp2(st - m).astype(BF16))
            state += [m, full]

        def past_body(j, state):
            tile = jnp.minimum(qb - j, N_BIAS_TILES - 1)
            new = []
            for h in range(2):
                m_old, full_old = state[2 * h:2 * h + 2]
                st = scores(h, qb, j, tile, j + 1)
                m_new = jnp.maximum(m_old, jnp.max(st, axis=0, keepdims=True))
                full = pv(h, j, jnp.exp2(st - m_new).astype(BF16))
                new += [m_new, jnp.exp2(m_old - m_new) * full_old + full]
            return tuple(new)

        state = lax.fori_loop(0, qb, past_body, tuple(state))
        outs = [state[2 * h + 1][0:DA, :] / state[2 * h + 1][DA:DA + 1, :] for h in range(2)]
        o_ref[pl.ds(pl.multiple_of(qb * BS, BS), BS), :] = jnp.concatenate(outs, axis=0).T.astype(BF16)
        return carry

    @pl.when(l_min < FAST_SOFTMAX_MIN_DENOM)
    def _redo():
        lax.fori_loop(0, n_blocks, online, 0)


def _moba_logit_bound(rel_bias, q_norm_g, k_norm_g):
    qk = A_HEAD_DIM * jnp.max(jnp.abs(q_norm_g)) * jnp.max(jnp.abs(k_norm_g)) * (A_HEAD_DIM ** -0.5) * 1.02
    b = (qk + jnp.max(rel_bias, axis=0)) * LOG2E
    return jnp.broadcast_to(b[:, None, None], (A_HEADS, 1, LANES)).astype(F32)


def _moba(z, zt, bias_tiles, bound, qg_col, kg_row, batch, seq):
    n = batch * seq
    nb = seq // MOBA_BLOCK
    bs = MOBA_BLOCK
    hp = A_HEADS // 2
    w2 = 2 * A_HEAD_DIM
    assert nb + 2 <= MOBA_ROWB_ROWS
    n_groups, items = _moba_items(nb)
    grid_spec = pltpu.PrefetchScalarGridSpec(
        num_scalar_prefetch=len(items),
        grid=(batch, hp),
        in_specs=[
            pl.BlockSpec((w2, seq), lambda b, p, *_: (p, b)),
            pl.BlockSpec((seq, w2), lambda b, p, *_: (b, C_AK // w2 + p)),
            pl.BlockSpec((w2, seq), lambda b, p, *_: (hp + p, b)),
            pl.BlockSpec((2, N_BIAS_TILES, bs, bs), lambda b, p, *_: (p, 0, 0, 0)),
            pl.BlockSpec((2, 1, LANES), lambda b, p, *_: (p, 0, 0)),
            pl.BlockSpec((w2, 1), lambda b, p, *_: (0, 0)),
            pl.BlockSpec((1, w2), lambda b, p, *_: (0, 0)),
        ],
        out_specs=pl.BlockSpec((seq, w2), lambda b, p, *_: (b, p)),
        scratch_shapes=[
            pltpu.VMEM((seq, w2), BF16),
            pltpu.VMEM((nb, 2, MOBA_V_ROWS, bs), BF16),
            pltpu.VMEM((nb, w2), F32),
            pltpu.VMEM((nb, 2, w2, bs), BF16),
            pltpu.VMEM((nb, 2, MOBA_ROWB_ROWS, bs), F32),
            pltpu.VMEM((nb, 2, MOBA_V_ROWS, bs), F32),
            pltpu.VMEM((2, bs, bs), F32),
            pltpu.VMEM((2, bs, bs), F32),
            pltpu.VMEM((2, bs, bs), BF16),
            pltpu.VMEM((2, bs, bs), BF16),
        ],
    )
    return pl.pallas_call(
        functools.partial(_moba_kernel, n_blocks=nb, n_groups=n_groups),
        grid_spec=grid_spec,
        out_shape=jax.ShapeDtypeStruct((n, A_WIDTH), BF16),
        compiler_params=_params("arbitrary", "arbitrary"),
        name="moba",
    )(*items, zt, z, zt, bias_tiles, bound, qg_col, kg_row)


def _t5_bucket(dist):
    n = jnp.maximum(dist, 0)
    max_exact = REL_BUCKETS // 2
    log_ratio = jnp.log(jnp.maximum(n, max_exact).astype(F32) / max_exact) / math.log(REL_MAX_DIST / max_exact)
    large = max_exact + (log_ratio * (REL_BUCKETS - max_exact)).astype(jnp.int32)
    large = jnp.minimum(large, REL_BUCKETS - 1)
    return jnp.where(n < max_exact, n, large)


def _bias_tiles_kernel(rb_ref, bucket_ref, o_ref):
    bucket = bucket_ref[0]
    hit = [bucket == b for b in range(REL_BUCKETS)]
    tk = lax.broadcasted_iota(jnp.int32, bucket.shape, 0)
    tq = lax.broadcasted_iota(jnp.int32, bucket.shape, 1)
    masked = (tk > tq) & (pl.program_id(0) == 0)
    for h in range(A_HEADS):
        acc = jnp.zeros(bucket.shape, F32)
        for b in range(REL_BUCKETS):
            acc = jnp.where(hit[b], rb_ref[b, h], acc)
        o_ref[h, 0] = jnp.where(masked, NEG, acc * LOG2E)


def _moba_bias_tiles(rel_bias):
    assert (N_BIAS_TILES - 1) * MOBA_BLOCK - (MOBA_BLOCK - 1) >= REL_MAX_DIST
    tk = jnp.arange(MOBA_BLOCK)[None, :, None]
    tq = jnp.arange(MOBA_BLOCK)[None, None, :]
    diff = jnp.arange(N_BIAS_TILES)[:, None, None]
    bucket = _t5_bucket(diff * MOBA_BLOCK + tq - tk).astype(jnp.int32)
    return pl.pallas_call(
        _bias_tiles_kernel,
        grid=(N_BIAS_TILES,),
        in_specs=[
            pl.BlockSpec(memory_space=pltpu.SMEM),
            pl.BlockSpec((1, MOBA_BLOCK, MOBA_BLOCK), lambda t: (t, 0, 0)),
        ],
        out_specs=pl.BlockSpec((A_HEADS, 1, MOBA_BLOCK, MOBA_BLOCK), lambda t: (0, t, 0, 0)),
        out_shape=jax.ShapeDtypeStruct((A_HEADS, N_BIAS_TILES, MOBA_BLOCK, MOBA_BLOCK), F32),
        compiler_params=_params("arbitrary"),
        name="moba_bias_tiles",
    )(rel_bias.astype(F32), bucket)


def _merge_kernel(x_ref, ha_ref, ob_ref, ga_ref, gb_ref, wa_ref, wb_ref, wo_ref, o_ref):
    ya = jnp.dot(ha_ref[...], wa_ref[...], preferred_element_type=F32)
    yb = jnp.dot(ob_ref[...], wb_ref[...], preferred_element_type=F32)
    y = jax.nn.sigmoid(ga_ref[...].astype(F32)) * ya + jax.nn.sigmoid(gb_ref[...].astype(F32)) * yb
    o_ref[...] = x_ref[...] + jnp.dot(y.astype(BF16), wo_ref[...], preferred_element_type=F32)


def _merge(x2, ha, ob, z, wa, wb, wo):
    n = x2.shape[0]
    tm = MERGE_TM
    return pl.pallas_call(
        _merge_kernel,
        grid=(n // tm,),
        in_specs=[
            pl.BlockSpec((tm, D_MODEL), lambda i: (i, 0)),
            pl.BlockSpec((tm, M_WIDTH), lambda i: (i, 0)),
            pl.BlockSpec((tm, A_WIDTH), lambda i: (i, 0)),
            pl.BlockSpec((tm, D_MODEL), lambda i: (i, C_GA // D_MODEL)),
            pl.BlockSpec((tm, D_MODEL), lambda i: (i, C_GB // D_MODEL)),
            _const_spec((M_WIDTH, D_MODEL)),
            _const_spec((A_WIDTH, D_MODEL)),
            _const_spec((D_MODEL, D_MODEL)),
        ],
        out_specs=pl.BlockSpec((tm, D_MODEL), lambda i: (i, 0)),
        out_shape=jax.ShapeDtypeStruct((n, D_MODEL), F32),
        compiler_params=_params("arbitrary"),
        name="merge",
    )(x2, ha, ob, z, z, wa, wb, wo)


def _ffn_kernel(x_ref, g_ref, wg_ref, wu_ref, wd_ref, o_ref, *, d_ff):
    x = x_ref[...]
    hn = _rms(x, g_ref[...]).astype(BF16)
    acc = x
    for c in range(0, d_ff, FFN_CHUNK):
        g = jnp.dot(hn, wg_ref[:, c:c + FFN_CHUNK], preferred_element_type=F32)
        u = jnp.dot(hn, wu_ref[:, c:c + FFN_CHUNK], preferred_element_type=F32)
        a = (g * jax.nn.sigmoid(g) * u).astype(BF16)
        acc = acc + jnp.dot(a, wd_ref[c:c + FFN_CHUNK, :], preferred_element_type=F32)
    o_ref[...] = acc


def _ffn(x2, g, wg, wu, wd):
    n = x2.shape[0]
    d_ff = wg.shape[1]
    assert d_ff % FFN_CHUNK == 0
    tm = FFN_TM
    return pl.pallas_call(
        functools.partial(_ffn_kernel, d_ff=d_ff),
        grid=(n // tm,),
        in_specs=[
            pl.BlockSpec((tm, D_MODEL), lambda i: (i, 0)),
            _const_spec((1, D_MODEL)),
            _const_spec((D_MODEL, d_ff)),
            _const_spec((D_MODEL, d_ff)),
            _const_spec((d_ff, D_MODEL)),
        ],
        out_specs=pl.BlockSpec((tm, D_MODEL), lambda i: (i, 0)),
        out_shape=jax.ShapeDtypeStruct((n, D_MODEL), F32),
        compiler_params=_params("arbitrary"),
        name="ffn_dense",
    )(x2, g, wg, wu, wd)


META_W0, META_W1, META_E0, META_E1, META_R0, META_R1 = range(6)


def _router_kernel(x_ref, g_ref, rw_ref, h_ref, meta_ref, cnt_ref, carry_s):
    @pl.when(pl.program_id(0) == 0)
    def _init():
        carry_s[...] = jnp.zeros_like(carry_s)

    h = _rms(x_ref[...], g_ref[...])
    h_ref[...] = h
    tm = h.shape[0]
    logits = jnp.dot(h, rw_ref[...], precision=HIGHEST, preferred_element_type=F32)
    lane = lax.broadcasted_iota(jnp.int32, (tm, LANES), 1)
    lg = jnp.where(lane < N_EXPERTS, logits, -jnp.inf)
    m1 = jnp.max(lg, axis=-1, keepdims=True)
    i1 = jnp.min(jnp.where(lg == m1, lane, LANES), axis=-1, keepdims=True)
    lg2 = jnp.where(lane == i1, -jnp.inf, lg)
    m2 = jnp.max(lg2, axis=-1, keepdims=True)
    i2 = jnp.min(jnp.where(lg2 == m2, lane, LANES), axis=-1, keepdims=True)
    e = jnp.exp(m2 - m1)
    w1 = 1.0 / (1.0 + e)
    w2 = e / (1.0 + e)
    hit1 = lane == i1
    hit2 = lane == i2
    onehot = jnp.where(hit1 | hit2, 1.0, 0.0)
    r = lax.broadcasted_iota(jnp.int32, (tm, tm), 0)
    c = lax.broadcasted_iota(jnp.int32, (tm, tm), 1)
    before = jnp.where(c < r, 1.0, 0.0).astype(BF16)
    carry = carry_s[0:1, :]
    pref = jnp.dot(before, onehot.astype(BF16), preferred_element_type=F32) + carry
    r1 = jnp.sum(jnp.where(hit1, pref, 0.0), axis=-1, keepdims=True)
    r2 = jnp.sum(jnp.where(hit2, pref, 0.0), axis=-1, keepdims=True)
    new_carry = carry + jnp.sum(onehot, axis=0, keepdims=True)
    carry_s[...] = jnp.broadcast_to(new_carry, carry_s.shape)
    cnt_ref[...] = jnp.broadcast_to(new_carry, cnt_ref.shape)
    meta = jnp.zeros((tm, LANES), F32)
    for idx, val in ((META_W0, w1), (META_W1, w2), (META_E0, i1.astype(F32)), (META_E1, i2.astype(F32)),
                     (META_R0, r1), (META_R1, r2)):
        meta = jnp.where(lane == idx, val, meta)
    meta_ref[...] = meta


def _router(x2, g, rw):
    n = x2.shape[0]
    tm = ROUTER_TM
    return pl.pallas_call(
        _router_kernel,
        grid=(n // tm,),
        in_specs=[
            pl.BlockSpec((tm, D_MODEL), lambda i: (i, 0)),
            _const_spec((1, D_MODEL)),
            _const_spec((D_MODEL, LANES)),
        ],
        out_specs=[
            pl.BlockSpec((tm, D_MODEL), lambda i: (i, 0)),
            pl.BlockSpec((tm, LANES), lambda i: (i, 0)),
            _const_spec((SUBLANES, LANES)),
        ],
        out_shape=[
            jax.ShapeDtypeStruct((n, D_MODEL), F32),
            jax.ShapeDtypeStruct((n, LANES), F32),
            jax.ShapeDtypeStruct((SUBLANES, LANES), F32),
        ],
        scratch_shapes=[pltpu.VMEM((SUBLANES, LANES), F32)],
        compiler_params=_params("arbitrary"),
        name="moe_router",
    )(x2, g, rw)


def _row_copy(src_ref, src_row, dst_ref, dst_row, sem):
    return pltpu.make_async_copy(src_ref.at[pl.ds(src_row, 1)], dst_ref.at[pl.ds(dst_row, 1)], sem)


def _scatter_kernel(pos_ref, h_ref, xs_in_ref, xs_ref, sem, *, rows):
    del xs_in_ref

    def issue(r, carry):
        _row_copy(h_ref, r, xs_ref, pos_ref[0, 0, r], sem).start(priority=0)
        _row_copy(h_ref, r, xs_ref, pos_ref[0, 0, rows + r], sem).start(priority=1)
        return carry

    lax.fori_loop(0, rows, issue, 0)
    for _ in range(2):
        pltpu.make_async_copy(h_ref, xs_ref.at[pl.ds(0, rows)], sem).wait()


def _scatter(pos, h, xs_zero):
    n = h.shape[0]
    r = SCATTER_R
    return pl.pallas_call(
        functools.partial(_scatter_kernel, rows=r),
        grid=(n // r,),
        in_specs=[
            pl.BlockSpec((1, 1, 2 * r), lambda i: (i, 0, 0), memory_space=pltpu.SMEM),
            pl.BlockSpec((r, D_MODEL), lambda i: (i, 0)),
            pl.BlockSpec(memory_space=pl.ANY),
        ],
        out_specs=pl.BlockSpec(memory_space=pl.ANY),
        out_shape=jax.ShapeDtypeStruct(xs_zero.shape, F32),
        scratch_shapes=[pltpu.SemaphoreType.DMA(())],
        input_output_aliases={2: 0},
        compiler_params=_params("arbitrary"),
        name="moe_scatter",
    )(pos, h, xs_zero)


def _moe_kernel(blk_ref, exp_ref, valid_ref, xs_ref, wg_ref, wu_ref, wd_ref, y_ref, xb_s, acc_s):
    del blk_ref, exp_ref
    i = pl.program_id(0)
    f = pl.program_id(1)

    @pl.when(valid_ref[i] == 1)
    def _tile():
        @pl.when(f == 0)
        def _cast():
            xb_s[...] = xs_ref[...].astype(BF16)

        xb = xb_s[...]
        g = jnp.dot(xb, wg_ref[0], preferred_element_type=F32)
        u = jnp.dot(xb, wu_ref[0], preferred_element_type=F32)
        a = (g * jax.nn.sigmoid(g) * u).astype(BF16)
        d = jnp.dot(a, wd_ref[0], preferred_element_type=F32)

        @pl.when(f == 0)
        def _first():
            acc_s[...] = d

        @pl.when(f > 0)
        def _rest():
            acc_s[...] += d

        @pl.when(f == pl.num_programs(1) - 1)
        def _out():
            y_ref[...] = acc_s[...]


def _moe_experts(tile_blk, tile_exp, tile_valid, xs, wg, wu, wd):
    rows = xs.shape[0]
    d_ff = wg.shape[2]
    tm, tf = MOE_TM, MOE_TF
    assert rows % tm == 0 and d_ff % tf == 0
    grid_spec = pltpu.PrefetchScalarGridSpec(
        num_scalar_prefetch=3,
        grid=(rows // tm, d_ff // tf),
        in_specs=[
            pl.BlockSpec((tm, D_MODEL), lambda i, f, blk, ex, va: (blk[i], 0)),
            pl.BlockSpec((1, D_MODEL, tf), lambda i, f, blk, ex, va: (ex[i], 0, f)),
            pl.BlockSpec((1, D_MODEL, tf), lambda i, f, blk, ex, va: (ex[i], 0, f)),
            pl.BlockSpec((1, tf, D_MODEL), lambda i, f, blk, ex, va: (ex[i], f, 0)),
        ],
        out_specs=pl.BlockSpec((tm, D_MODEL), lambda i, f, blk, ex, va: (blk[i], 0)),
        scratch_shapes=[pltpu.VMEM((tm, D_MODEL), BF16), pltpu.VMEM((tm, D_MODEL), F32)],
    )
    return pl.pallas_call(
        _moe_kernel,
        grid_spec=grid_spec,
        out_shape=jax.ShapeDtypeStruct((rows, D_MODEL), F32),
        input_output_aliases={3: 0},
        compiler_params=_params("arbitrary", "arbitrary"),
        name="moe_experts",
    )(tile_blk, tile_exp, tile_valid, xs, wg, wu, wd)


def _combine_kernel(pos_ref, x_ref, meta_ref, ys_ref, o_ref, ybuf, sem, *, rows):
    def issue(r, carry):
        _row_copy(ys_ref, pos_ref[0, 0, r], ybuf, r, sem).start(priority=0)
        _row_copy(ys_ref, pos_ref[0, 0, rows + r], ybuf, rows + r, sem).start(priority=1)
        return carry

    lax.fori_loop(0, rows, issue, 0)
    pltpu.make_async_copy(ys_ref.at[pl.ds(0, 2 * rows)], ybuf, sem).wait()
    meta = meta_ref[...]
    w0 = meta[:, META_W0:META_W0 + 1]
    w1 = meta[:, META_W1:META_W1 + 1]
    o_ref[...] = x_ref[...] + w0 * ybuf[0:rows, :] + w1 * ybuf[rows:2 * rows, :]


def _combine(pos, x2, meta, ys):
    n = x2.shape[0]
    r = SCATTER_R
    return pl.pallas_call(
        functools.partial(_combine_kernel, rows=r),
        grid=(n // r,),
        in_specs=[
            pl.BlockSpec((1, 1, 2 * r), lambda i: (i, 0, 0), memory_space=pltpu.SMEM),
            pl.BlockSpec((r, D_MODEL), lambda i: (i, 0)),
            pl.BlockSpec((r, LANES), lambda i: (i, 0)),
            pl.BlockSpec(memory_space=pl.ANY),
        ],
        out_specs=pl.BlockSpec((r, D_MODEL), lambda i: (i, 0)),
        out_shape=jax.ShapeDtypeStruct((n, D_MODEL), F32),
        scratch_shapes=[pltpu.VMEM((2 * r, D_MODEL), F32), pltpu.SemaphoreType.DMA(())],
        compiler_params=_params("arbitrary"),
        name="moe_combine",
    )(pos, x2, meta, ys)


def _moe(x2, g, router_w, wg, wu, wd):
    n = x2.shape[0]
    tm = MOE_TM
    rw = jnp.zeros((D_MODEL, LANES), F32).at[:, :N_EXPERTS].set(router_w)
    h, meta, cnt = _router(x2, g, rw)
    counts = cnt[0, :N_EXPERTS].astype(jnp.int32)
    padded = ((counts + tm - 1) // tm) * tm
    ends = jnp.cumsum(padded)
    offs = ends - padded
    n_tiles = (2 * n) // tm + N_EXPERTS
    e0 = meta[:, META_E0].astype(jnp.int32)
    e1 = meta[:, META_E1].astype(jnp.int32)
    pos0 = offs[e0] + meta[:, META_R0].astype(jnp.int32)
    pos1 = offs[e1] + meta[:, META_R1].astype(jnp.int32)
    r = SCATTER_R
    pos = jnp.concatenate([pos0.reshape(n // r, 1, r), pos1.reshape(n // r, 1, r)], axis=-1)
    starts = jnp.arange(n_tiles, dtype=jnp.int32) * tm
    n_valid = ends[-1] // tm
    tile_valid = (starts < ends[-1]).astype(jnp.int32)
    tile_blk = jnp.minimum(jnp.arange(n_tiles, dtype=jnp.int32), n_valid - 1)
    tile_exp = jnp.sum((tile_blk[:, None] * tm >= ends[None, :]).astype(jnp.int32), axis=1)
    xs = _scatter(pos, h, jnp.zeros((n_tiles * tm, D_MODEL), F32))
    ys = _moe_experts(tile_blk, tile_exp, tile_valid, xs, wg, wu, wd)
    return _combine(pos, x2, meta, ys)


def _mixer(x2, batch, seq, rel_bias, bias_tiles, norm_g, w_in, conv_w, conv_b, igate_b, fgate_b, mlstm_norm_g,
           q_norm_g, k_norm_g, w_branch_a, w_branch_b, w_out):
    sizes = (M_WIDTH, M_WIDTH, M_WIDTH, M_WIDTH, M_HEADS, M_HEADS, A_WIDTH, A_WIDTH, A_WIDTH, D_MODEL, D_MODEL)
    cuts = [0]
    for s in sizes:
        cuts.append(cuts[-1] + s)
    mq, mk, mv, mo, mi, mf, aq, ak, av, ga, gb = [w_in[:, cuts[i]:cuts[i + 1]] for i in range(len(sizes))]
    wn = jnp.concatenate([ga, gb, mq, mk, mv, mo, ak], axis=1).astype(BF16)
    wt = jnp.concatenate([aq, av], axis=1).T.astype(BF16)
    wgate = jnp.zeros((D_MODEL, LANES), F32).at[:, :M_HEADS].set(mi).at[:, M_HEADS:2 * M_HEADS].set(mf).astype(BF16)
    gate_b = jnp.zeros((1, LANES), F32).at[0, :M_HEADS].set(igate_b).at[0, M_HEADS:2 * M_HEADS].set(fgate_b)
    z, zt, gates = _in_proj(x2, norm_g.reshape(1, D_MODEL), wn, wt, wgate)
    ha = _mlstm(z, gates, conv_w, conv_b.reshape(1, -1), gate_b, mlstm_norm_g.reshape(1, -1), batch, seq)
    qg_col = jnp.tile(q_norm_g, 2).reshape(2 * A_HEAD_DIM, 1)
    kg_row = jnp.tile(k_norm_g, 2).reshape(1, 2 * A_HEAD_DIM)
    ob = _moba(z, zt, bias_tiles, _moba_logit_bound(rel_bias, q_norm_g, k_norm_g), qg_col, kg_row, batch, seq)
    return _merge(x2, ha, ob, z, w_branch_a.astype(BF16), w_branch_b.astype(BF16), w_out.astype(BF16))


def kernel(x, rel_bias, mix_norm_g, w_in, conv_w, conv_b, igate_b, fgate_b, mlstm_norm_g, q_norm_g, k_norm_g,
           w_branch_a, w_branch_b, w_out, ffn_norm_g, dense_w_gate, dense_w_up, dense_w_down, router_w,
           expert_w_gate, expert_w_up, expert_w_down):
    batch, seq, d = x.shape
    depth = w_in.shape[0]
    assert d == D_MODEL and seq % MLSTM_T == 0 and seq % MOBA_BLOCK == 0
    assert (batch * seq) % MOE_TM == 0
    x2 = x.reshape(batch * seq, d)
    bias_tiles = _moba_bias_tiles(rel_bias)
    for layer in range(depth):
        x2 = _mixer(x2, batch, seq, rel_bias, bias_tiles, mix_norm_g[layer], w_in[layer], conv_w[layer], conv_b[layer],
                    igate_b[layer], fgate_b[layer], mlstm_norm_g[layer], q_norm_g[layer], k_norm_g[layer],
                    w_branch_a[layer], w_branch_b[layer], w_out[layer])
        g = ffn_norm_g[layer].reshape(1, d)
        j = layer // 2
        if layer % 2 == 0:
            x2 = _ffn(x2, g, dense_w_gate[j].astype(BF16), dense_w_up[j].astype(BF16), dense_w_down[j].astype(BF16))
        else:
            x2 = _moe(x2, g, router_w[j], expert_w_gate[j].astype(BF16), expert_w_up[j].astype(BF16),
                      expert_w_down[j].astype(BF16))
    return x2.reshape(batch, seq, d)
```

```python
import functools
import math

import jax
import jax.numpy as jnp
from jax import lax
from jax.experimental import pallas as pl
from jax.experimental.pallas import tpu as pltpu

F32 = jnp.float32
BF16 = jnp.bfloat16
HIGHEST = lax.Precision.HIGHEST

D_MODEL = 1024
M_HEADS = 4
M_HEAD_DIM = 128
M_WIDTH = M_HEADS * M_HEAD_DIM
CONV_WIDTH = 4
A_HEADS = 8
A_HEAD_DIM = 64
A_WIDTH = A_HEADS * A_HEAD_DIM
MOBA_BLOCK = 256
MOBA_TOPK = 3
REL_BUCKETS = 32
REL_MAX_DIST = 1024
N_EXPERTS = 8
EPS = 1e-6

LANES = 128
SUBLANES = 8
NEG = -1e30
VMEM_LIMIT = 56 * 1024 * 1024

C_GA = 0
C_GB = D_MODEL
C_MQ = 2 * D_MODEL
C_MK = C_MQ + M_WIDTH
C_MV = C_MK + M_WIDTH
C_MO = C_MV + M_WIDTH
C_AK = C_MO + M_WIDTH
NAT_WIDTH = C_AK + A_WIDTH
N_BIAS_TILES = 6

PROJ_TM = 512
MLSTM_T = 512
MLSTM_CHUNK = 128
MERGE_TM = 512
FFN_TM = 512
FFN_CHUNK = 256
ROUTER_TM = 512
MOE_TM = 512
MOE_TF = 1792
SCATTER_R = 256


def _params(*sem):
    return pltpu.CompilerParams(dimension_semantics=sem, vmem_limit_bytes=VMEM_LIMIT)


def _const_spec(shape):
    nd = len(shape)
    return pl.BlockSpec(shape, lambda *_: (0,) * nd)


def _rms(x, g):
    return x * lax.rsqrt(jnp.mean(x * x, axis=-1, keepdims=True) + EPS) * g


def _in_proj_kernel(x_ref, g_ref, wn_ref, wt_ref, wg_ref, z_ref, zt_ref, gt_ref):
    hn = _rms(x_ref[...], g_ref[...]).astype(BF16)
    for c in range(0, NAT_WIDTH, 512):
        z_ref[:, c:c + 512] = jnp.dot(hn, wn_ref[:, c:c + 512], preferred_element_type=F32).astype(BF16)
    for c in range(0, 2 * A_WIDTH, 256):
        zt_ref[c:c + 256, :] = lax.dot_general(
            wt_ref[c:c + 256, :], hn, (((1,), (1,)), ((), ())), preferred_element_type=F32).astype(BF16)
    gt_ref[...] = jnp.dot(hn, wg_ref[...], preferred_element_type=F32)


def _in_proj(x2, g, wn, wt, wg):
    n = x2.shape[0]
    tm = PROJ_TM
    return pl.pallas_call(
        _in_proj_kernel,
        grid=(n // tm,),
        in_specs=[
            pl.BlockSpec((tm, D_MODEL), lambda i: (i, 0)),
            _const_spec((1, D_MODEL)),
            _const_spec((D_MODEL, NAT_WIDTH)),
            _const_spec((2 * A_WIDTH, D_MODEL)),
            _const_spec((D_MODEL, LANES)),
        ],
        out_specs=[
            pl.BlockSpec((tm, NAT_WIDTH), lambda i: (i, 0)),
            pl.BlockSpec((2 * A_WIDTH, tm), lambda i: (0, i)),
            pl.BlockSpec((tm, LANES), lambda i: (i, 0)),
        ],
        out_shape=[
            jax.ShapeDtypeStruct((n, NAT_WIDTH), BF16),
            jax.ShapeDtypeStruct((2 * A_WIDTH, n), BF16),
            jax.ShapeDtypeStruct((n, LANES), F32),
        ],
        compiler_params=_params("arbitrary"),
        name="in_proj",
    )(x2, g, wn, wt, wg)


def _mlstm_kernel(zq_ref, zk_ref, zv_ref, zo_ref, gt_ref, cw_ref, cb_ref, gb_ref, ng_ref, o_ref,
                  qk_buf, q_s, k_s, c_s, m_s, *, t_blk, chunk):
    L = chunk
    DH = M_HEAD_DIM
    assert L == LANES and DH == LANES

    @pl.when(pl.program_id(1) == 0)
    def _init():
        qk_buf[0:SUBLANES, :] = jnp.zeros((SUBLANES, 2 * M_WIDTH), F32)
        c_s[...] = jnp.zeros_like(c_s)
        m_s[...] = jnp.zeros_like(m_s)

    qk_buf[SUBLANES:SUBLANES + t_blk, 0:M_WIDTH] = zq_ref[...].astype(F32)
    qk_buf[SUBLANES:SUBLANES + t_blk, M_WIDTH:] = zk_ref[...].astype(F32)
    acc = cb_ref[...] + cw_ref[CONV_WIDTH - 1:CONV_WIDTH, :] * qk_buf[SUBLANES:SUBLANES + t_blk, :]
    for j in range(CONV_WIDTH - 1):
        off = SUBLANES - (CONV_WIDTH - 1) + j
        acc = acc + cw_ref[j:j + 1, :] * qk_buf[off:off + t_blk, :]
    qk = acc * jax.nn.sigmoid(acc)
    qk_buf[0:SUBLANES, :] = qk_buf[t_blk:t_blk + SUBLANES, :]
    q_s[...] = qk[:, :M_WIDTH].astype(BF16)
    k_s[...] = qk[:, M_WIDTH:] * (DH ** -0.5)

    row = lax.broadcasted_iota(jnp.int32, (L, L), 0)
    col = lax.broadcasted_iota(jnp.int32, (L, L), 1)
    causal = col <= row
    tri = causal.astype(F32)
    lane = lax.broadcasted_iota(jnp.int32, (L, LANES), 1)
    ones_blk = jnp.ones((L, LANES), BF16)

    def chunk_body(c, carry):
        r0 = pl.multiple_of(c * L, L)
        g_pre = gt_ref[pl.ds(r0, L), :] + gb_ref[...]
        log_f = jnp.minimum(g_pre, 0.0) - jnp.log1p(jnp.exp(-jnp.abs(g_pre)))
        bcum = jnp.dot(tri, log_f, precision=HIGHEST, preferred_element_type=F32)
        gm = jnp.where(lane < M_HEADS, g_pre, bcum)
        gm_t = gm.T
        c_all = [c_s[h] for h in range(M_HEADS)]
        m_all = m_s[...]
        heads = range(M_HEADS)
        hs = [slice(h * DH, (h + 1) * DH) for h in heads]
        qh = [q_s[pl.ds(r0, L), hs[h]] for h in heads]
        kf = [k_s[pl.ds(r0, L), hs[h]] for h in heads]
        v_ext = [jnp.concatenate([zv_ref[pl.ds(r0, L), hs[h]], ones_blk], axis=1) for h in heads]
        qk = [lax.dot_general(qh[h], kf[h].astype(BF16), (((1,), (1,)), ((), ())), preferred_element_type=F32)
              for h in heads]
        inter = [jnp.dot(qh[h], c_all[h].astype(BF16), preferred_element_type=F32) for h in heads]
        mt, w_state, s_bf, kw_t, decay, m_new = [], [], [], [], [], []
        for h in heads:
            bb = jnp.broadcast_to(gm[:, M_HEADS + h:M_HEADS + h + 1], (L, LANES))
            ii = jnp.broadcast_to(gm[:, h:h + 1], (L, LANES))
            bb_row = gm_t[M_HEADS + h:M_HEADS + h + 1, :]
            ii_row = gm_t[h:h + 1, :]
            m_old = m_all[h:h + 1, :]
            dlog = jnp.where(causal, bb - (bb_row - ii_row), -jnp.inf)
            a = bb + m_old
            mt.append(jnp.maximum(a, jnp.broadcast_to(jnp.max(dlog, axis=-1, keepdims=True), (L, LANES))))
            w_state.append(jnp.exp(a - mt[h]))
            s_bf.append((qk[h] * jnp.exp(dlog - mt[h])).astype(BF16))
            b_last = bb[L - 1:L, :]
            g = b_last - bb + ii
            m_new.append(jnp.maximum(b_last + m_old, jnp.max(g, axis=0, keepdims=True)))
            decay.append(jnp.exp(b_last + m_old - m_new[h]))
            kw_t.append((kf[h] * jnp.exp(g - m_new[h])).T.astype(BF16))
        intra = [jnp.dot(s_bf[h], v_ext[h], preferred_element_type=F32) for h in heads]
        upd = [jnp.dot(kw_t[h], v_ext[h], preferred_element_type=F32) for h in heads]
        for h in heads:
            num = intra[h][:, :DH] + w_state[h] * inter[h][:, :DH]
            den = intra[h][:, DH:] + w_state[h] * inter[h][:, DH:]
            h_t = num / jnp.maximum(jnp.abs(den), jnp.exp(-mt[h]))
            hc = jax.nn.sigmoid(zo_ref[pl.ds(r0, L), hs[h]].astype(F32)) * h_t
            o_ref[pl.ds(r0, L), hs[h]] = _rms(hc, ng_ref[:, hs[h]]).astype(BF16)
        for h in heads:
            c_s[h] = jnp.concatenate([decay[h], decay[h]], axis=1) * c_all[h] + upd[h]
            m_s[h:h + 1, :] = m_new[h]
        return carry

    lax.fori_loop(0, t_blk // L, chunk_body, 0)


def _mlstm(z, gates, conv_w, conv_b, gate_b, norm_g, batch, seq):
    n = batch * seq
    t = MLSTM_T
    nt = seq // t
    row_blk = lambda b, s: b * nt + s
    zspec = lambda cb: pl.BlockSpec((t, M_WIDTH), lambda b, s: (row_blk(b, s), cb))
    return pl.pallas_call(
        functools.partial(_mlstm_kernel, t_blk=t, chunk=MLSTM_CHUNK),
        grid=(batch, nt),
        in_specs=[
            zspec(C_MQ // M_WIDTH), zspec(C_MK // M_WIDTH), zspec(C_MV // M_WIDTH), zspec(C_MO // M_WIDTH),
            pl.BlockSpec((t, LANES), lambda b, s: (row_blk(b, s), 0)),
            _const_spec((CONV_WIDTH, 2 * M_WIDTH)),
            _const_spec((1, 2 * M_WIDTH)),
            _const_spec((1, LANES)),
            _const_spec((1, M_WIDTH)),
        ],
        out_specs=pl.BlockSpec((t, M_WIDTH), lambda b, s: (row_blk(b, s), 0)),
        out_shape=jax.ShapeDtypeStruct((n, M_WIDTH), BF16),
        scratch_shapes=[
            pltpu.VMEM((t + SUBLANES, 2 * M_WIDTH), F32),
            pltpu.VMEM((t, M_WIDTH), BF16),
            pltpu.VMEM((t, M_WIDTH), F32),
            pltpu.VMEM((M_HEADS, M_HEAD_DIM, M_HEAD_DIM + LANES), F32),
            pltpu.VMEM((SUBLANES, LANES), F32),
        ],
        compiler_params=_params("arbitrary", "arbitrary"),
        name="mlstm",
    )(z, z, z, z, gates, conv_w, conv_b, gate_b, norm_g)


MOBA_V_ROWS = A_HEAD_DIM + 16
MOBA_ROWB_ROWS = 24
MOBA_GROUP = 34
LOG2E = math.log2(math.e)
FAST_SOFTMAX_MIN_DENOM = 1e-25


def _moba_items(n_blocks):
    items = []
    for qb in range(n_blocks):
        items.append((qb, qb, 0, 0))
        for j in range(qb):
            items.append((qb, j, min(qb - j, N_BIAS_TILES - 1), j + 1))
    n_groups = -(-len(items) // MOBA_GROUP)
    noop = (n_blocks - 1, 0, 0, MOBA_ROWB_ROWS - 1)
    items += [noop] * (n_groups * MOBA_GROUP + 2 - len(items))
    return n_groups, [jnp.asarray([it[c] for it in items], jnp.int32) for c in range(4)]


def _moba_kernel(it_q, it_blk, it_tile, it_row, qt_ref, k_ref, vt_ref, bias_ref, bound_ref, qg_ref, kg_ref, o_ref,
                 kn_s, vt_s, kmean_s, qh_s, rowb_s, acc_s, st_a, st_b, p_a, p_b, *, n_blocks, n_groups):
    BS = MOBA_BLOCK
    DA = A_HEAD_DIM
    seq = n_blocks * BS
    lane_k = lax.broadcasted_iota(jnp.int32, (BS, LANES), 1)
    head0_k = lane_k < DA

    ones_row = jnp.where(lax.broadcasted_iota(jnp.int32, (16, BS), 0) == 0, 1.0, 0.0).astype(BF16)
    for j in range(n_blocks):
        kf = k_ref[j * BS:(j + 1) * BS, :].astype(F32)
        k2 = kf * kf
        s0 = jnp.sum(jnp.where(head0_k, k2, 0.0), axis=-1, keepdims=True)
        s1 = jnp.sum(jnp.where(head0_k, 0.0, k2), axis=-1, keepdims=True)
        inv = jnp.where(head0_k, lax.rsqrt(s0 / DA + EPS), lax.rsqrt(s1 / DA + EPS))
        kn = kf * inv * kg_ref[...]
        kn_s[j * BS:(j + 1) * BS, :] = kn.astype(BF16)
        kmean_s[j:j + 1, :] = jnp.mean(kn, axis=0, keepdims=True)
        for h in range(2):
            vt_s[j, h, 0:DA, :] = vt_ref[h * DA:(h + 1) * DA, j * BS:(j + 1) * BS]
            vt_s[j, h, DA:MOBA_V_ROWS, :] = ones_row

    qf = qt_ref[...].astype(F32)
    sub_q = lax.broadcasted_iota(jnp.int32, qf.shape, 0)
    head0_q = sub_q < DA
    q2 = qf * qf
    ss0 = jnp.sum(jnp.where(head0_q, q2, 0.0), axis=0, keepdims=True)
    ss1 = jnp.sum(jnp.where(head0_q, 0.0, q2), axis=0, keepdims=True)
    qn = qf * jnp.where(head0_q, lax.rsqrt(ss0 / DA + EPS), lax.rsqrt(ss1 / DA + EPS)) * qg_ref[...]

    blk = lax.broadcasted_iota(jnp.int32, (n_blocks, seq), 0)
    past = blk < lax.broadcasted_iota(jnp.int32, (n_blocks, seq), 1) // BS
    lane_m = lax.broadcasted_iota(jnp.int32, (n_blocks, LANES), 1)
    kmean = kmean_s[...]
    kmean2 = jnp.concatenate([jnp.where(lane_m < DA, kmean, 0.0), jnp.where(lane_m < DA, 0.0, kmean)], axis=0)
    gates = jnp.dot(kmean2, qn, precision=HIGHEST, preferred_element_type=F32)
    for h in range(2):
        gate = jnp.where(past, gates[h * n_blocks:(h + 1) * n_blocks, :], -jnp.inf)
        rank = jnp.zeros((n_blocks, seq), jnp.int32)
        for j2 in range(n_blocks):
            other = gate[j2:j2 + 1, :]
            beats = (other > gate) | ((other == gate) & (j2 < blk))
            rank = rank + beats.astype(jnp.int32)
        bound = bound_ref[h, :, 0:1]
        selb = jnp.where(past & (rank < MOBA_TOPK), 0.0, NEG) - bound
        hmask_q = head0_q if h == 0 else jnp.logical_not(head0_q)
        qh = jnp.where(hmask_q, qn * (DA ** -0.5 * LOG2E), 0.0).astype(BF16)
        for qb in range(n_blocks):
            cols = slice(qb * BS, (qb + 1) * BS)
            qh_s[qb, h] = qh[:, cols]
            rowb_s[qb, h, 0:1, :] = jnp.broadcast_to(-bound, (1, BS))
            rowb_s[qb, h, 1:n_blocks + 1, :] = selb[:, cols]
            rowb_s[qb, h, n_blocks + 1:, :] = jnp.full((MOBA_ROWB_ROWS - n_blocks - 1, BS), NEG, F32)
    acc_s[...] = jnp.zeros_like(acc_s)

    def scores(h, qb, j, tile, row):
        kj = kn_s[pl.ds(pl.multiple_of(j * BS, BS), BS), :]
        st = jnp.dot(kj, qh_s[qb, h], preferred_element_type=F32)
        return st + bias_ref[h, tile] + rowb_s[qb, h, pl.ds(row, 1), :]

    def pv(h, j, p):
        return jnp.dot(vt_s[j, h], p, preferred_element_type=F32)

    def stage_scores(i, st_ref):
        for h in range(2):
            st_ref[h] = scores(h, it_q[i], it_blk[i], it_tile[i], it_row[i])

    def stage_exp(st_ref, p_ref):
        for h in range(2):
            p_ref[h] = jnp.exp2(st_ref[h]).astype(BF16)

    def stage_pv(i, p_ref):
        for h in range(2):
            acc_s[it_q[i], h] += pv(h, it_blk[i], p_ref[h])

    def group(m, carry):
        for u in range(0, MOBA_GROUP, 2):
            i = MOBA_GROUP * m + u
            stage_pv(i, p_a)
            stage_scores(i + 2, st_a)
            stage_exp(st_b, p_b)
            stage_pv(i + 1, p_b)
            stage_scores(i + 3, st_b)
            stage_exp(st_a, p_a)
        return carry

    stage_scores(0, st_a)
    stage_exp(st_a, p_a)
    stage_scores(1, st_b)
    lax.fori_loop(0, n_groups, group, 0)

    def finish(qb, l_min):
        outs = []
        for h in range(2):
            acc = acc_s[qb, h]
            l = acc[DA:DA + 1, :]
            outs.append(acc[0:DA, :] / l)
            l_min = jnp.minimum(l_min, jnp.min(l))
        o_ref[pl.ds(pl.multiple_of(qb * BS, BS), BS), :] = jnp.concatenate(outs, axis=0).T.astype(BF16)
        return l_min

    l_min = lax.fori_loop(0, n_blocks, finish, jnp.float32(jnp.inf))

    def online(qb, carry):
        state = []
        for h in range(2):
            st = scores(h, qb, qb, 0, 0)
            m = jnp.max(st, axis=0, keepdims=True)
            full = pv(h, qb, jnp.exp2(st - m).astype(BF16))
            state += [m, full]

        def past_body(j, state):
            tile = jnp.minimum(qb - j, N_BIAS_TILES - 1)
            new = []
            for h in range(2):
                m_old, full_old = state[2 * h:2 * h + 2]
                st = scores(h, qb, j, tile, j + 1)
                m_new = jnp.maximum(m_old, jnp.max(st, axis=0, keepdims=True))
                full = pv(h, j, jnp.exp2(st - m_new).astype(BF16))
                new += [m_new, jnp.exp2(m_old - m_new) * full_old + full]
            return tuple(new)

        state = lax.fori_loop(0, qb, past_body, tuple(state))
        outs = [state[2 * h + 1][0:DA, :] / state[2 * h + 1][DA:DA + 1, :] for h in range(2)]
        o_ref[pl.ds(pl.multiple_of(qb * BS, BS), BS), :] = jnp.concatenate(outs, axis=0).T.astype(BF16)
        return carry

    @pl.when(l_min < FAST_SOFTMAX_MIN_DENOM)
    def _redo():
        lax.fori_loop(0, n_blocks, online, 0)


def _moba_logit_bound(rel_bias, q_norm_g, k_norm_g):
    qk = A_HEAD_DIM * jnp.max(jnp.abs(q_norm_g)) * jnp.max(jnp.abs(k_norm_g)) * (A_HEAD_DIM ** -0.5) * 1.02
    b = (qk + jnp.max(rel_bias, axis=0)) * LOG2E
    return jnp.broadcast_to(b[:, None, None], (A_HEADS, 1, LANES)).astype(F32)


def _moba(z, zt, bias_tiles, bound, qg_col, kg_row, batch, seq):
    n = batch * seq
    nb = seq // MOBA_BLOCK
    bs = MOBA_BLOCK
    hp = A_HEADS // 2
    w2 = 2 * A_HEAD_DIM
    assert nb + 2 <= MOBA_ROWB_ROWS
    n_groups, items = _moba_items(nb)
    grid_spec = pltpu.PrefetchScalarGridSpec(
        num_scalar_prefetch=len(items),
        grid=(batch, hp),
        in_specs=[
            pl.BlockSpec((w2, seq), lambda b, p, *_: (p, b)),
            pl.BlockSpec((seq, w2), lambda b, p, *_: (b, C_AK // w2 + p)),
            pl.BlockSpec((w2, seq), lambda b, p, *_: (hp + p, b)),
            pl.BlockSpec((2, N_BIAS_TILES, bs, bs), lambda b, p, *_: (p, 0, 0, 0)),
            pl.BlockSpec((2, 1, LANES), lambda b, p, *_: (p, 0, 0)),
            pl.BlockSpec((w2, 1), lambda b, p, *_: (0, 0)),
            pl.BlockSpec((1, w2), lambda b, p, *_: (0, 0)),
        ],
        out_specs=pl.BlockSpec((seq, w2), lambda b, p, *_: (b, p)),
        scratch_shapes=[
            pltpu.VMEM((seq, w2), BF16),
            pltpu.VMEM((nb, 2, MOBA_V_ROWS, bs), BF16),
            pltpu.VMEM((nb, w2), F32),
            pltpu.VMEM((nb, 2, w2, bs), BF16),
            pltpu.VMEM((nb, 2, MOBA_ROWB_ROWS, bs), F32),
            pltpu.VMEM((nb, 2, MOBA_V_ROWS, bs), F32),
            pltpu.VMEM((2, bs, bs), F32),
            pltpu.VMEM((2, bs, bs), F32),
            pltpu.VMEM((2, bs, bs), BF16),
            pltpu.VMEM((2, bs, bs), BF16),
        ],
    )
    return pl.pallas_call(
        functools.partial(_moba_kernel, n_blocks=nb, n_groups=n_groups),
        grid_spec=grid_spec,
        out_shape=jax.ShapeDtypeStruct((n, A_WIDTH), BF16),
        compiler_params=_params("arbitrary", "arbitrary"),
        name="moba",
    )(*items, zt, z, zt, bias_tiles, bound, qg_col, kg_row)


def _t5_bucket(dist):
    n = jnp.maximum(dist, 0)
    max_exact = REL_BUCKETS // 2
    log_ratio = jnp.log(jnp.maximum(n, max_exact).astype(F32) / max_exact) / math.log(REL_MAX_DIST / max_exact)
    large = max_exact + (log_ratio * (REL_BUCKETS - max_exact)).astype(jnp.int32)
    large = jnp.minimum(large, REL_BUCKETS - 1)
    return jnp.where(n < max_exact, n, large)


def _bias_tiles_kernel(rb_ref, bucket_ref, o_ref):
    bucket = bucket_ref[0]
    hit = [bucket == b for b in range(REL_BUCKETS)]
    tk = lax.broadcasted_iota(jnp.int32, bucket.shape, 0)
    tq = lax.broadcasted_iota(jnp.int32, bucket.shape, 1)
    masked = (tk > tq) & (pl.program_id(0) == 0)
    for h in range(A_HEADS):
        acc = jnp.zeros(bucket.shape, F32)
        for b in range(REL_BUCKETS):
            acc = jnp.where(hit[b], rb_ref[b, h], acc)
        o_ref[h, 0] = jnp.where(masked, NEG, acc * LOG2E)


def _moba_bias_tiles(rel_bias):
    assert (N_BIAS_TILES - 1) * MOBA_BLOCK - (MOBA_BLOCK - 1) >= REL_MAX_DIST
    tk = jnp.arange(MOBA_BLOCK)[None, :, None]
    tq = jnp.arange(MOBA_BLOCK)[None, None, :]
    diff = jnp.arange(N_BIAS_TILES)[:, None, None]
    bucket = _t5_bucket(diff * MOBA_BLOCK + tq - tk).astype(jnp.int32)
    return pl.pallas_call(
        _bias_tiles_kernel,
        grid=(N_BIAS_TILES,),
        in_specs=[
            pl.BlockSpec(memory_space=pltpu.SMEM),
            pl.BlockSpec((1, MOBA_BLOCK, MOBA_BLOCK), lambda t: (t, 0, 0)),
        ],
        out_specs=pl.BlockSpec((A_HEADS, 1, MOBA_BLOCK, MOBA_BLOCK), lambda t: (0, t, 0, 0)),
        out_shape=jax.ShapeDtypeStruct((A_HEADS, N_BIAS_TILES, MOBA_BLOCK, MOBA_BLOCK), F32),
        compiler_params=_params("arbitrary"),
        name="moba_bias_tiles",
    )(rel_bias.astype(F32), bucket)


def _merge_kernel(x_ref, ha_ref, ob_ref, ga_ref, gb_ref, wa_ref, wb_ref, wo_ref, o_ref):
    ya = jnp.dot(ha_ref[...], wa_ref[...], preferred_element_type=F32)
    yb = jnp.dot(ob_ref[...], wb_ref[...], preferred_element_type=F32)
    y = jax.nn.sigmoid(ga_ref[...].astype(F32)) * ya + jax.nn.sigmoid(gb_ref[...].astype(F32)) * yb
    o_ref[...] = x_ref[...] + jnp.dot(y.astype(BF16), wo_ref[...], preferred_element_type=F32)


def _merge(x2, ha, ob, z, wa, wb, wo):
    n = x2.shape[0]
    tm = MERGE_TM
    return pl.pallas_call(
        _merge_kernel,
        grid=(n // tm,),
        in_specs=[
            pl.BlockSpec((tm, D_MODEL), lambda i: (i, 0)),
            pl.BlockSpec((tm, M_WIDTH), lambda i: (i, 0)),
            pl.BlockSpec((tm, A_WIDTH), lambda i: (i, 0)),
            pl.BlockSpec((tm, D_MODEL), lambda i: (i, C_GA // D_MODEL)),
            pl.BlockSpec((tm, D_MODEL), lambda i: (i, C_GB // D_MODEL)),
            _const_spec((M_WIDTH, D_MODEL)),
            _const_spec((A_WIDTH, D_MODEL)),
            _const_spec((D_MODEL, D_MODEL)),
        ],
        out_specs=pl.BlockSpec((tm, D_MODEL), lambda i: (i, 0)),
        out_shape=jax.ShapeDtypeStruct((n, D_MODEL), F32),
        compiler_params=_params("arbitrary"),
        name="merge",
    )(x2, ha, ob, z, z, wa, wb, wo)


def _ffn_kernel(x_ref, g_ref, wg_ref, wu_ref, wd_ref, o_ref, *, d_ff):
    x = x_ref[...]
    hn = _rms(x, g_ref[...]).astype(BF16)
    acc = x
    for c in range(0, d_ff, FFN_CHUNK):
        g = jnp.dot(hn, wg_ref[:, c:c + FFN_CHUNK], preferred_element_type=F32)
        u = jnp.dot(hn, wu_ref[:, c:c + FFN_CHUNK], preferred_element_type=F32)
        a = (g * jax.nn.sigmoid(g) * u).astype(BF16)
        acc = acc + jnp.dot(a, wd_ref[c:c + FFN_CHUNK, :], preferred_element_type=F32)
    o_ref[...] = acc


def _ffn(x2, g, wg, wu, wd):
    n = x2.shape[0]
    d_ff = wg.shape[1]
    assert d_ff % FFN_CHUNK == 0
    tm = FFN_TM
    return pl.pallas_call(
        functools.partial(_ffn_kernel, d_ff=d_ff),
        grid=(n // tm,),
        in_specs=[
            pl.BlockSpec((tm, D_MODEL), lambda i: (i, 0)),
            _const_spec((1, D_MODEL)),
            _const_spec((D_MODEL, d_ff)),
            _const_spec((D_MODEL, d_ff)),
            _const_spec((d_ff, D_MODEL)),
        ],
        out_specs=pl.BlockSpec((tm, D_MODEL), lambda i: (i, 0)),
        out_shape=jax.ShapeDtypeStruct((n, D_MODEL), F32),
        compiler_params=_params("arbitrary"),
        name="ffn_dense",
    )(x2, g, wg, wu, wd)


META_W0, META_W1, META_E0, META_E1, META_R0, META_R1 = range(6)


def _router_kernel(x_ref, g_ref, rw_ref, h_ref, meta_ref, cnt_ref, carry_s):
    @pl.when(pl.program_id(0) == 0)
    def _init():
        carry_s[...] = jnp.zeros_like(carry_s)

    h = _rms(x_ref[...], g_ref[...])
    h_ref[...] = h
    tm = h.shape[0]
    logits = jnp.dot(h, rw_ref[...], precision=HIGHEST, preferred_element_type=F32)
    lane = lax.broadcasted_iota(jnp.int32, (tm, LANES), 1)
    lg = jnp.where(lane < N_EXPERTS, logits, -jnp.inf)
    m1 = jnp.max(lg, axis=-1, keepdims=True)
    i1 = jnp.min(jnp.where(lg == m1, lane, LANES), axis=-1, keepdims=True)
    lg2 = jnp.where(lane == i1, -jnp.inf, lg)
    m2 = jnp.max(lg2, axis=-1, keepdims=True)
    i2 = jnp.min(jnp.where(lg2 == m2, lane, LANES), axis=-1, keepdims=True)
    e = jnp.exp(m2 - m1)
    w1 = 1.0 / (1.0 + e)
    w2 = e / (1.0 + e)
    hit1 = lane == i1
    hit2 = lane == i2
    onehot = jnp.where(hit1 | hit2, 1.0, 0.0)
    r = lax.broadcasted_iota(jnp.int32, (tm, tm), 0)
    c = lax.broadcasted_iota(jnp.int32, (tm, tm), 1)
    before = jnp.where(c < r, 1.0, 0.0).astype(BF16)
    carry = carry_s[0:1, :]
    pref = jnp.dot(before, onehot.astype(BF16), preferred_element_type=F32) + carry
    r1 = jnp.sum(jnp.where(hit1, pref, 0.0), axis=-1, keepdims=True)
    r2 = jnp.sum(jnp.where(hit2, pref, 0.0), axis=-1, keepdims=True)
    new_carry = carry + jnp.sum(onehot, axis=0, keepdims=True)
    carry_s[...] = jnp.broadcast_to(new_carry, carry_s.shape)
    cnt_ref[...] = jnp.broadcast_to(new_carry, cnt_ref.shape)
    meta = jnp.zeros((tm, LANES), F32)
    for idx, val in ((META_W0, w1), (META_W1, w2), (META_E0, i1.astype(F32)), (META_E1, i2.astype(F32)),
                     (META_R0, r1), (META_R1, r2)):
        meta = jnp.where(lane == idx, val, meta)
    meta_ref[...] = meta


def _router(x2, g, rw):
    n = x2.shape[0]
    tm = ROUTER_TM
    return pl.pallas_call(
        _router_kernel,
        grid=(n // tm,),
        in_specs=[
            pl.BlockSpec((tm, D_MODEL), lambda i: (i, 0)),
            _const_spec((1, D_MODEL)),
            _const_spec((D_MODEL, LANES)),
        ],
        out_specs=[
            pl.BlockSpec((tm, D_MODEL), lambda i: (i, 0)),
            pl.BlockSpec((tm, LANES), lambda i: (i, 0)),
            _const_spec((SUBLANES, LANES)),
        ],
        out_shape=[
            jax.ShapeDtypeStruct((n, D_MODEL), F32),
            jax.ShapeDtypeStruct((n, LANES), F32),
            jax.ShapeDtypeStruct((SUBLANES, LANES), F32),
        ],
        scratch_shapes=[pltpu.VMEM((SUBLANES, LANES), F32)],
        compiler_params=_params("arbitrary"),
        name="moe_router",
    )(x2, g, rw)


def _row_copy(src_ref, src_row, dst_ref, dst_row, sem):
    return pltpu.make_async_copy(src_ref.at[pl.ds(src_row, 1)], dst_ref.at[pl.ds(dst_row, 1)], sem)


def _scatter_kernel(pos_ref, h_ref, xs_in_ref, xs_ref, sem, *, rows):
    del xs_in_ref

    def issue(r, carry):
        _row_copy(h_ref, r, xs_ref, pos_ref[0, 0, r], sem).start(priority=0)
        _row_copy(h_ref, r, xs_ref, pos_ref[0, 0, rows + r], sem).start(priority=1)
        return carry

    lax.fori_loop(0, rows, issue, 0)
    for _ in range(2):
        pltpu.make_async_copy(h_ref, xs_ref.at[pl.ds(0, rows)], sem).wait()


def _scatter(pos, h, xs_zero):
    n = h.shape[0]
    r = SCATTER_R
    return pl.pallas_call(
        functools.partial(_scatter_kernel, rows=r),
        grid=(n // r,),
        in_specs=[
            pl.BlockSpec((1, 1, 2 * r), lambda i: (i, 0, 0), memory_space=pltpu.SMEM),
            pl.BlockSpec((r, D_MODEL), lambda i: (i, 0)),
            pl.BlockSpec(memory_space=pl.ANY),
        ],
        out_specs=pl.BlockSpec(memory_space=pl.ANY),
        out_shape=jax.ShapeDtypeStruct(xs_zero.shape, F32),
        scratch_shapes=[pltpu.SemaphoreType.DMA(())],
        input_output_aliases={2: 0},
        compiler_params=_params("arbitrary"),
        name="moe_scatter",
    )(pos, h, xs_zero)


def _moe_kernel(blk_ref, exp_ref, valid_ref, xs_ref, wg_ref, wu_ref, wd_ref, y_ref, xb_s, acc_s):
    del blk_ref, exp_ref
    i = pl.program_id(0)
    f = pl.program_id(1)

    @pl.when(valid_ref[i] == 1)
    def _tile():
        @pl.when(f == 0)
        def _cast():
            xb_s[...] = xs_ref[...].astype(BF16)

        xb = xb_s[...]
        d = None
        for c in range(0, wg_ref.shape[2], FFN_CHUNK):
            g = jnp.dot(xb, wg_ref[0, :, c:c + FFN_CHUNK], preferred_element_type=F32)
            u = jnp.dot(xb, wu_ref[0, :, c:c + FFN_CHUNK], preferred_element_type=F32)
            a = (g * jax.nn.sigmoid(g) * u).astype(BF16)
            dc = jnp.dot(a, wd_ref[0, c:c + FFN_CHUNK, :], preferred_element_type=F32)
            d = dc if d is None else d + dc

        @pl.when(f == 0)
        def _first():
            acc_s[...] = d

        @pl.when(f > 0)
        def _rest():
            acc_s[...] += d

        @pl.when(f == pl.num_programs(1) - 1)
        def _out():
            y_ref[...] = acc_s[...]


def _moe_experts(tile_blk, tile_exp, tile_valid, xs, wg, wu, wd):
    rows = xs.shape[0]
    d_ff = wg.shape[2]
    tm, tf = MOE_TM, MOE_TF
    assert rows % tm == 0 and d_ff % tf == 0
    grid_spec = pltpu.PrefetchScalarGridSpec(
        num_scalar_prefetch=3,
        grid=(rows // tm, d_ff // tf),
        in_specs=[
            pl.BlockSpec((tm, D_MODEL), lambda i, f, blk, ex, va: (blk[i], 0)),
            pl.BlockSpec((1, D_MODEL, tf), lambda i, f, blk, ex, va: (ex[i], 0, f)),
            pl.BlockSpec((1, D_MODEL, tf), lambda i, f, blk, ex, va: (ex[i], 0, f)),
            pl.BlockSpec((1, tf, D_MODEL), lambda i, f, blk, ex, va: (ex[i], f, 0)),
        ],
        out_specs=pl.BlockSpec((tm, D_MODEL), lambda i, f, blk, ex, va: (blk[i], 0)),
        scratch_shapes=[pltpu.VMEM((tm, D_MODEL), BF16), pltpu.VMEM((tm, D_MODEL), F32)],
    )
    return pl.pallas_call(
        _moe_kernel,
        grid_spec=grid_spec,
        out_shape=jax.ShapeDtypeStruct((rows, D_MODEL), F32),
        input_output_aliases={3: 0},
        compiler_params=_params("arbitrary", "arbitrary"),
        name="moe_experts",
    )(tile_blk, tile_exp, tile_valid, xs, wg, wu, wd)


def _combine_kernel(pos_ref, x_ref, meta_ref, ys_ref, o_ref, ybuf, sem, *, rows):
    def issue(r, carry):
        _row_copy(ys_ref, pos_ref[0, 0, r], ybuf, r, sem).start(priority=0)
        _row_copy(ys_ref, pos_ref[0, 0, rows + r], ybuf, rows + r, sem).start(priority=1)
        return carry

    lax.fori_loop(0, rows, issue, 0)
    pltpu.make_async_copy(ys_ref.at[pl.ds(0, 2 * rows)], ybuf, sem).wait()
    meta = meta_ref[...]
    w0 = meta[:, META_W0:META_W0 + 1]
    w1 = meta[:, META_W1:META_W1 + 1]
    o_ref[...] = x_ref[...] + w0 * ybuf[0:rows, :] + w1 * ybuf[rows:2 * rows, :]


def _combine(pos, x2, meta, ys):
    n = x2.shape[0]
    r = SCATTER_R
    return pl.pallas_call(
        functools.partial(_combine_kernel, rows=r),
        grid=(n // r,),
        in_specs=[
            pl.BlockSpec((1, 1, 2 * r), lambda i: (i, 0, 0), memory_space=pltpu.SMEM),
            pl.BlockSpec((r, D_MODEL), lambda i: (i, 0)),
            pl.BlockSpec((r, LANES), lambda i: (i, 0)),
            pl.BlockSpec(memory_space=pl.ANY),
        ],
        out_specs=pl.BlockSpec((r, D_MODEL), lambda i: (i, 0)),
        out_shape=jax.ShapeDtypeStruct((n, D_MODEL), F32),
        scratch_shapes=[pltpu.VMEM((2 * r, D_MODEL), F32), pltpu.SemaphoreType.DMA(())],
        compiler_params=_params("arbitrary"),
        name="moe_combine",
    )(pos, x2, meta, ys)


def _moe(x2, g, router_w, wg, wu, wd):
    n = x2.shape[0]
    tm = MOE_TM
    rw = jnp.zeros((D_MODEL, LANES), F32).at[:, :N_EXPERTS].set(router_w)
    h, meta, cnt = _router(x2, g, rw)
    counts = cnt[0, :N_EXPERTS].astype(jnp.int32)
    padded = ((counts + tm - 1) // tm) * tm
    ends = jnp.cumsum(padded)
    offs = ends - padded
    n_tiles = (2 * n) // tm + N_EXPERTS
    e0 = meta[:, META_E0].astype(jnp.int32)
    e1 = meta[:, META_E1].astype(jnp.int32)
    pos0 = offs[e0] + meta[:, META_R0].astype(jnp.int32)
    pos1 = offs[e1] + meta[:, META_R1].astype(jnp.int32)
    r = SCATTER_R
    pos = jnp.concatenate([pos0.reshape(n // r, 1, r), pos1.reshape(n // r, 1, r)], axis=-1)
    starts = jnp.arange(n_tiles, dtype=jnp.int32) * tm
    n_valid = ends[-1] // tm
    tile_valid = (starts < ends[-1]).astype(jnp.int32)
    tile_blk = jnp.minimum(jnp.arange(n_tiles, dtype=jnp.int32), n_valid - 1)
    tile_exp = jnp.sum((tile_blk[:, None] * tm >= ends[None, :]).astype(jnp.int32), axis=1)
    xs = _scatter(pos, h, jnp.zeros((n_tiles * tm, D_MODEL), F32))
    ys = _moe_experts(tile_blk, tile_exp, tile_valid, xs, wg, wu, wd)
    return _combine(pos, x2, meta, ys)


def _mixer(x2, batch, seq, rel_bias, bias_tiles, norm_g, w_in, conv_w, conv_b, igate_b, fgate_b, mlstm_norm_g,
           q_norm_g, k_norm_g, w_branch_a, w_branch_b, w_out):
    sizes = (M_WIDTH, M_WIDTH, M_WIDTH, M_WIDTH, M_HEADS, M_HEADS, A_WIDTH, A_WIDTH, A_WIDTH, D_MODEL, D_MODEL)
    cuts = [0]
    for s in sizes:
        cuts.append(cuts[-1] + s)
    mq, mk, mv, mo, mi, mf, aq, ak, av, ga, gb = [w_in[:, cuts[i]:cuts[i + 1]] for i in range(len(sizes))]
    wn = jnp.concatenate([ga, gb, mq, mk, mv, mo, ak], axis=1).astype(BF16)
    wt = jnp.concatenate([aq, av], axis=1).T.astype(BF16)
    wgate = jnp.zeros((D_MODEL, LANES), F32).at[:, :M_HEADS].set(mi).at[:, M_HEADS:2 * M_HEADS].set(mf).astype(BF16)
    gate_b = jnp.zeros((1, LANES), F32).at[0, :M_HEADS].set(igate_b).at[0, M_HEADS:2 * M_HEADS].set(fgate_b)
    z, zt, gates = _in_proj(x2, norm_g.reshape(1, D_MODEL), wn, wt, wgate)
    ha = _mlstm(z, gates, conv_w, conv_b.reshape(1, -1), gate_b, mlstm_norm_g.reshape(1, -1), batch, seq)
    qg_col = jnp.tile(q_norm_g, 2).reshape(2 * A_HEAD_DIM, 1)
    kg_row = jnp.tile(k_norm_g, 2).reshape(1, 2 * A_HEAD_DIM)
    ob = _moba(z, zt, bias_tiles, _moba_logit_bound(rel_bias, q_norm_g, k_norm_g), qg_col, kg_row, batch, seq)
    return _merge(x2, ha, ob, z, w_branch_a.astype(BF16), w_branch_b.astype(BF16), w_out.astype(BF16))


def kernel(x, rel_bias, mix_norm_g, w_in, conv_w, conv_b, igate_b, fgate_b, mlstm_norm_g, q_norm_g, k_norm_g,
           w_branch_a, w_branch_b, w_out, ffn_norm_g, dense_w_gate, dense_w_up, dense_w_down, router_w,
           expert_w_gate, expert_w_up, expert_w_down):
    batch, seq, d = x.shape
    depth = w_in.shape[0]
    assert d == D_MODEL and seq % MLSTM_T == 0 and seq % MOBA_BLOCK == 0
    assert (batch * seq) % MOE_TM == 0
    x2 = x.reshape(batch * seq, d)
    bias_tiles = _moba_bias_tiles(rel_bias)
    for layer in range(depth):
        x2 = _mixer(x2, batch, seq, rel_bias, bias_tiles, mix_norm_g[layer], w_in[layer], conv_w[layer], conv_b[layer],
                    igate_b[layer], fgate_b[layer], mlstm_norm_g[layer], q_norm_g[layer], k_norm_g[layer],
                    w_branch_a[layer], w_branch_b[layer], w_out[layer])
        g = ffn_norm_g[layer].reshape(1, d)
        j = layer // 2
        if layer % 2 == 0:
            x2 = _ffn(x2, g, dense_w_gate[j].astype(BF16), dense_w_up[j].astype(BF16), dense_w_down[j].astype(BF16))
        else:
            x2 = _moe(x2, g, router_w[j], expert_w_gate[j].astype(BF16), expert_w_up[j].astype(BF16),
                      expert_w_down[j].astype(BF16))
    return x2.reshape(batch, seq, d)
```

```python
import functools
import math

import jax
import jax.numpy as jnp
from jax import lax
from jax.experimental import pallas as pl
from jax.experimental.pallas import tpu as pltpu

F32 = jnp.float32
BF16 = jnp.bfloat16
HIGHEST = lax.Precision.HIGHEST

D_MODEL = 1024
M_HEADS = 4
M_HEAD_DIM = 128
M_WIDTH = M_HEADS * M_HEAD_DIM
CONV_WIDTH = 4
A_HEADS = 8
A_HEAD_DIM = 64
A_WIDTH = A_HEADS * A_HEAD_DIM
MOBA_BLOCK = 256
MOBA_TOPK = 3
REL_BUCKETS = 32
REL_MAX_DIST = 1024
N_EXPERTS = 8
EPS = 1e-6

LANES = 128
SUBLANES = 8
NEG = -1e30
VMEM_LIMIT = 56 * 1024 * 1024

C_GA = 0
C_GB = D_MODEL
C_MQ = 2 * D_MODEL
C_MK = C_MQ + M_WIDTH
C_MV = C_MK + M_WIDTH
C_MO = C_MV + M_WIDTH
C_AK = C_MO + M_WIDTH
NAT_WIDTH = C_AK + A_WIDTH
N_BIAS_TILES = 6

PROJ_TM = 512
MLSTM_T = 512
MLSTM_CHUNK = 128
MERGE_TM = 512
FFN_TM = 512
FFN_CHUNK = 256
ROUTER_TM = 512
MOE_TM = 512
MOE_TF = 1792
SCATTER_R = 256


def _params(*sem):
    return pltpu.CompilerParams(dimension_semantics=sem, vmem_limit_bytes=VMEM_LIMIT)


def _const_spec(shape):
    nd = len(shape)
    return pl.BlockSpec(shape, lambda *_: (0,) * nd)


def _rms(x, g):
    return x * lax.rsqrt(jnp.mean(x * x, axis=-1, keepdims=True) + EPS) * g


def _in_proj_kernel(x_ref, g_ref, wn_ref, wt_ref, wg_ref, z_ref, zt_ref, gt_ref):
    hn = _rms(x_ref[...], g_ref[...]).astype(BF16)
    for c in range(0, NAT_WIDTH, 512):
        z_ref[:, c:c + 512] = jnp.dot(hn, wn_ref[:, c:c + 512], preferred_element_type=F32).astype(BF16)
    for c in range(0, 2 * A_WIDTH, 256):
        zt_ref[c:c + 256, :] = lax.dot_general(
            wt_ref[c:c + 256, :], hn, (((1,), (1,)), ((), ())), preferred_element_type=F32).astype(BF16)
    gt_ref[...] = jnp.dot(hn, wg_ref[...], preferred_element_type=F32)


def _in_proj(x2, g, wn, wt, wg):
    n = x2.shape[0]
    tm = PROJ_TM
    return pl.pallas_call(
        _in_proj_kernel,
        grid=(n // tm,),
        in_specs=[
            pl.BlockSpec((tm, D_MODEL), lambda i: (i, 0)),
            _const_spec((1, D_MODEL)),
            _const_spec((D_MODEL, NAT_WIDTH)),
            _const_spec((2 * A_WIDTH, D_MODEL)),
            _const_spec((D_MODEL, LANES)),
        ],
        out_specs=[
            pl.BlockSpec((tm, NAT_WIDTH), lambda i: (i, 0)),
            pl.BlockSpec((2 * A_WIDTH, tm), lambda i: (0, i)),
            pl.BlockSpec((tm, LANES), lambda i: (i, 0)),
        ],
        out_shape=[
            jax.ShapeDtypeStruct((n, NAT_WIDTH), BF16),
            jax.ShapeDtypeStruct((2 * A_WIDTH, n), BF16),
            jax.ShapeDtypeStruct((n, LANES), F32),
        ],
        compiler_params=_params("arbitrary"),
        name="in_proj",
    )(x2, g, wn, wt, wg)


def _mlstm_kernel(zq_ref, zk_ref, zv_ref, zo_ref, gt_ref, cw_ref, cb_ref, gb_ref, ng_ref, o_ref,
                  qk_buf, q_s, k_s, c_s, m_s, *, t_blk, chunk):
    L = chunk
    DH = M_HEAD_DIM
    assert L == LANES and DH == LANES

    @pl.when(pl.program_id(1) == 0)
    def _init():
        qk_buf[0:SUBLANES, :] = jnp.zeros((SUBLANES, 2 * M_WIDTH), F32)
        c_s[...] = jnp.zeros_like(c_s)
        m_s[...] = jnp.zeros_like(m_s)

    qk_buf[SUBLANES:SUBLANES + t_blk, 0:M_WIDTH] = zq_ref[...].astype(F32)
    qk_buf[SUBLANES:SUBLANES + t_blk, M_WIDTH:] = zk_ref[...].astype(F32)
    acc = cb_ref[...] + cw_ref[CONV_WIDTH - 1:CONV_WIDTH, :] * qk_buf[SUBLANES:SUBLANES + t_blk, :]
    for j in range(CONV_WIDTH - 1):
        off = SUBLANES - (CONV_WIDTH - 1) + j
        acc = acc + cw_ref[j:j + 1, :] * qk_buf[off:off + t_blk, :]
    qk = acc * jax.nn.sigmoid(acc)
    qk_buf[0:SUBLANES, :] = qk_buf[t_blk:t_blk + SUBLANES, :]
    q_s[...] = qk[:, :M_WIDTH].astype(BF16)
    k_s[...] = qk[:, M_WIDTH:] * (DH ** -0.5)

    row = lax.broadcasted_iota(jnp.int32, (L, L), 0)
    col = lax.broadcasted_iota(jnp.int32, (L, L), 1)
    causal = col <= row
    tri = causal.astype(F32)
    lane = lax.broadcasted_iota(jnp.int32, (L, LANES), 1)
    ones_blk = jnp.ones((L, LANES), BF16)

    def chunk_body(c, carry):
        r0 = pl.multiple_of(c * L, L)
        g_pre = gt_ref[pl.ds(r0, L), :] + gb_ref[...]
        log_f = jnp.minimum(g_pre, 0.0) - jnp.log1p(jnp.exp(-jnp.abs(g_pre)))
        bcum = jnp.dot(tri, log_f, precision=HIGHEST, preferred_element_type=F32)
        gm = jnp.where(lane < M_HEADS, g_pre, bcum)
        gm_t = gm.T
        c_all = [c_s[h] for h in range(M_HEADS)]
        m_all = m_s[...]
        heads = range(M_HEADS)
        hs = [slice(h * DH, (h + 1) * DH) for h in heads]
        qh = [q_s[pl.ds(r0, L), hs[h]] for h in heads]
        kf = [k_s[pl.ds(r0, L), hs[h]] for h in heads]
        v_ext = [jnp.concatenate([zv_ref[pl.ds(r0, L), hs[h]], ones_blk], axis=1) for h in heads]
        qk = [lax.dot_general(qh[h], kf[h].astype(BF16), (((1,), (1,)), ((), ())), preferred_element_type=F32)
              for h in heads]
        inter = [jnp.dot(qh[h], c_all[h].astype(BF16), preferred_element_type=F32) for h in heads]
        mt, w_state, s_bf, kw_t, decay, m_new = [], [], [], [], [], []
        for h in heads:
            bb = jnp.broadcast_to(gm[:, M_HEADS + h:M_HEADS + h + 1], (L, LANES))
            ii = jnp.broadcast_to(gm[:, h:h + 1], (L, LANES))
            bb_row = gm_t[M_HEADS + h:M_HEADS + h + 1, :]
            ii_row = gm_t[h:h + 1, :]
            m_old = m_all[h:h + 1, :]
            dlog = jnp.where(causal, bb - (bb_row - ii_row), -jnp.inf)
            a = bb + m_old
            mt.append(jnp.maximum(a, jnp.broadcast_to(jnp.max(dlog, axis=-1, keepdims=True), (L, LANES))))
            w_state.append(jnp.exp(a - mt[h]))
            s_bf.append((qk[h] * jnp.exp(dlog - mt[h])).astype(BF16))
            b_last = bb[L - 1:L, :]
            g = b_last - bb + ii
            m_new.append(jnp.maximum(b_last + m_old, jnp.max(g, axis=0, keepdims=True)))
            decay.append(jnp.exp(b_last + m_old - m_new[h]))
            kw_t.append((kf[h] * jnp.exp(g - m_new[h])).T.astype(BF16))
        intra = [jnp.dot(s_bf[h], v_ext[h], preferred_element_type=F32) for h in heads]
        upd = [jnp.dot(kw_t[h], v_ext[h], preferred_element_type=F32) for h in heads]
        for h in heads:
            num = intra[h][:, :DH] + w_state[h] * inter[h][:, :DH]
            den = intra[h][:, DH:] + w_state[h] * inter[h][:, DH:]
            h_t = num / jnp.maximum(jnp.abs(den), jnp.exp(-mt[h]))
            hc = jax.nn.sigmoid(zo_ref[pl.ds(r0, L), hs[h]].astype(F32)) * h_t
            o_ref[pl.ds(r0, L), hs[h]] = _rms(hc, ng_ref[:, hs[h]]).astype(BF16)
        for h in heads:
            c_s[h] = jnp.concatenate([decay[h], decay[h]], axis=1) * c_all[h] + upd[h]
            m_s[h:h + 1, :] = m_new[h]
        return carry

    lax.fori_loop(0, t_blk // L, chunk_body, 0)


def _mlstm(z, gates, conv_w, conv_b, gate_b, norm_g, batch, seq):
    n = batch * seq
    t = MLSTM_T
    nt = seq // t
    row_blk = lambda b, s: b * nt + s
    zspec = lambda cb: pl.BlockSpec((t, M_WIDTH), lambda b, s: (row_blk(b, s), cb))
    return pl.pallas_call(
        functools.partial(_mlstm_kernel, t_blk=t, chunk=MLSTM_CHUNK),
        grid=(batch, nt),
        in_specs=[
            zspec(C_MQ // M_WIDTH), zspec(C_MK // M_WIDTH), zspec(C_MV // M_WIDTH), zspec(C_MO // M_WIDTH),
            pl.BlockSpec((t, LANES), lambda b, s: (row_blk(b, s), 0)),
            _const_spec((CONV_WIDTH, 2 * M_WIDTH)),
            _const_spec((1, 2 * M_WIDTH)),
            _const_spec((1, LANES)),
            _const_spec((1, M_WIDTH)),
        ],
        out_specs=pl.BlockSpec((t, M_WIDTH), lambda b, s: (row_blk(b, s), 0)),
        out_shape=jax.ShapeDtypeStruct((n, M_WIDTH), BF16),
        scratch_shapes=[
            pltpu.VMEM((t + SUBLANES, 2 * M_WIDTH), F32),
            pltpu.VMEM((t, M_WIDTH), BF16),
            pltpu.VMEM((t, M_WIDTH), F32),
            pltpu.VMEM((M_HEADS, M_HEAD_DIM, M_HEAD_DIM + LANES), F32),
            pltpu.VMEM((SUBLANES, LANES), F32),
        ],
        compiler_params=_params("arbitrary", "arbitrary"),
        name="mlstm",
    )(z, z, z, z, gates, conv_w, conv_b, gate_b, norm_g)


MOBA_V_ROWS = A_HEAD_DIM + 16
MOBA_ROWB_ROWS = 24
MOBA_GROUP = 34
LOG2E = math.log2(math.e)
FAST_SOFTMAX_MIN_DENOM = 1e-25


def _moba_items(n_blocks):
    items = []
    for qb in range(n_blocks):
        items.append((qb, qb, 0, 0))
        for j in range(qb):
            items.append((qb, j, min(qb - j, N_BIAS_TILES - 1), j + 1))
    n_groups = -(-len(items) // MOBA_GROUP)
    noop = (n_blocks - 1, 0, 0, MOBA_ROWB_ROWS - 1)
    items += [noop] * (n_groups * MOBA_GROUP + 2 - len(items))
    return n_groups, [jnp.asarray([it[c] for it in items], jnp.int32) for c in range(4)]


def _moba_kernel(it_q, it_blk, it_tile, it_row, qt_ref, k_ref, vt_ref, bias_ref, bound_ref, qg_ref, kg_ref, o_ref,
                 kn_s, vt_s, kmean_s, qh_s, rowb_s, acc_s, st_a, st_b, p_a, p_b, *, n_blocks, n_groups):
    BS = MOBA_BLOCK
    DA = A_HEAD_DIM
    seq = n_blocks * BS
    lane_k = lax.broadcasted_iota(jnp.int32, (BS, LANES), 1)
    head0_k = lane_k < DA

    ones_row = jnp.where(lax.broadcasted_iota(jnp.int32, (16, BS), 0) == 0, 1.0, 0.0).astype(BF16)
    for j in range(n_blocks):
        kf = k_ref[j * BS:(j + 1) * BS, :].astype(F32)
        k2 = kf * kf
        s0 = jnp.sum(jnp.where(head0_k, k2, 0.0), axis=-1, keepdims=True)
        s1 = jnp.sum(jnp.where(head0_k, 0.0, k2), axis=-1, keepdims=True)
        inv = jnp.where(head0_k, lax.rsqrt(s0 / DA + EPS), lax.rsqrt(s1 / DA + EPS))
        kn = kf * inv * kg_ref[...]
        kn_s[j * BS:(j + 1) * BS, :] = kn.astype(BF16)
        kmean_s[j:j + 1, :] = jnp.mean(kn, axis=0, keepdims=True)
        for h in range(2):
            vt_s[j, h, 0:DA, :] = vt_ref[h * DA:(h + 1) * DA, j * BS:(j + 1) * BS]
            vt_s[j, h, DA:MOBA_V_ROWS, :] = ones_row

    qf = qt_ref[...].astype(F32)
    sub_q = lax.broadcasted_iota(jnp.int32, qf.shape, 0)
    head0_q = sub_q < DA
    q2 = qf * qf
    ss0 = jnp.sum(jnp.where(head0_q, q2, 0.0), axis=0, keepdims=True)
    ss1 = jnp.sum(jnp.where(head0_q, 0.0, q2), axis=0, keepdims=True)
    qn = qf * jnp.where(head0_q, lax.rsqrt(ss0 / DA + EPS), lax.rsqrt(ss1 / DA + EPS)) * qg_ref[...]

    blk = lax.broadcasted_iota(jnp.int32, (n_blocks, seq), 0)
    past = blk < lax.broadcasted_iota(jnp.int32, (n_blocks, seq), 1) // BS
    lane_m = lax.broadcasted_iota(jnp.int32, (n_blocks, LANES), 1)
    kmean = kmean_s[...]
    kmean2 = jnp.concatenate([jnp.where(lane_m < DA, kmean, 0.0), jnp.where(lane_m < DA, 0.0, kmean)], axis=0)
    gates = jnp.dot(kmean2, qn, precision=HIGHEST, preferred_element_type=F32)
    for h in range(2):
        gate = jnp.where(past, gates[h * n_blocks:(h + 1) * n_blocks, :], -jnp.inf)
        rank = jnp.zeros((n_blocks, seq), jnp.int32)
        for j2 in range(n_blocks):
            other = gate[j2:j2 + 1, :]
            beats = (other > gate) | ((other == gate) & (j2 < blk))
            rank = rank + beats.astype(jnp.int32)
        bound = bound_ref[h, :, 0:1]
        selb = jnp.where(past & (rank < MOBA_TOPK), 0.0, NEG) - bound
        hmask_q = head0_q if h == 0 else jnp.logical_not(head0_q)
        qh = jnp.where(hmask_q, qn * (DA ** -0.5 * LOG2E), 0.0).astype(BF16)
        for qb in range(n_blocks):
            cols = slice(qb * BS, (qb + 1) * BS)
            qh_s[qb, h] = qh[:, cols]
            rowb_s[qb, h, 0:1, :] = jnp.broadcast_to(-bound, (1, BS))
            rowb_s[qb, h, 1:n_blocks + 1, :] = selb[:, cols]
            rowb_s[qb, h, n_blocks + 1:, :] = jnp.full((MOBA_ROWB_ROWS - n_blocks - 1, BS), NEG, F32)
    acc_s[...] = jnp.zeros_like(acc_s)

    def scores(h, qb, j, tile, row):
        kj = kn_s[pl.ds(pl.multiple_of(j * BS, BS), BS), :]
        st = jnp.dot(kj, qh_s[qb, h], preferred_element_type=F32)
        return st + bias_ref[h, tile] + rowb_s[qb, h, pl.ds(row, 1), :]

    def pv(h, j, p):
        return jnp.dot(vt_s[j, h], p, preferred_element_type=F32)

    def stage_scores(i, st_ref):
        for h in range(2):
            st_ref[h] = scores(h, it_q[i], it_blk[i], it_tile[i], it_row[i])

    def stage_exp(st_ref, p_ref):
        for h in range(2):
            p_ref[h] = jnp.exp2(st_ref[h]).astype(BF16)

    def stage_pv(i, p_ref):
        for h in range(2):
            acc_s[it_q[i], h] += pv(h, it_blk[i], p_ref[h])

    def group(m, carry):
        for u in range(0, MOBA_GROUP, 2):
            i = MOBA_GROUP * m + u
            stage_pv(i, p_a)
            stage_scores(i + 2, st_a)
            stage_exp(st_b, p_b)
            stage_pv(i + 1, p_b)
            stage_scores(i + 3, st_b)
            stage_exp(st_a, p_a)
        return carry

    stage_scores(0, st_a)
    stage_exp(st_a, p_a)
    stage_scores(1, st_b)
    lax.fori_loop(0, n_groups, group, 0)

    def finish(qb, l_min):
        outs = []
        for h in range(2):
            acc = acc_s[qb, h]
            l = acc[DA:DA + 1, :]
            outs.append(acc[0:DA, :] / l)
            l_min = jnp.minimum(l_min, jnp.min(l))
        o_ref[pl.ds(pl.multiple_of(qb * BS, BS), BS), :] = jnp.concatenate(outs, axis=0).T.astype(BF16)
        return l_min

    l_min = lax.fori_loop(0, n_blocks, finish, jnp.float32(jnp.inf))

    def online(qb, carry):
        state = []
        for h in range(2):
            st = scores(h, qb, qb, 0, 0)
            m = jnp.max(st, axis=0, keepdims=True)
            full = pv(h, qb, jnp.exp2(st - m).astype(BF16))
            state += [m, full]

        def past_body(j, state):
            tile = jnp.minimum(qb - j, N_BIAS_TILES - 1)
            new = []
            for h in range(2):
                m_old, full_old = state[2 * h:2 * h + 2]
                st = scores(h, qb, j, tile, j + 1)
                m_new = jnp.maximum(m_old, jnp.max(st, axis=0, keepdims=True))
                full = pv(h, j, jnp.exp2(st - m_new).astype(BF16))
                new += [m_new, jnp.exp2(m_old - m_new) * full_old + full]
            return tuple(new)

        state = lax.fori_loop(0, qb, past_body, tuple(state))
        outs = [state[2 * h + 1][0:DA, :] / state[2 * h + 1][DA:DA + 1, :] for h in range(2)]
        o_ref[pl.ds(pl.multiple_of(qb * BS, BS), BS), :] = jnp.concatenate(outs, axis=0).T.astype(BF16)
        return carry

    @pl.when(l_min < FAST_SOFTMAX_MIN_DENOM)
    def _redo():
        lax.fori_loop(0, n_blocks, online, 0)


def _moba_logit_bound(rel_bias, q_norm_g, k_norm_g):
    qk = A_HEAD_DIM * jnp.max(jnp.abs(q_norm_g)) * jnp.max(jnp.abs(k_norm_g)) * (A_HEAD_DIM ** -0.5) * 1.02
    b = (qk + jnp.max(rel_bias, axis=0)) * LOG2E
    return jnp.broadcast_to(b[:, None, None], (A_HEADS, 1, LANES)).astype(F32)


def _moba(z, zt, bias_tiles, bound, qg_col, kg_row, batch, seq):
    n = batch * seq
    nb = seq // MOBA_BLOCK
    bs = MOBA_BLOCK
    hp = A_HEADS // 2
    w2 = 2 * A_HEAD_DIM
    assert nb + 2 <= MOBA_ROWB_ROWS
    n_groups, items = _moba_items(nb)
    grid_spec = pltpu.PrefetchScalarGridSpec(
        num_scalar_prefetch=len(items),
        grid=(batch, hp),
        in_specs=[
            pl.BlockSpec((w2, seq), lambda b, p, *_: (p, b)),
            pl.BlockSpec((seq, w2), lambda b, p, *_: (b, C_AK // w2 + p)),
            pl.BlockSpec((w2, seq), lambda b, p, *_: (hp + p, b)),
            pl.BlockSpec((2, N_BIAS_TILES, bs, bs), lambda b, p, *_: (p, 0, 0, 0)),
            pl.BlockSpec((2, 1, LANES), lambda b, p, *_: (p, 0, 0)),
            pl.BlockSpec((w2, 1), lambda b, p, *_: (0, 0)),
            pl.BlockSpec((1, w2), lambda b, p, *_: (0, 0)),
        ],
        out_specs=pl.BlockSpec((seq, w2), lambda b, p, *_: (b, p)),
        scratch_shapes=[
            pltpu.VMEM((seq, w2), BF16),
            pltpu.VMEM((nb, 2, MOBA_V_ROWS, bs), BF16),
            pltpu.VMEM((nb, w2), F32),
            pltpu.VMEM((nb, 2, w2, bs), BF16),
            pltpu.VMEM((nb, 2, MOBA_ROWB_ROWS, bs), F32),
            pltpu.VMEM((nb, 2, MOBA_V_ROWS, bs), F32),
            pltpu.VMEM((2, bs, bs), F32),
            pltpu.VMEM((2, bs, bs), F32),
            pltpu.VMEM((2, bs, bs), BF16),
            pltpu.VMEM((2, bs, bs), BF16),
        ],
    )
    return pl.pallas_call(
        functools.partial(_moba_kernel, n_blocks=nb, n_groups=n_groups),
        grid_spec=grid_spec,
        out_shape=jax.ShapeDtypeStruct((n, A_WIDTH), BF16),
        compiler_params=_params("arbitrary", "arbitrary"),
        name="moba",
    )(*items, zt, z, zt, bias_tiles, bound, qg_col, kg_row)


def _t5_bucket(dist):
    n = jnp.maximum(dist, 0)
    max_exact = REL_BUCKETS // 2
    log_ratio = jnp.log(jnp.maximum(n, max_exact).astype(F32) / max_exact) / math.log(REL_MAX_DIST / max_exact)
    large = max_exact + (log_ratio * (REL_BUCKETS - max_exact)).astype(jnp.int32)
    large = jnp.minimum(large, REL_BUCKETS - 1)
    return jnp.where(n < max_exact, n, large)


def _bias_tiles_kernel(rb_ref, bucket_ref, o_ref):
    bucket = bucket_ref[0]
    hit = [bucket == b for b in range(REL_BUCKETS)]
    tk = lax.broadcasted_iota(jnp.int32, bucket.shape, 0)
    tq = lax.broadcasted_iota(jnp.int32, bucket.shape, 1)
    masked = (tk > tq) & (pl.program_id(0) == 0)
    for h in range(A_HEADS):
        acc = jnp.zeros(bucket.shape, F32)
        for b in range(REL_BUCKETS):
            acc = jnp.where(hit[b], rb_ref[b, h], acc)
        o_ref[h, 0] = jnp.where(masked, NEG, acc * LOG2E)


def _moba_bias_tiles(rel_bias):
    assert (N_BIAS_TILES - 1) * MOBA_BLOCK - (MOBA_BLOCK - 1) >= REL_MAX_DIST
    tk = jnp.arange(MOBA_BLOCK)[None, :, None]
    tq = jnp.arange(MOBA_BLOCK)[None, None, :]
    diff = jnp.arange(N_BIAS_TILES)[:, None, None]
    bucket = _t5_bucket(diff * MOBA_BLOCK + tq - tk).astype(jnp.int32)
    return pl.pallas_call(
        _bias_tiles_kernel,
        grid=(N_BIAS_TILES,),
        in_specs=[
            pl.BlockSpec(memory_space=pltpu.SMEM),
            pl.BlockSpec((1, MOBA_BLOCK, MOBA_BLOCK), lambda t: (t, 0, 0)),
        ],
        out_specs=pl.BlockSpec((A_HEADS, 1, MOBA_BLOCK, MOBA_BLOCK), lambda t: (0, t, 0, 0)),
        out_shape=jax.ShapeDtypeStruct((A_HEADS, N_BIAS_TILES, MOBA_BLOCK, MOBA_BLOCK), F32),
        compiler_params=_params("arbitrary"),
        name="moba_bias_tiles",
    )(rel_bias.astype(F32), bucket)


def _merge_kernel(x_ref, ha_ref, ob_ref, ga_ref, gb_ref, wa_ref, wb_ref, wo_ref, o_ref):
    ya = jnp.dot(ha_ref[...], wa_ref[...], preferred_element_type=F32)
    yb = jnp.dot(ob_ref[...], wb_ref[...], preferred_element_type=F32)
    y = jax.nn.sigmoid(ga_ref[...].astype(F32)) * ya + jax.nn.sigmoid(gb_ref[...].astype(F32)) * yb
    o_ref[...] = x_ref[...] + jnp.dot(y.astype(BF16), wo_ref[...], preferred_element_type=F32)


def _merge(x2, ha, ob, z, wa, wb, wo):
    n = x2.shape[0]
    tm = MERGE_TM
    return pl.pallas_call(
        _merge_kernel,
        grid=(n // tm,),
        in_specs=[
            pl.BlockSpec((tm, D_MODEL), lambda i: (i, 0)),
            pl.BlockSpec((tm, M_WIDTH), lambda i: (i, 0)),
            pl.BlockSpec((tm, A_WIDTH), lambda i: (i, 0)),
            pl.BlockSpec((tm, D_MODEL), lambda i: (i, C_GA // D_MODEL)),
            pl.BlockSpec((tm, D_MODEL), lambda i: (i, C_GB // D_MODEL)),
            _const_spec((M_WIDTH, D_MODEL)),
            _const_spec((A_WIDTH, D_MODEL)),
            _const_spec((D_MODEL, D_MODEL)),
        ],
        out_specs=pl.BlockSpec((tm, D_MODEL), lambda i: (i, 0)),
        out_shape=jax.ShapeDtypeStruct((n, D_MODEL), F32),
        compiler_params=_params("arbitrary"),
        name="merge",
    )(x2, ha, ob, z, z, wa, wb, wo)


def _ffn_kernel(x_ref, g_ref, wg_ref, wu_ref, wd_ref, o_ref, *, d_ff):
    x = x_ref[...]
    hn = _rms(x, g_ref[...]).astype(BF16)
    acc = x
    for c in range(0, d_ff, FFN_CHUNK):
        g = jnp.dot(hn, wg_ref[:, c:c + FFN_CHUNK], preferred_element_type=F32)
        u = jnp.dot(hn, wu_ref[:, c:c + FFN_CHUNK], preferred_element_type=F32)
        a = (g * jax.nn.sigmoid(g) * u).astype(BF16)
        acc = acc + jnp.dot(a, wd_ref[c:c + FFN_CHUNK, :], preferred_element_type=F32)
    o_ref[...] = acc


def _ffn(x2, g, wg, wu, wd):
    n = x2.shape[0]
    d_ff = wg.shape[1]
    assert d_ff % FFN_CHUNK == 0
    tm = FFN_TM
    return pl.pallas_call(
        functools.partial(_ffn_kernel, d_ff=d_ff),
        grid=(n // tm,),
        in_specs=[
            pl.BlockSpec((tm, D_MODEL), lambda i: (i, 0)),
            _const_spec((1, D_MODEL)),
            _const_spec((D_MODEL, d_ff)),
            _const_spec((D_MODEL, d_ff)),
            _const_spec((d_ff, D_MODEL)),
        ],
        out_specs=pl.BlockSpec((tm, D_MODEL), lambda i: (i, 0)),
        out_shape=jax.ShapeDtypeStruct((n, D_MODEL), F32),
        compiler_params=_params("arbitrary"),
        name="ffn_dense",
    )(x2, g, wg, wu, wd)


META_W0, META_W1, META_E0, META_E1, META_R0, META_R1 = range(6)


def _router_kernel(x_ref, g_ref, rw_ref, h_ref, meta_ref, cnt_ref, carry_s):
    @pl.when(pl.program_id(0) == 0)
    def _init():
        carry_s[...] = jnp.zeros_like(carry_s)

    h = _rms(x_ref[...], g_ref[...])
    h_ref[...] = h
    tm = h.shape[0]
    logits = jnp.dot(h, rw_ref[...], precision=HIGHEST, preferred_element_type=F32)
    lane = lax.broadcasted_iota(jnp.int32, (tm, LANES), 1)
    lg = jnp.where(lane < N_EXPERTS, logits, -jnp.inf)
    m1 = jnp.max(lg, axis=-1, keepdims=True)
    i1 = jnp.min(jnp.where(lg == m1, lane, LANES), axis=-1, keepdims=True)
    lg2 = jnp.where(lane == i1, -jnp.inf, lg)
    m2 = jnp.max(lg2, axis=-1, keepdims=True)
    i2 = jnp.min(jnp.where(lg2 == m2, lane, LANES), axis=-1, keepdims=True)
    e = jnp.exp(m2 - m1)
    w1 = 1.0 / (1.0 + e)
    w2 = e / (1.0 + e)
    hit1 = lane == i1
    hit2 = lane == i2
    onehot = jnp.where(hit1 | hit2, 1.0, 0.0)
    r = lax.broadcasted_iota(jnp.int32, (tm, tm), 0)
    c = lax.broadcasted_iota(jnp.int32, (tm, tm), 1)
    before = jnp.where(c < r, 1.0, 0.0).astype(BF16)
    carry = carry_s[0:1, :]
    pref = jnp.dot(before, onehot.astype(BF16), preferred_element_type=F32) + carry
    r1 = jnp.sum(jnp.where(hit1, pref, 0.0), axis=-1, keepdims=True)
    r2 = jnp.sum(jnp.where(hit2, pref, 0.0), axis=-1, keepdims=True)
    new_carry = carry + jnp.sum(onehot, axis=0, keepdims=True)
    carry_s[...] = jnp.broadcast_to(new_carry, carry_s.shape)
    cnt_ref[...] = jnp.broadcast_to(new_carry, cnt_ref.shape)
    meta = jnp.zeros((tm, LANES), F32)
    for idx, val in ((META_W0, w1), (META_W1, w2), (META_E0, i1.astype(F32)), (META_E1, i2.astype(F32)),
                     (META_R0, r1), (META_R1, r2)):
        meta = jnp.where(lane == idx, val, meta)
    meta_ref[...] = meta


def _router(x2, g, rw):
    n = x2.shape[0]
    tm = ROUTER_TM
    return pl.pallas_call(
        _router_kernel,
        grid=(n // tm,),
        in_specs=[
            pl.BlockSpec((tm, D_MODEL), lambda i: (i, 0)),
            _const_spec((1, D_MODEL)),
            _const_spec((D_MODEL, LANES)),
        ],
        out_specs=[
            pl.BlockSpec((tm, D_MODEL), lambda i: (i, 0)),
            pl.BlockSpec((tm, LANES), lambda i: (i, 0)),
            _const_spec((SUBLANES, LANES)),
        ],
        out_shape=[
            jax.ShapeDtypeStruct((n, D_MODEL), F32),
            jax.ShapeDtypeStruct((n, LANES), F32),
            jax.ShapeDtypeStruct((SUBLANES, LANES), F32),
        ],
        scratch_shapes=[pltpu.VMEM((SUBLANES, LANES), F32)],
        compiler_params=_params("arbitrary"),
        name="moe_router",
    )(x2, g, rw)


def _row_copy(src_ref, src_row, dst_ref, dst_row, sem):
    return pltpu.make_async_copy(src_ref.at[pl.ds(src_row, 1)], dst_ref.at[pl.ds(dst_row, 1)], sem)


ROW_TOKENS_UNROLL = 8


def _row_tokens_kernel(trips_ref, pos0_ref, pos1_ref, tok_ref):
    def clear(c, carry):
        for u in range(ROW_TOKENS_UNROLL):
            tok_ref[c * ROW_TOKENS_UNROLL + u] = 0
        return carry

    lax.fori_loop(0, trips_ref[0], clear, 0)

    def put(c, carry):
        for u in range(ROW_TOKENS_UNROLL):
            t = c * ROW_TOKENS_UNROLL + u
            tok_ref[pos0_ref[t]] = t
            tok_ref[pos1_ref[t]] = t
        return carry

    lax.fori_loop(0, trips_ref[1], put, 0)


def _row_tokens(pos0, pos1, n_rows):
    n_tokens = pos0.shape[0]
    assert n_rows % ROW_TOKENS_UNROLL == 0 and n_tokens % ROW_TOKENS_UNROLL == 0
    trips = jnp.asarray([n_rows // ROW_TOKENS_UNROLL, n_tokens // ROW_TOKENS_UNROLL], jnp.int32)
    smem = pl.BlockSpec(memory_space=pltpu.SMEM)
    return pl.pallas_call(
        _row_tokens_kernel,
        in_specs=[smem, smem, smem],
        out_specs=smem,
        out_shape=jax.ShapeDtypeStruct((n_rows,), jnp.int32),
        name="moe_row_tokens",
    )(trips, pos0, pos1)


def _moe_kernel(exp_ref, valid_ref, tok_ref, h_ref, wg_ref, wu_ref, wd_ref, y_ref, xg_s, xb_s, acc_s, sem,
                *, n_ff):
    del exp_ref
    i = pl.program_id(0)
    f = pl.program_id(1)
    tm = xb_s.shape[0]
    part = tm // n_ff
    slot = i % 2

    def start_gather(tile, sl, first_row, count):
        for r in range(count):
            row = first_row + r
            _row_copy(h_ref, tok_ref[tile * tm + row], xg_s.at[sl], row, sem.at[sl]).start(priority=r % 2)

    def wait_gather(sl):
        pltpu.make_async_copy(h_ref.at[pl.ds(0, tm)], xg_s.at[sl], sem.at[sl]).wait()

    @pl.when((i == 0) & (f == 0))
    def _first_tile():
        start_gather(0, 0, 0, tm)

    @pl.when(f == 0)
    def _arrive():
        wait_gather(slot)

    @pl.when(valid_ref[i] == 0)
    def _idle():
        start_gather(i + 1, 1 - slot, f * part, part)

        @pl.when(f == n_ff - 1)
        def _zero():
            y_ref[...] = jnp.zeros_like(y_ref)

    @pl.when(valid_ref[i] == 1)
    def _tile():
        @pl.when(f == 0)
        def _cast():
            xb_s[...] = xg_s[slot].astype(BF16)

        start_gather(i + 1, 1 - slot, f * part, part)
        xb = xb_s[...]
        d = None
        for c in range(0, wg_ref.shape[2], FFN_CHUNK):
            g = jnp.dot(xb, wg_ref[0, :, c:c + FFN_CHUNK], preferred_element_type=F32)
            u = jnp.dot(xb, wu_ref[0, :, c:c + FFN_CHUNK], preferred_element_type=F32)
            a = (g * jax.nn.sigmoid(g) * u).astype(BF16)
            dc = jnp.dot(a, wd_ref[0, c:c + FFN_CHUNK, :], preferred_element_type=F32)
            d = dc if d is None else d + dc

        @pl.when(f == 0)
        def _first():
            acc_s[...] = d

        @pl.when(f > 0)
        def _rest():
            acc_s[...] += d

        @pl.when(f == n_ff - 1)
        def _out():
            y_ref[...] = acc_s[...]

    @pl.when((i == pl.num_programs(0) - 1) & (f == n_ff - 1))
    def _drain():
        wait_gather(1 - slot)


def _moe_experts(tile_exp, tile_valid, row_tok, h, wg, wu, wd):
    d_ff = wg.shape[2]
    tm, tf = MOE_TM, MOE_TF
    n_tiles = tile_exp.shape[0]
    n_ff = d_ff // tf
    assert d_ff % tf == 0 and tm % n_ff == 0 and row_tok.shape[0] == (n_tiles + 1) * tm
    grid_spec = pltpu.PrefetchScalarGridSpec(
        num_scalar_prefetch=3,
        grid=(n_tiles, n_ff),
        in_specs=[
            pl.BlockSpec(memory_space=pl.ANY),
            pl.BlockSpec((1, D_MODEL, tf), lambda i, f, ex, va, tok: (ex[i], 0, f)),
            pl.BlockSpec((1, D_MODEL, tf), lambda i, f, ex, va, tok: (ex[i], 0, f)),
            pl.BlockSpec((1, tf, D_MODEL), lambda i, f, ex, va, tok: (ex[i], f, 0)),
        ],
        out_specs=pl.BlockSpec((tm, D_MODEL), lambda i, f, ex, va, tok: (i, 0)),
        scratch_shapes=[
            pltpu.VMEM((2, tm, D_MODEL), F32),
            pltpu.VMEM((tm, D_MODEL), BF16),
            pltpu.VMEM((tm, D_MODEL), F32),
            pltpu.SemaphoreType.DMA((2,)),
        ],
    )
    return pl.pallas_call(
        functools.partial(_moe_kernel, n_ff=n_ff),
        grid_spec=grid_spec,
        out_shape=jax.ShapeDtypeStruct((n_tiles * tm, D_MODEL), F32),
        compiler_params=_params("arbitrary", "arbitrary"),
        name="moe_experts",
    )(tile_exp, tile_valid, row_tok, h, wg, wu, wd)


def _combine_kernel(pos_ref, x_ref, meta_ref, ys_ref, o_ref, ybuf, sem, *, rows):
    for r in range(rows):
        _row_copy(ys_ref, pos_ref[0, 0, r], ybuf, r, sem).start(priority=0)
        _row_copy(ys_ref, pos_ref[0, 0, rows + r], ybuf, rows + r, sem).start(priority=1)
    pltpu.make_async_copy(ys_ref.at[pl.ds(0, 2 * rows)], ybuf, sem).wait()
    meta = meta_ref[...]
    w0 = meta[:, META_W0:META_W0 + 1]
    w1 = meta[:, META_W1:META_W1 + 1]
    o_ref[...] = x_ref[...] + w0 * ybuf[0:rows, :] + w1 * ybuf[rows:2 * rows, :]


def _combine(pos, x2, meta, ys):
    n = x2.shape[0]
    r = SCATTER_R
    return pl.pallas_call(
        functools.partial(_combine_kernel, rows=r),
        grid=(n // r,),
        in_specs=[
            pl.BlockSpec((1, 1, 2 * r), lambda i: (i, 0, 0), memory_space=pltpu.SMEM),
            pl.BlockSpec((r, D_MODEL), lambda i: (i, 0)),
            pl.BlockSpec((r, LANES), lambda i: (i, 0)),
            pl.BlockSpec(memory_space=pl.ANY),
        ],
        out_specs=pl.BlockSpec((r, D_MODEL), lambda i: (i, 0)),
        out_shape=jax.ShapeDtypeStruct((n, D_MODEL), F32),
        scratch_shapes=[pltpu.VMEM((2 * r, D_MODEL), F32), pltpu.SemaphoreType.DMA(())],
        compiler_params=_params("arbitrary"),
        name="moe_combine",
    )(pos, x2, meta, ys)


def _moe(x2, g, router_w, wg, wu, wd):
    n = x2.shape[0]
    tm = MOE_TM
    rw = jnp.zeros((D_MODEL, LANES), F32).at[:, :N_EXPERTS].set(router_w)
    h, meta, cnt = _router(x2, g, rw)
    counts = cnt[0, :N_EXPERTS].astype(jnp.int32)
    padded = ((counts + tm - 1) // tm) * tm
    ends = jnp.cumsum(padded)
    offs = ends - padded
    n_tiles = (2 * n) // tm + N_EXPERTS
    e0 = meta[:, META_E0].astype(jnp.int32)
    e1 = meta[:, META_E1].astype(jnp.int32)
    pos0 = offs[e0] + meta[:, META_R0].astype(jnp.int32)
    pos1 = offs[e1] + meta[:, META_R1].astype(jnp.int32)
    r = SCATTER_R
    pos = jnp.concatenate([pos0.reshape(n // r, 1, r), pos1.reshape(n // r, 1, r)], axis=-1)
    starts = jnp.arange(n_tiles, dtype=jnp.int32) * tm
    n_valid = ends[-1] // tm
    tile_valid = (starts < ends[-1]).astype(jnp.int32)
    tile_blk = jnp.minimum(jnp.arange(n_tiles, dtype=jnp.int32), n_valid - 1)
    tile_exp = jnp.sum((tile_blk[:, None] * tm >= ends[None, :]).astype(jnp.int32), axis=1)
    row_tok = _row_tokens(pos0, pos1, (n_tiles + 1) * tm)
    ys = _moe_experts(tile_exp, tile_valid, row_tok, h, wg, wu, wd)
    return _combine(pos, x2, meta, ys)


def _mixer(x2, batch, seq, rel_bias, bias_tiles, norm_g, w_in, conv_w, conv_b, igate_b, fgate_b, mlstm_norm_g,
           q_norm_g, k_norm_g, w_branch_a, w_branch_b, w_out):
    sizes = (M_WIDTH, M_WIDTH, M_WIDTH, M_WIDTH, M_HEADS, M_HEADS, A_WIDTH, A_WIDTH, A_WIDTH, D_MODEL, D_MODEL)
    cuts = [0]
    for s in sizes:
        cuts.append(cuts[-1] + s)
    mq, mk, mv, mo, mi, mf, aq, ak, av, ga, gb = [w_in[:, cuts[i]:cuts[i + 1]] for i in range(len(sizes))]
    wn = jnp.concatenate([ga, gb, mq, mk, mv, mo, ak], axis=1).astype(BF16)
    wt = jnp.concatenate([aq, av], axis=1).T.astype(BF16)
    wgate = jnp.zeros((D_MODEL, LANES), F32).at[:, :M_HEADS].set(mi).at[:, M_HEADS:2 * M_HEADS].set(mf).astype(BF16)
    gate_b = jnp.zeros((1, LANES), F32).at[0, :M_HEADS].set(igate_b).at[0, M_HEADS:2 * M_HEADS].set(fgate_b)
    z, zt, gates = _in_proj(x2, norm_g.reshape(1, D_MODEL), wn, wt, wgate)
    ha = _mlstm(z, gates, conv_w, conv_b.reshape(1, -1), gate_b, mlstm_norm_g.reshape(1, -1), batch, seq)
    qg_col = jnp.tile(q_norm_g, 2).reshape(2 * A_HEAD_DIM, 1)
    kg_row = jnp.tile(k_norm_g, 2).reshape(1, 2 * A_HEAD_DIM)
    ob = _moba(z, zt, bias_tiles, _moba_logit_bound(rel_bias, q_norm_g, k_norm_g), qg_col, kg_row, batch, seq)
    return _merge(x2, ha, ob, z, w_branch_a.astype(BF16), w_branch_b.astype(BF16), w_out.astype(BF16))


def kernel(x, rel_bias, mix_norm_g, w_in, conv_w, conv_b, igate_b, fgate_b, mlstm_norm_g, q_norm_g, k_norm_g,
           w_branch_a, w_branch_b, w_out, ffn_norm_g, dense_w_gate, dense_w_up, dense_w_down, router_w,
           expert_w_gate, expert_w_up, expert_w_down):
    batch, seq, d = x.shape
    depth = w_in.shape[0]
    assert d == D_MODEL and seq % MLSTM_T == 0 and seq % MOBA_BLOCK == 0
    assert (batch * seq) % MOE_TM == 0
    x2 = x.reshape(batch * seq, d)
    bias_tiles = _moba_bias_tiles(rel_bias)
    for layer in range(depth):
        x2 = _mixer(x2, batch, seq, rel_bias, bias_tiles, mix_norm_g[layer], w_in[layer], conv_w[layer], conv_b[layer],
                    igate_b[layer], fgate_b[layer], mlstm_norm_g[layer], q_norm_g[layer], k_norm_g[layer],
                    w_branch_a[layer], w_branch_b[layer], w_out[layer])
        g = ffn_norm_g[layer].reshape(1, d)
        j = layer // 2
        if layer % 2 == 0:
            x2 = _ffn(x2, g, dense_w_gate[j].astype(BF16), dense_w_up[j].astype(BF16), dense_w_down[j].astype(BF16))
        else:
            x2 = _moe(x2, g, router_w[j], expert_w_gate[j].astype(BF16), expert_w_up[j].astype(BF16),
                      expert_w_down[j].astype(BF16))
    return x2.reshape(batch, seq, d)
```

```python
import functools
import math

import jax
import jax.numpy as jnp
from jax import lax
from jax.experimental import pallas as pl
from jax.experimental.pallas import tpu as pltpu

F32 = jnp.float32
BF16 = jnp.bfloat16
HIGHEST = lax.Precision.HIGHEST

D_MODEL = 1024
M_HEADS = 4
M_HEAD_DIM = 128
M_WIDTH = M_HEADS * M_HEAD_DIM
CONV_WIDTH = 4
A_HEADS = 8
A_HEAD_DIM = 64
A_WIDTH = A_HEADS * A_HEAD_DIM
MOBA_BLOCK = 256
MOBA_TOPK = 3
REL_BUCKETS = 32
REL_MAX_DIST = 1024
N_EXPERTS = 8
EPS = 1e-6

LANES = 128
SUBLANES = 8
NEG = -1e30
VMEM_LIMIT = 56 * 1024 * 1024

C_GA = 0
C_GB = D_MODEL
C_MQ = 2 * D_MODEL
C_MK = C_MQ + M_WIDTH
C_MV = C_MK + M_WIDTH
C_MO = C_MV + M_WIDTH
C_AK = C_MO + M_WIDTH
NAT_WIDTH = C_AK + A_WIDTH
N_BIAS_TILES = 6

PROJ_TM = 512
MLSTM_T = 512
MLSTM_CHUNK = 128
MERGE_TM = 512
FFN_TM = 512
FFN_CHUNK = 256
ROUTER_TM = 512
MOE_TM = 512
MOE_TF = 1792
SCATTER_R = 256


def _params(*sem):
    return pltpu.CompilerParams(dimension_semantics=sem, vmem_limit_bytes=VMEM_LIMIT)


def _const_spec(shape):
    nd = len(shape)
    return pl.BlockSpec(shape, lambda *_: (0,) * nd)


def _rms(x, g):
    return x * lax.rsqrt(jnp.mean(x * x, axis=-1, keepdims=True) + EPS) * g


def _in_proj_kernel(x_ref, g_ref, wn_ref, wt_ref, wg_ref, z_ref, zt_ref, gt_ref):
    hn = _rms(x_ref[...], g_ref[...]).astype(BF16)
    for c in range(0, NAT_WIDTH, 512):
        z_ref[:, c:c + 512] = jnp.dot(hn, wn_ref[:, c:c + 512], preferred_element_type=F32).astype(BF16)
    for c in range(0, 2 * A_WIDTH, 256):
        zt_ref[c:c + 256, :] = lax.dot_general(
            wt_ref[c:c + 256, :], hn, (((1,), (1,)), ((), ())), preferred_element_type=F32).astype(BF16)
    gt_ref[...] = jnp.dot(hn, wg_ref[...], preferred_element_type=F32)


def _in_proj(x2, g, wn, wt, wg):
    n = x2.shape[0]
    tm = PROJ_TM
    return pl.pallas_call(
        _in_proj_kernel,
        grid=(n // tm,),
        in_specs=[
            pl.BlockSpec((tm, D_MODEL), lambda i: (i, 0)),
            _const_spec((1, D_MODEL)),
            _const_spec((D_MODEL, NAT_WIDTH)),
            _const_spec((2 * A_WIDTH, D_MODEL)),
            _const_spec((D_MODEL, LANES)),
        ],
        out_specs=[
            pl.BlockSpec((tm, NAT_WIDTH), lambda i: (i, 0)),
            pl.BlockSpec((2 * A_WIDTH, tm), lambda i: (0, i)),
            pl.BlockSpec((tm, LANES), lambda i: (i, 0)),
        ],
        out_shape=[
            jax.ShapeDtypeStruct((n, NAT_WIDTH), BF16),
            jax.ShapeDtypeStruct((2 * A_WIDTH, n), BF16),
            jax.ShapeDtypeStruct((n, LANES), F32),
        ],
        compiler_params=_params("arbitrary"),
        name="in_proj",
    )(x2, g, wn, wt, wg)


def _mlstm_kernel(zq_ref, zk_ref, zv_ref, zo_ref, gt_ref, cw_ref, cb_ref, gb_ref, ng_ref, o_ref,
                  qk_buf, q_s, k_s, c_s, m_s, *, t_blk, chunk):
    L = chunk
    DH = M_HEAD_DIM
    assert L == LANES and DH == LANES

    @pl.when(pl.program_id(1) == 0)
    def _init():
        qk_buf[0:SUBLANES, :] = jnp.zeros((SUBLANES, 2 * M_WIDTH), F32)
        c_s[...] = jnp.zeros_like(c_s)
        m_s[...] = jnp.zeros_like(m_s)

    qk_buf[SUBLANES:SUBLANES + t_blk, 0:M_WIDTH] = zq_ref[...].astype(F32)
    qk_buf[SUBLANES:SUBLANES + t_blk, M_WIDTH:] = zk_ref[...].astype(F32)
    acc = cb_ref[...] + cw_ref[CONV_WIDTH - 1:CONV_WIDTH, :] * qk_buf[SUBLANES:SUBLANES + t_blk, :]
    for j in range(CONV_WIDTH - 1):
        off = SUBLANES - (CONV_WIDTH - 1) + j
        acc = acc + cw_ref[j:j + 1, :] * qk_buf[off:off + t_blk, :]
    qk = acc * jax.nn.sigmoid(acc)
    qk_buf[0:SUBLANES, :] = qk_buf[t_blk:t_blk + SUBLANES, :]
    q_s[...] = qk[:, :M_WIDTH].astype(BF16)
    k_s[...] = qk[:, M_WIDTH:] * (DH ** -0.5)

    row = lax.broadcasted_iota(jnp.int32, (L, L), 0)
    col = lax.broadcasted_iota(jnp.int32, (L, L), 1)
    causal = col <= row
    tri = causal.astype(F32)
    lane = lax.broadcasted_iota(jnp.int32, (L, LANES), 1)
    ones_blk = jnp.ones((L, LANES), BF16)

    def chunk_body(c, carry):
        r0 = c * L
        g_pre = gt_ref[pl.ds(r0, L), :] + gb_ref[...]
        log_f = jnp.minimum(g_pre, 0.0) - jnp.log1p(jnp.exp(-jnp.abs(g_pre)))
        bcum = jnp.dot(tri, log_f, precision=HIGHEST, preferred_element_type=F32)
        gm = jnp.where(lane < M_HEADS, g_pre, bcum)
        gm_t = gm.T
        c_all = [c_s[h] for h in range(M_HEADS)]
        m_all = m_s[...]
        heads = range(M_HEADS)
        hs = [slice(h * DH, (h + 1) * DH) for h in heads]
        qh = [q_s[pl.ds(r0, L), hs[h]] for h in heads]
        kf = [k_s[pl.ds(r0, L), hs[h]] for h in heads]
        v_ext = [jnp.concatenate([zv_ref[pl.ds(r0, L), hs[h]], ones_blk], axis=1) for h in heads]
        qk = [lax.dot_general(qh[h], kf[h].astype(BF16), (((1,), (1,)), ((), ())), preferred_element_type=F32)
              for h in heads]
        inter = [jnp.dot(qh[h], c_all[h].astype(BF16), preferred_element_type=F32) for h in heads]
        mt, w_state, s_bf, kw_t, decay, m_new = [], [], [], [], [], []
        for h in heads:
            bb = jnp.broadcast_to(gm[:, M_HEADS + h:M_HEADS + h + 1], (L, LANES))
            ii = jnp.broadcast_to(gm[:, h:h + 1], (L, LANES))
            bb_row = gm_t[M_HEADS + h:M_HEADS + h + 1, :]
            ii_row = gm_t[h:h + 1, :]
            m_old = m_all[h:h + 1, :]
            dlog = jnp.where(causal, bb - (bb_row - ii_row), -jnp.inf)
            a = bb + m_old
            mt.append(jnp.maximum(a, jnp.broadcast_to(jnp.max(dlog, axis=-1, keepdims=True), (L, LANES))))
            w_state.append(jnp.exp(a - mt[h]))
            s_bf.append((qk[h] * jnp.exp(dlog - mt[h])).astype(BF16))
            b_last = bb[L - 1:L, :]
            g = b_last - bb + ii
            m_new.append(jnp.maximum(b_last + m_old, jnp.max(g, axis=0, keepdims=True)))
            decay.append(jnp.exp(b_last + m_old - m_new[h]))
            kw_t.append((kf[h] * jnp.exp(g - m_new[h])).T.astype(BF16))
        intra = [jnp.dot(s_bf[h], v_ext[h], preferred_element_type=F32) for h in heads]
        upd = [jnp.dot(kw_t[h], v_ext[h], preferred_element_type=F32) for h in heads]
        for h in heads:
            num = intra[h][:, :DH] + w_state[h] * inter[h][:, :DH]
            den = intra[h][:, DH:] + w_state[h] * inter[h][:, DH:]
            h_t = num / jnp.maximum(jnp.abs(den), jnp.exp(-mt[h]))
            hc = jax.nn.sigmoid(zo_ref[pl.ds(r0, L), hs[h]].astype(F32)) * h_t
            o_ref[pl.ds(r0, L), hs[h]] = _rms(hc, ng_ref[:, hs[h]]).astype(BF16)
        for h in heads:
            c_s[h] = jnp.concatenate([decay[h], decay[h]], axis=1) * c_all[h] + upd[h]
            m_s[h:h + 1, :] = m_new[h]
        return carry

    for c in range(t_blk // L):
        chunk_body(c, 0)


def _mlstm(z, gates, conv_w, conv_b, gate_b, norm_g, batch, seq):
    n = batch * seq
    t = MLSTM_T
    nt = seq // t
    row_blk = lambda b, s: b * nt + s
    zspec = lambda cb: pl.BlockSpec((t, M_WIDTH), lambda b, s: (row_blk(b, s), cb))
    return pl.pallas_call(
        functools.partial(_mlstm_kernel, t_blk=t, chunk=MLSTM_CHUNK),
        grid=(batch, nt),
        in_specs=[
            zspec(C_MQ // M_WIDTH), zspec(C_MK // M_WIDTH), zspec(C_MV // M_WIDTH), zspec(C_MO // M_WIDTH),
            pl.BlockSpec((t, LANES), lambda b, s: (row_blk(b, s), 0)),
            _const_spec((CONV_WIDTH, 2 * M_WIDTH)),
            _const_spec((1, 2 * M_WIDTH)),
            _const_spec((1, LANES)),
            _const_spec((1, M_WIDTH)),
        ],
        out_specs=pl.BlockSpec((t, M_WIDTH), lambda b, s: (row_blk(b, s), 0)),
        out_shape=jax.ShapeDtypeStruct((n, M_WIDTH), BF16),
        scratch_shapes=[
            pltpu.VMEM((t + SUBLANES, 2 * M_WIDTH), F32),
            pltpu.VMEM((t, M_WIDTH), BF16),
            pltpu.VMEM((t, M_WIDTH), F32),
            pltpu.VMEM((M_HEADS, M_HEAD_DIM, M_HEAD_DIM + LANES), F32),
            pltpu.VMEM((SUBLANES, LANES), F32),
        ],
        compiler_params=_params("arbitrary", "arbitrary"),
        name="mlstm",
    )(z, z, z, z, gates, conv_w, conv_b, gate_b, norm_g)


MOBA_V_ROWS = A_HEAD_DIM + 16
MOBA_ROWB_ROWS = 24
MOBA_GROUP = 34
LOG2E = math.log2(math.e)
FAST_SOFTMAX_MIN_DENOM = 1e-25


def _moba_items(n_blocks):
    items = []
    for qb in range(n_blocks):
        items.append((qb, qb, 0, 0))
        for j in range(qb):
            items.append((qb, j, min(qb - j, N_BIAS_TILES - 1), j + 1))
    n_groups = -(-len(items) // MOBA_GROUP)
    noop = (n_blocks - 1, 0, 0, MOBA_ROWB_ROWS - 1)
    items += [noop] * (n_groups * MOBA_GROUP + 2 - len(items))
    return n_groups, [jnp.asarray([it[c] for it in items], jnp.int32) for c in range(4)]


def _moba_kernel(it_q, it_blk, it_tile, it_row, qt_ref, k_ref, vt_ref, bias_ref, bound_ref, qg_ref, kg_ref, o_ref,
                 kn_s, vt_s, kmean_s, qh_s, rowb_s, acc_s, st_a, st_b, p_a, p_b, *, n_blocks, n_groups):
    BS = MOBA_BLOCK
    DA = A_HEAD_DIM
    seq = n_blocks * BS
    lane_k = lax.broadcasted_iota(jnp.int32, (BS, LANES), 1)
    head0_k = lane_k < DA

    ones_row = jnp.where(lax.broadcasted_iota(jnp.int32, (16, BS), 0) == 0, 1.0, 0.0).astype(BF16)
    for j in range(n_blocks):
        kf = k_ref[j * BS:(j + 1) * BS, :].astype(F32)
        k2 = kf * kf
        s0 = jnp.sum(jnp.where(head0_k, k2, 0.0), axis=-1, keepdims=True)
        s1 = jnp.sum(jnp.where(head0_k, 0.0, k2), axis=-1, keepdims=True)
        inv = jnp.where(head0_k, lax.rsqrt(s0 / DA + EPS), lax.rsqrt(s1 / DA + EPS))
        kn = kf * inv * kg_ref[...]
        kn_s[j * BS:(j + 1) * BS, :] = kn.astype(BF16)
        kmean_s[j:j + 1, :] = jnp.mean(kn, axis=0, keepdims=True)
        for h in range(2):
            vt_s[j, h, 0:DA, :] = vt_ref[h * DA:(h + 1) * DA, j * BS:(j + 1) * BS]
            vt_s[j, h, DA:MOBA_V_ROWS, :] = ones_row

    qf = qt_ref[...].astype(F32)
    sub_q = lax.broadcasted_iota(jnp.int32, qf.shape, 0)
    head0_q = sub_q < DA
    q2 = qf * qf
    ss0 = jnp.sum(jnp.where(head0_q, q2, 0.0), axis=0, keepdims=True)
    ss1 = jnp.sum(jnp.where(head0_q, 0.0, q2), axis=0, keepdims=True)
    qn = qf * jnp.where(head0_q, lax.rsqrt(ss0 / DA + EPS), lax.rsqrt(ss1 / DA + EPS)) * qg_ref[...]

    blk = lax.broadcasted_iota(jnp.int32, (n_blocks, seq), 0)
    past = blk < lax.broadcasted_iota(jnp.int32, (n_blocks, seq), 1) // BS
    lane_m = lax.broadcasted_iota(jnp.int32, (n_blocks, LANES), 1)
    kmean = kmean_s[...]
    kmean2 = jnp.concatenate([jnp.where(lane_m < DA, kmean, 0.0), jnp.where(lane_m < DA, 0.0, kmean)], axis=0)
    gates = jnp.dot(kmean2, qn, precision=HIGHEST, preferred_element_type=F32)
    for h in range(2):
        gate = jnp.where(past, gates[h * n_blocks:(h + 1) * n_blocks, :], -jnp.inf)
        rank = jnp.zeros((n_blocks, seq), jnp.int32)
        for j2 in range(n_blocks):
            other = gate[j2:j2 + 1, :]
            beats = (other > gate) | ((other == gate) & (j2 < blk))
            rank = rank + beats.astype(jnp.int32)
        bound = bound_ref[h, :, 0:1]
        selb = jnp.where(past & (rank < MOBA_TOPK), 0.0, NEG) - bound
        hmask_q = head0_q if h == 0 else jnp.logical_not(head0_q)
        qh = jnp.where(hmask_q, qn * (DA ** -0.5 * LOG2E), 0.0).astype(BF16)
        for qb in range(n_blocks):
            cols = slice(qb * BS, (qb + 1) * BS)
            qh_s[qb, h] = qh[:, cols]
            rowb_s[qb, h, 0:1, :] = jnp.broadcast_to(-bound, (1, BS))
            rowb_s[qb, h, 1:n_blocks + 1, :] = selb[:, cols]
            rowb_s[qb, h, n_blocks + 1:, :] = jnp.full((MOBA_ROWB_ROWS - n_blocks - 1, BS), NEG, F32)
    acc_s[...] = jnp.zeros_like(acc_s)

    def scores(h, qb, j, tile, row):
        kj = kn_s[pl.ds(pl.multiple_of(j * BS, BS), BS), :]
        st = jnp.dot(kj, qh_s[qb, h], preferred_element_type=F32)
        return st + bias_ref[h, tile] + rowb_s[qb, h, pl.ds(row, 1), :]

    def pv(h, j, p):
        return jnp.dot(vt_s[j, h], p, preferred_element_type=F32)

    def stage_scores(i, st_ref):
        for h in range(2):
            st_ref[h] = scores(h, it_q[i], it_blk[i], it_tile[i], it_row[i])

    def stage_exp(st_ref, p_ref):
        for h in range(2):
            p_ref[h] = jnp.exp2(st_ref[h]).astype(BF16)

    def stage_pv(i, p_ref):
        for h in range(2):
            acc_s[it_q[i], h] += pv(h, it_blk[i], p_ref[h])

    def group(m, carry):
        for u in range(0, MOBA_GROUP, 2):
            i = MOBA_GROUP * m + u
            stage_pv(i, p_a)
            stage_scores(i + 2, st_a)
            stage_exp(st_b, p_b)
            stage_pv(i + 1, p_b)
            stage_scores(i + 3, st_b)
            stage_exp(st_a, p_a)
        return carry

    stage_scores(0, st_a)
    stage_exp(st_a, p_a)
    stage_scores(1, st_b)
    lax.fori_loop(0, n_groups, group, 0)

    def finish(qb, l_min):
        outs = []
        for h in range(2):
            acc = acc_s[qb, h]
            l = acc[DA:DA + 1, :]
            outs.append(acc[0:DA, :] / l)
            l_min = jnp.minimum(l_min, jnp.min(l))
        o_ref[pl.ds(pl.multiple_of(qb * BS, BS), BS), :] = jnp.concatenate(outs, axis=0).T.astype(BF16)
        return l_min

    l_min = lax.fori_loop(0, n_blocks, finish, jnp.float32(jnp.inf))

    def online(qb, carry):
        state = []
        for h in range(2):
            st = scores(h, qb, qb, 0, 0)
            m = jnp.max(st, axis=0, keepdims=True)
            full = pv(h, qb, jnp.exp2(st - m).astype(BF16))
            state += [m, full]

        def past_body(j, state):
            tile = jnp.minimum(qb - j, N_BIAS_TILES - 1)
            new = []
            for h in range(2):
                m_old, full_old = state[2 * h:2 * h + 2]
                st = scores(h, qb, j, tile, j + 1)
                m_new = jnp.maximum(m_old, jnp.max(st, axis=0, keepdims=True))
                full = pv(h, j, jnp.exp2(st - m_new).astype(BF16))
                new += [m_new, jnp.exp2(m_old - m_new) * full_old + full]
            return tuple(new)

        state = lax.fori_loop(0, qb, past_body, tuple(state))
        outs = [state[2 * h + 1][0:DA, :] / state[2 * h + 1][DA:DA + 1, :] for h in range(2)]
        o_ref[pl.ds(pl.multiple_of(qb * BS, BS), BS), :] = jnp.concatenate(outs, axis=0).T.astype(BF16)
        return carry

    @pl.when(l_min < FAST_SOFTMAX_MIN_DENOM)
    def _redo():
        lax.fori_loop(0, n_blocks, online, 0)


def _moba_logit_bound(rel_bias, q_norm_g, k_norm_g):
    qk = A_HEAD_DIM * jnp.max(jnp.abs(q_norm_g)) * jnp.max(jnp.abs(k_norm_g)) * (A_HEAD_DIM ** -0.5) * 1.02
    b = (qk + jnp.max(rel_bias, axis=0)) * LOG2E
    return jnp.broadcast_to(b[:, None, None], (A_HEADS, 1, LANES)).astype(F32)


def _moba(z, zt, bias_tiles, bound, qg_col, kg_row, batch, seq):
    n = batch * seq
    nb = seq // MOBA_BLOCK
    bs = MOBA_BLOCK
    hp = A_HEADS // 2
    w2 = 2 * A_HEAD_DIM
    assert nb + 2 <= MOBA_ROWB_ROWS
    n_groups, items = _moba_items(nb)
    grid_spec = pltpu.PrefetchScalarGridSpec(
        num_scalar_prefetch=len(items),
        grid=(batch, hp),
        in_specs=[
            pl.BlockSpec((w2, seq), lambda b, p, *_: (p, b)),
            pl.BlockSpec((seq, w2), lambda b, p, *_: (b, C_AK // w2 + p)),
            pl.BlockSpec((w2, seq), lambda b, p, *_: (hp + p, b)),
            pl.BlockSpec((2, N_BIAS_TILES, bs, bs), lambda b, p, *_: (p, 0, 0, 0)),
            pl.BlockSpec((2, 1, LANES), lambda b, p, *_: (p, 0, 0)),
            pl.BlockSpec((w2, 1), lambda b, p, *_: (0, 0)),
            pl.BlockSpec((1, w2), lambda b, p, *_: (0, 0)),
        ],
        out_specs=pl.BlockSpec((seq, w2), lambda b, p, *_: (b, p)),
        scratch_shapes=[
            pltpu.VMEM((seq, w2), BF16),
            pltpu.VMEM((nb, 2, MOBA_V_ROWS, bs), BF16),
            pltpu.VMEM((nb, w2), F32),
            pltpu.VMEM((nb, 2, w2, bs), BF16),
            pltpu.VMEM((nb, 2, MOBA_ROWB_ROWS, bs), F32),
            pltpu.VMEM((nb, 2, MOBA_V_ROWS, bs), F32),
            pltpu.VMEM((2, bs, bs), F32),
            pltpu.VMEM((2, bs, bs), F32),
            pltpu.VMEM((2, bs, bs), BF16),
            pltpu.VMEM((2, bs, bs), BF16),
        ],
    )
    return pl.pallas_call(
        functools.partial(_moba_kernel, n_blocks=nb, n_groups=n_groups),
        grid_spec=grid_spec,
        out_shape=jax.ShapeDtypeStruct((n, A_WIDTH), BF16),
        compiler_params=_params("arbitrary", "arbitrary"),
        name="moba",
    )(*items, zt, z, zt, bias_tiles, bound, qg_col, kg_row)


def _t5_bucket(dist):
    n = jnp.maximum(dist, 0)
    max_exact = REL_BUCKETS // 2
    log_ratio = jnp.log(jnp.maximum(n, max_exact).astype(F32) / max_exact) / math.log(REL_MAX_DIST / max_exact)
    large = max_exact + (log_ratio * (REL_BUCKETS - max_exact)).astype(jnp.int32)
    large = jnp.minimum(large, REL_BUCKETS - 1)
    return jnp.where(n < max_exact, n, large)


def _bias_tiles_kernel(rb_ref, bucket_ref, o_ref):
    bucket = bucket_ref[0]
    hit = [bucket == b for b in range(REL_BUCKETS)]
    tk = lax.broadcasted_iota(jnp.int32, bucket.shape, 0)
    tq = lax.broadcasted_iota(jnp.int32, bucket.shape, 1)
    masked = (tk > tq) & (pl.program_id(0) == 0)
    for h in range(A_HEADS):
        acc = jnp.zeros(bucket.shape, F32)
        for b in range(REL_BUCKETS):
            acc = jnp.where(hit[b], rb_ref[b, h], acc)
        o_ref[h, 0] = jnp.where(masked, NEG, acc * LOG2E)


def _moba_bias_tiles(rel_bias):
    assert (N_BIAS_TILES - 1) * MOBA_BLOCK - (MOBA_BLOCK - 1) >= REL_MAX_DIST
    tk = jnp.arange(MOBA_BLOCK)[None, :, None]
    tq = jnp.arange(MOBA_BLOCK)[None, None, :]
    diff = jnp.arange(N_BIAS_TILES)[:, None, None]
    bucket = _t5_bucket(diff * MOBA_BLOCK + tq - tk).astype(jnp.int32)
    return pl.pallas_call(
        _bias_tiles_kernel,
        grid=(N_BIAS_TILES,),
        in_specs=[
            pl.BlockSpec(memory_space=pltpu.SMEM),
            pl.BlockSpec((1, MOBA_BLOCK, MOBA_BLOCK), lambda t: (t, 0, 0)),
        ],
        out_specs=pl.BlockSpec((A_HEADS, 1, MOBA_BLOCK, MOBA_BLOCK), lambda t: (0, t, 0, 0)),
        out_shape=jax.ShapeDtypeStruct((A_HEADS, N_BIAS_TILES, MOBA_BLOCK, MOBA_BLOCK), F32),
        compiler_params=_params("arbitrary"),
        name="moba_bias_tiles",
    )(rel_bias.astype(F32), bucket)


def _merge_kernel(x_ref, ha_ref, ob_ref, ga_ref, gb_ref, wa_ref, wb_ref, wo_ref, o_ref):
    ya = jnp.dot(ha_ref[...], wa_ref[...], preferred_element_type=F32)
    yb = jnp.dot(ob_ref[...], wb_ref[...], preferred_element_type=F32)
    y = jax.nn.sigmoid(ga_ref[...].astype(F32)) * ya + jax.nn.sigmoid(gb_ref[...].astype(F32)) * yb
    o_ref[...] = x_ref[...] + jnp.dot(y.astype(BF16), wo_ref[...], preferred_element_type=F32)


def _merge(x2, ha, ob, z, wa, wb, wo):
    n = x2.shape[0]
    tm = MERGE_TM
    return pl.pallas_call(
        _merge_kernel,
        grid=(n // tm,),
        in_specs=[
            pl.BlockSpec((tm, D_MODEL), lambda i: (i, 0)),
            pl.BlockSpec((tm, M_WIDTH), lambda i: (i, 0)),
            pl.BlockSpec((tm, A_WIDTH), lambda i: (i, 0)),
            pl.BlockSpec((tm, D_MODEL), lambda i: (i, C_GA // D_MODEL)),
            pl.BlockSpec((tm, D_MODEL), lambda i: (i, C_GB // D_MODEL)),
            _const_spec((M_WIDTH, D_MODEL)),
            _const_spec((A_WIDTH, D_MODEL)),
            _const_spec((D_MODEL, D_MODEL)),
        ],
        out_specs=pl.BlockSpec((tm, D_MODEL), lambda i: (i, 0)),
        out_shape=jax.ShapeDtypeStruct((n, D_MODEL), F32),
        compiler_params=_params("arbitrary"),
        name="merge",
    )(x2, ha, ob, z, z, wa, wb, wo)


def _ffn_kernel(x_ref, g_ref, wg_ref, wu_ref, wd_ref, o_ref, *, d_ff):
    x = x_ref[...]
    hn = _rms(x, g_ref[...]).astype(BF16)
    acc = x
    for c in range(0, d_ff, FFN_CHUNK):
        g = jnp.dot(hn, wg_ref[:, c:c + FFN_CHUNK], preferred_element_type=F32)
        u = jnp.dot(hn, wu_ref[:, c:c + FFN_CHUNK], preferred_element_type=F32)
        a = (g * jax.nn.sigmoid(g) * u).astype(BF16)
        acc = acc + jnp.dot(a, wd_ref[c:c + FFN_CHUNK, :], preferred_element_type=F32)
    o_ref[...] = acc


def _ffn(x2, g, wg, wu, wd):
    n = x2.shape[0]
    d_ff = wg.shape[1]
    assert d_ff % FFN_CHUNK == 0
    tm = FFN_TM
    return pl.pallas_call(
        functools.partial(_ffn_kernel, d_ff=d_ff),
        grid=(n // tm,),
        in_specs=[
            pl.BlockSpec((tm, D_MODEL), lambda i: (i, 0)),
            _const_spec((1, D_MODEL)),
            _const_spec((D_MODEL, d_ff)),
            _const_spec((D_MODEL, d_ff)),
            _const_spec((d_ff, D_MODEL)),
        ],
        out_specs=pl.BlockSpec((tm, D_MODEL), lambda i: (i, 0)),
        out_shape=jax.ShapeDtypeStruct((n, D_MODEL), F32),
        compiler_params=_params("arbitrary"),
        name="ffn_dense",
    )(x2, g, wg, wu, wd)


META_W0, META_W1, META_E0, META_E1, META_R0, META_R1 = range(6)


def _router_kernel(x_ref, g_ref, rw_ref, h_ref, meta_ref, cnt_ref, carry_s):
    @pl.when(pl.program_id(0) == 0)
    def _init():
        carry_s[...] = jnp.zeros_like(carry_s)

    h = _rms(x_ref[...], g_ref[...])
    h_ref[...] = h
    tm = h.shape[0]
    logits = jnp.dot(h, rw_ref[...], precision=HIGHEST, preferred_element_type=F32)
    lane = lax.broadcasted_iota(jnp.int32, (tm, LANES), 1)
    lg = jnp.where(lane < N_EXPERTS, logits, -jnp.inf)
    m1 = jnp.max(lg, axis=-1, keepdims=True)
    i1 = jnp.min(jnp.where(lg == m1, lane, LANES), axis=-1, keepdims=True)
    lg2 = jnp.where(lane == i1, -jnp.inf, lg)
    m2 = jnp.max(lg2, axis=-1, keepdims=True)
    i2 = jnp.min(jnp.where(lg2 == m2, lane, LANES), axis=-1, keepdims=True)
    e = jnp.exp(m2 - m1)
    w1 = 1.0 / (1.0 + e)
    w2 = e / (1.0 + e)
    hit1 = lane == i1
    hit2 = lane == i2
    onehot = jnp.where(hit1 | hit2, 1.0, 0.0)
    r = lax.broadcasted_iota(jnp.int32, (tm, tm), 0)
    c = lax.broadcasted_iota(jnp.int32, (tm, tm), 1)
    before = jnp.where(c < r, 1.0, 0.0).astype(BF16)
    carry = carry_s[0:1, :]
    pref = jnp.dot(before, onehot.astype(BF16), preferred_element_type=F32) + carry
    r1 = jnp.sum(jnp.where(hit1, pref, 0.0), axis=-1, keepdims=True)
    r2 = jnp.sum(jnp.where(hit2, pref, 0.0), axis=-1, keepdims=True)
    new_carry = carry + jnp.sum(onehot, axis=0, keepdims=True)
    carry_s[...] = jnp.broadcast_to(new_carry, carry_s.shape)
    cnt_ref[...] = jnp.broadcast_to(new_carry, cnt_ref.shape)
    meta = jnp.zeros((tm, LANES), F32)
    for idx, val in ((META_W0, w1), (META_W1, w2), (META_E0, i1.astype(F32)), (META_E1, i2.astype(F32)),
                     (META_R0, r1), (META_R1, r2)):
        meta = jnp.where(lane == idx, val, meta)
    meta_ref[...] = meta


def _router(x2, g, rw):
    n = x2.shape[0]
    tm = ROUTER_TM
    return pl.pallas_call(
        _router_kernel,
        grid=(n // tm,),
        in_specs=[
            pl.BlockSpec((tm, D_MODEL), lambda i: (i, 0)),
            _const_spec((1, D_MODEL)),
            _const_spec((D_MODEL, LANES)),
        ],
        out_specs=[
            pl.BlockSpec((tm, D_MODEL), lambda i: (i, 0)),
            pl.BlockSpec((tm, LANES), lambda i: (i, 0)),
            _const_spec((SUBLANES, LANES)),
        ],
        out_shape=[
            jax.ShapeDtypeStruct((n, D_MODEL), F32),
            jax.ShapeDtypeStruct((n, LANES), F32),
            jax.ShapeDtypeStruct((SUBLANES, LANES), F32),
        ],
        scratch_shapes=[pltpu.VMEM((SUBLANES, LANES), F32)],
        compiler_params=_params("arbitrary"),
        name="moe_router",
    )(x2, g, rw)


def _row_copy(src_ref, src_row, dst_ref, dst_row, sem):
    return pltpu.make_async_copy(src_ref.at[pl.ds(src_row, 1)], dst_ref.at[pl.ds(dst_row, 1)], sem)


ROW_TOKENS_UNROLL = 8


def _row_tokens_kernel(trips_ref, pos0_ref, pos1_ref, tok_ref):
    def clear(c, carry):
        for u in range(ROW_TOKENS_UNROLL):
            tok_ref[c * ROW_TOKENS_UNROLL + u] = 0
        return carry

    lax.fori_loop(0, trips_ref[0], clear, 0)

    def put(c, carry):
        for u in range(ROW_TOKENS_UNROLL):
            t = c * ROW_TOKENS_UNROLL + u
            tok_ref[pos0_ref[t]] = t
            tok_ref[pos1_ref[t]] = t
        return carry

    lax.fori_loop(0, trips_ref[1], put, 0)


def _row_tokens(pos0, pos1, n_rows):
    n_tokens = pos0.shape[0]
    assert n_rows % ROW_TOKENS_UNROLL == 0 and n_tokens % ROW_TOKENS_UNROLL == 0
    trips = jnp.asarray([n_rows // ROW_TOKENS_UNROLL, n_tokens // ROW_TOKENS_UNROLL], jnp.int32)
    smem = pl.BlockSpec(memory_space=pltpu.SMEM)
    return pl.pallas_call(
        _row_tokens_kernel,
        in_specs=[smem, smem, smem],
        out_specs=smem,
        out_shape=jax.ShapeDtypeStruct((n_rows,), jnp.int32),
        name="moe_row_tokens",
    )(trips, pos0, pos1)


def _moe_kernel(exp_ref, valid_ref, tok_ref, h_ref, wg_ref, wu_ref, wd_ref, y_ref, xg_s, xb_s, acc_s, sem,
                *, n_ff):
    del exp_ref
    i = pl.program_id(0)
    f = pl.program_id(1)
    tm = xb_s.shape[0]
    slot = i % 2

    def start_gather(tile, sl, first_row, count):
        for r in range(count):
            row = first_row + r
            _row_copy(h_ref, tok_ref[tile * tm + row], xg_s.at[sl], row, sem.at[sl]).start(priority=r % 2)

    def wait_gather(sl):
        pltpu.make_async_copy(h_ref.at[pl.ds(0, tm)], xg_s.at[sl], sem.at[sl]).wait()

    @pl.when((i == 0) & (f == 0))
    def _first_tile():
        start_gather(0, 0, 0, tm)

    @pl.when(f == 0)
    def _arrive():
        wait_gather(slot)
        start_gather(i + 1, 1 - slot, 0, tm)

    @pl.when((valid_ref[i] == 0) & (f == n_ff - 1))
    def _idle():
        y_ref[...] = jnp.zeros_like(y_ref)

    @pl.when(valid_ref[i] == 1)
    def _tile():
        @pl.when(f == 0)
        def _cast():
            xb_s[...] = xg_s[slot].astype(BF16)

        xb = xb_s[...]
        d = None
        for c in range(0, wg_ref.shape[2], FFN_CHUNK):
            g = jnp.dot(xb, wg_ref[0, :, c:c + FFN_CHUNK], preferred_element_type=F32)
            u = jnp.dot(xb, wu_ref[0, :, c:c + FFN_CHUNK], preferred_element_type=F32)
            a = (g * jax.nn.sigmoid(g) * u).astype(BF16)
            dc = jnp.dot(a, wd_ref[0, c:c + FFN_CHUNK, :], preferred_element_type=F32)
            d = dc if d is None else d + dc

        @pl.when(f == 0)
        def _first():
            acc_s[...] = d

        @pl.when(f > 0)
        def _rest():
            acc_s[...] += d

        @pl.when(f == n_ff - 1)
        def _out():
            y_ref[...] = acc_s[...]

    @pl.when((i == pl.num_programs(0) - 1) & (f == n_ff - 1))
    def _drain():
        wait_gather(1 - slot)


def _moe_experts(tile_exp, tile_valid, row_tok, h, wg, wu, wd):
    d_ff = wg.shape[2]
    tm, tf = MOE_TM, MOE_TF
    n_tiles = tile_exp.shape[0]
    n_ff = d_ff // tf
    assert d_ff % tf == 0 and tm % n_ff == 0 and row_tok.shape[0] == (n_tiles + 1) * tm
    grid_spec = pltpu.PrefetchScalarGridSpec(
        num_scalar_prefetch=3,
        grid=(n_tiles, n_ff),
        in_specs=[
            pl.BlockSpec(memory_space=pl.ANY),
            pl.BlockSpec((1, D_MODEL, tf), lambda i, f, ex, va, tok: (ex[i], 0, f)),
            pl.BlockSpec((1, D_MODEL, tf), lambda i, f, ex, va, tok: (ex[i], 0, f)),
            pl.BlockSpec((1, tf, D_MODEL), lambda i, f, ex, va, tok: (ex[i], f, 0)),
        ],
        out_specs=pl.BlockSpec((tm, D_MODEL), lambda i, f, ex, va, tok: (i, 0)),
        scratch_shapes=[
            pltpu.VMEM((2, tm, D_MODEL), F32),
            pltpu.VMEM((tm, D_MODEL), BF16),
            pltpu.VMEM((tm, D_MODEL), F32),
            pltpu.SemaphoreType.DMA((2,)),
        ],
    )
    return pl.pallas_call(
        functools.partial(_moe_kernel, n_ff=n_ff),
        grid_spec=grid_spec,
        out_shape=jax.ShapeDtypeStruct((n_tiles * tm, D_MODEL), F32),
        compiler_params=_params("arbitrary", "arbitrary"),
        name="moe_experts",
    )(tile_exp, tile_valid, row_tok, h, wg, wu, wd)


def _combine_kernel(pos_ref, x_ref, meta_ref, ys_ref, o_ref, ybuf, sem, *, rows):
    for r in range(rows):
        _row_copy(ys_ref, pos_ref[0, 0, r], ybuf, r, sem).start(priority=0)
        _row_copy(ys_ref, pos_ref[0, 0, rows + r], ybuf, rows + r, sem).start(priority=1)
    pltpu.make_async_copy(ys_ref.at[pl.ds(0, 2 * rows)], ybuf, sem).wait()
    meta = meta_ref[...]
    w0 = meta[:, META_W0:META_W0 + 1]
    w1 = meta[:, META_W1:META_W1 + 1]
    o_ref[...] = x_ref[...] + w0 * ybuf[0:rows, :] + w1 * ybuf[rows:2 * rows, :]


def _combine(pos, x2, meta, ys):
    n = x2.shape[0]
    r = SCATTER_R
    return pl.pallas_call(
        functools.partial(_combine_kernel, rows=r),
        grid=(n // r,),
        in_specs=[
            pl.BlockSpec((1, 1, 2 * r), lambda i: (i, 0, 0), memory_space=pltpu.SMEM),
            pl.BlockSpec((r, D_MODEL), lambda i: (i, 0)),
            pl.BlockSpec((r, LANES), lambda i: (i, 0)),
            pl.BlockSpec(memory_space=pl.ANY),
        ],
        out_specs=pl.BlockSpec((r, D_MODEL), lambda i: (i, 0)),
        out_shape=jax.ShapeDtypeStruct((n, D_MODEL), F32),
        scratch_shapes=[pltpu.VMEM((2 * r, D_MODEL), F32), pltpu.SemaphoreType.DMA(())],
        compiler_params=_params("arbitrary"),
        name="moe_combine",
    )(pos, x2, meta, ys)


def _moe(x2, g, router_w, wg, wu, wd):
    n = x2.shape[0]
    tm = MOE_TM
    rw = jnp.zeros((D_MODEL, LANES), F32).at[:, :N_EXPERTS].set(router_w)
    h, meta, cnt = _router(x2, g, rw)
    counts = cnt[0, :N_EXPERTS].astype(jnp.int32)
    padded = ((counts + tm - 1) // tm) * tm
    ends = jnp.cumsum(padded)
    offs = ends - padded
    n_tiles = (2 * n) // tm + N_EXPERTS
    e0 = meta[:, META_E0].astype(jnp.int32)
    e1 = meta[:, META_E1].astype(jnp.int32)
    pos0 = offs[e0] + meta[:, META_R0].astype(jnp.int32)
    pos1 = offs[e1] + meta[:, META_R1].astype(jnp.int32)
    r = SCATTER_R
    pos = jnp.concatenate([pos0.reshape(n // r, 1, r), pos1.reshape(n // r, 1, r)], axis=-1)
    starts = jnp.arange(n_tiles, dtype=jnp.int32) * tm
    n_valid = ends[-1] // tm
    tile_valid = (starts < ends[-1]).astype(jnp.int32)
    tile_blk = jnp.minimum(jnp.arange(n_tiles, dtype=jnp.int32), n_valid - 1)
    tile_exp = jnp.sum((tile_blk[:, None] * tm >= ends[None, :]).astype(jnp.int32), axis=1)
    row_tok = _row_tokens(pos0, pos1, (n_tiles + 1) * tm)
    ys = _moe_experts(tile_exp, tile_valid, row_tok, h, wg, wu, wd)
    return _combine(pos, x2, meta, ys)


def _mixer(x2, batch, seq, rel_bias, bias_tiles, norm_g, w_in, conv_w, conv_b, igate_b, fgate_b, mlstm_norm_g,
           q_norm_g, k_norm_g, w_branch_a, w_branch_b, w_out):
    sizes = (M_WIDTH, M_WIDTH, M_WIDTH, M_WIDTH, M_HEADS, M_HEADS, A_WIDTH, A_WIDTH, A_WIDTH, D_MODEL, D_MODEL)
    cuts = [0]
    for s in sizes:
        cuts.append(cuts[-1] + s)
    mq, mk, mv, mo, mi, mf, aq, ak, av, ga, gb = [w_in[:, cuts[i]:cuts[i + 1]] for i in range(len(sizes))]
    wn = jnp.concatenate([ga, gb, mq, mk, mv, mo, ak], axis=1).astype(BF16)
    wt = jnp.concatenate([aq, av], axis=1).T.astype(BF16)
    wgate = jnp.zeros((D_MODEL, LANES), F32).at[:, :M_HEADS].set(mi).at[:, M_HEADS:2 * M_HEADS].set(mf).astype(BF16)
    gate_b = jnp.zeros((1, LANES), F32).at[0, :M_HEADS].set(igate_b).at[0, M_HEADS:2 * M_HEADS].set(fgate_b)
    z, zt, gates = _in_proj(x2, norm_g.reshape(1, D_MODEL), wn, wt, wgate)
    ha = _mlstm(z, gates, conv_w, conv_b.reshape(1, -1), gate_b, mlstm_norm_g.reshape(1, -1), batch, seq)
    qg_col = jnp.tile(q_norm_g, 2).reshape(2 * A_HEAD_DIM, 1)
    kg_row = jnp.tile(k_norm_g, 2).reshape(1, 2 * A_HEAD_DIM)
    ob = _moba(z, zt, bias_tiles, _moba_logit_bound(rel_bias, q_norm_g, k_norm_g), qg_col, kg_row, batch, seq)
    return _merge(x2, ha, ob, z, w_branch_a.astype(BF16), w_branch_b.astype(BF16), w_out.astype(BF16))


def kernel(x, rel_bias, mix_norm_g, w_in, conv_w, conv_b, igate_b, fgate_b, mlstm_norm_g, q_norm_g, k_norm_g,
           w_branch_a, w_branch_b, w_out, ffn_norm_g, dense_w_gate, dense_w_up, dense_w_down, router_w,
           expert_w_gate, expert_w_up, expert_w_down):
    batch, seq, d = x.shape
    depth = w_in.shape[0]
    assert d == D_MODEL and seq % MLSTM_T == 0 and seq % MOBA_BLOCK == 0
    assert (batch * seq) % MOE_TM == 0
    x2 = x.reshape(batch * seq, d)
    bias_tiles = _moba_bias_tiles(rel_bias)
    for layer in range(depth):
        x2 = _mixer(x2, batch, seq, rel_bias, bias_tiles, mix_norm_g[layer], w_in[layer], conv_w[layer], conv_b[layer],
                    igate_b[layer], fgate_b[layer], mlstm_norm_g[layer], q_norm_g[layer], k_norm_g[layer],
                    w_branch_a[layer], w_branch_b[layer], w_out[layer])
        g = ffn_norm_g[layer].reshape(1, d)
        j = layer // 2
        if layer % 2 == 0:
            x2 = _ffn(x2, g, dense_w_gate[j].astype(BF16), dense_w_up[j].astype(BF16), dense_w_down[j].astype(BF16))
        else:
            x2 = _moe(x2, g, router_w[j], expert_w_gate[j].astype(BF16), expert_w_up[j].astype(BF16),
                      expert_w_down[j].astype(BF16))
    return x2.reshape(batch, seq, d)
```

```python
import functools
import math

import jax
import jax.numpy as jnp
from jax import lax
from jax.experimental import pallas as pl
from jax.experimental.pallas import tpu as pltpu

F32 = jnp.float32
BF16 = jnp.bfloat16
HIGHEST = lax.Precision.HIGHEST

D_MODEL = 1024
M_HEADS = 4
M_HEAD_DIM = 128
M_WIDTH = M_HEADS * M_HEAD_DIM
CONV_WIDTH = 4
A_HEADS = 8
A_HEAD_DIM = 64
A_WIDTH = A_HEADS * A_HEAD_DIM
MOBA_BLOCK = 256
MOBA_TOPK = 3
REL_BUCKETS = 32
REL_MAX_DIST = 1024
N_EXPERTS = 8
EPS = 1e-6

LANES = 128
SUBLANES = 8
NEG = -1e30
VMEM_LIMIT = 56 * 1024 * 1024

C_GA = 0
C_GB = D_MODEL
C_MQ = 2 * D_MODEL
C_MK = C_MQ + M_WIDTH
C_MV = C_MK + M_WIDTH
C_MO = C_MV + M_WIDTH
C_AK = C_MO + M_WIDTH
NAT_WIDTH = C_AK + A_WIDTH
N_BIAS_TILES = 6

PROJ_TM = 512
MLSTM_T = 512
MLSTM_CHUNK = 128
MERGE_TM = 512
FFN_TM = 512
FFN_CHUNK = 256
ROUTER_TM = 512
MOE_TM = 512
MOE_TF = 1792
SCATTER_R = 256


def _params(*sem):
    return pltpu.CompilerParams(dimension_semantics=sem, vmem_limit_bytes=VMEM_LIMIT)


def _const_spec(shape):
    nd = len(shape)
    return pl.BlockSpec(shape, lambda *_: (0,) * nd)


def _rms(x, g):
    return x * lax.rsqrt(jnp.mean(x * x, axis=-1, keepdims=True) + EPS) * g


def _in_proj_kernel(x_ref, g_ref, wn_ref, wt_ref, wg_ref, z_ref, zt_ref, gt_ref):
    hn = _rms(x_ref[...], g_ref[...]).astype(BF16)
    for c in range(0, NAT_WIDTH, 512):
        z_ref[:, c:c + 512] = jnp.dot(hn, wn_ref[:, c:c + 512], preferred_element_type=F32).astype(BF16)
    for c in range(0, 2 * A_WIDTH, 256):
        zt_ref[c:c + 256, :] = lax.dot_general(
            wt_ref[c:c + 256, :], hn, (((1,), (1,)), ((), ())), preferred_element_type=F32).astype(BF16)
    gt_ref[...] = jnp.dot(hn, wg_ref[...], preferred_element_type=F32)


def _in_proj(x2, g, wn, wt, wg):
    n = x2.shape[0]
    tm = PROJ_TM
    return pl.pallas_call(
        _in_proj_kernel,
        grid=(n // tm,),
        in_specs=[
            pl.BlockSpec((tm, D_MODEL), lambda i: (i, 0)),
            _const_spec((1, D_MODEL)),
            _const_spec((D_MODEL, NAT_WIDTH)),
            _const_spec((2 * A_WIDTH, D_MODEL)),
            _const_spec((D_MODEL, LANES)),
        ],
        out_specs=[
            pl.BlockSpec((tm, NAT_WIDTH), lambda i: (i, 0)),
            pl.BlockSpec((2 * A_WIDTH, tm), lambda i: (0, i)),
            pl.BlockSpec((tm, LANES), lambda i: (i, 0)),
        ],
        out_shape=[
            jax.ShapeDtypeStruct((n, NAT_WIDTH), BF16),
            jax.ShapeDtypeStruct((2 * A_WIDTH, n), BF16),
            jax.ShapeDtypeStruct((n, LANES), F32),
        ],
        compiler_params=_params("arbitrary"),
        name="in_proj",
    )(x2, g, wn, wt, wg)


def _mlstm_kernel(zq_ref, zk_ref, zv_ref, zo_ref, gt_ref, cw_ref, cb_ref, gb_ref, ng_ref, o_ref,
                  qk_buf, q_s, k_s, c_s, m_s, *, t_blk, chunk):
    L = chunk
    DH = M_HEAD_DIM
    assert L == LANES and DH == LANES

    @pl.when(pl.program_id(1) == 0)
    def _init():
        qk_buf[0:SUBLANES, :] = jnp.zeros((SUBLANES, 2 * M_WIDTH), F32)
        c_s[...] = jnp.zeros_like(c_s)
        m_s[...] = jnp.zeros_like(m_s)

    qk_buf[SUBLANES:SUBLANES + t_blk, 0:M_WIDTH] = zq_ref[...].astype(F32)
    qk_buf[SUBLANES:SUBLANES + t_blk, M_WIDTH:] = zk_ref[...].astype(F32)
    acc = cb_ref[...] + cw_ref[CONV_WIDTH - 1:CONV_WIDTH, :] * qk_buf[SUBLANES:SUBLANES + t_blk, :]
    for j in range(CONV_WIDTH - 1):
        off = SUBLANES - (CONV_WIDTH - 1) + j
        acc = acc + cw_ref[j:j + 1, :] * qk_buf[off:off + t_blk, :]
    qk = acc * jax.nn.sigmoid(acc)
    qk_buf[0:SUBLANES, :] = qk_buf[t_blk:t_blk + SUBLANES, :]
    q_s[...] = qk[:, :M_WIDTH].astype(BF16)
    k_s[...] = qk[:, M_WIDTH:] * (DH ** -0.5)

    row = lax.broadcasted_iota(jnp.int32, (L, L), 0)
    col = lax.broadcasted_iota(jnp.int32, (L, L), 1)
    causal = col <= row
    tri = causal.astype(F32)
    lane = lax.broadcasted_iota(jnp.int32, (L, LANES), 1)
    ones_blk = jnp.ones((L, LANES), BF16)

    def chunk_body(c, carry):
        r0 = c * L
        g_pre = gt_ref[pl.ds(r0, L), :] + gb_ref[...]
        log_f = jnp.minimum(g_pre, 0.0) - jnp.log1p(jnp.exp(-jnp.abs(g_pre)))
        bcum = jnp.dot(tri, log_f, precision=HIGHEST, preferred_element_type=F32)
        gm = jnp.where(lane < M_HEADS, g_pre, bcum)
        gm_t = gm.T
        c_all = [c_s[h] for h in range(M_HEADS)]
        m_all = m_s[...]
        heads = range(M_HEADS)
        hs = [slice(h * DH, (h + 1) * DH) for h in heads]
        qh = [q_s[pl.ds(r0, L), hs[h]] for h in heads]
        kf = [k_s[pl.ds(r0, L), hs[h]] for h in heads]
        v_ext = [jnp.concatenate([zv_ref[pl.ds(r0, L), hs[h]], ones_blk], axis=1) for h in heads]
        qk = [lax.dot_general(qh[h], kf[h].astype(BF16), (((1,), (1,)), ((), ())), preferred_element_type=F32)
              for h in heads]
        inter = [jnp.dot(qh[h], c_all[h].astype(BF16), preferred_element_type=F32) for h in heads]
        mt, w_state, s_bf, kw_t, decay, m_new = [], [], [], [], [], []
        for h in heads:
            bb = jnp.broadcast_to(gm[:, M_HEADS + h:M_HEADS + h + 1], (L, LANES))
            ii = jnp.broadcast_to(gm[:, h:h + 1], (L, LANES))
            bb_row = gm_t[M_HEADS + h:M_HEADS + h + 1, :]
            ii_row = gm_t[h:h + 1, :]
            m_old = m_all[h:h + 1, :]
            dlog = jnp.where(causal, bb - (bb_row - ii_row), -jnp.inf)
            a = bb + m_old
            mt.append(jnp.maximum(a, jnp.broadcast_to(jnp.max(dlog, axis=-1, keepdims=True), (L, LANES))))
            w_state.append(jnp.exp(a - mt[h]))
            s_bf.append((qk[h] * jnp.exp(dlog - mt[h])).astype(BF16))
            b_last = bb[L - 1:L, :]
            g = b_last - bb + ii
            m_new.append(jnp.maximum(b_last + m_old, jnp.max(g, axis=0, keepdims=True)))
            decay.append(jnp.exp(b_last + m_old - m_new[h]))
            kw_t.append((kf[h] * jnp.exp(g - m_new[h])).T.astype(BF16))
        intra = [jnp.dot(s_bf[h], v_ext[h], preferred_element_type=F32) for h in heads]
        upd = [jnp.dot(kw_t[h], v_ext[h], preferred_element_type=F32) for h in heads]
        for h in heads:
            num = intra[h][:, :DH] + w_state[h] * inter[h][:, :DH]
            den = intra[h][:, DH:] + w_state[h] * inter[h][:, DH:]
            h_t = num / jnp.maximum(jnp.abs(den), jnp.exp(-mt[h]))
            hc = jax.nn.sigmoid(zo_ref[pl.ds(r0, L), hs[h]].astype(F32)) * h_t
            o_ref[pl.ds(r0, L), hs[h]] = _rms(hc, ng_ref[:, hs[h]]).astype(BF16)
        for h in heads:
            c_s[h] = jnp.concatenate([decay[h], decay[h]], axis=1) * c_all[h] + upd[h]
            m_s[h:h + 1, :] = m_new[h]
        return carry

    for c in range(t_blk // L):
        chunk_body(c, 0)


def _mlstm(z, gates, conv_w, conv_b, gate_b, norm_g, batch, seq):
    n = batch * seq
    t = MLSTM_T
    nt = seq // t
    row_blk = lambda b, s: b * nt + s
    zspec = lambda cb: pl.BlockSpec((t, M_WIDTH), lambda b, s: (row_blk(b, s), cb))
    return pl.pallas_call(
        functools.partial(_mlstm_kernel, t_blk=t, chunk=MLSTM_CHUNK),
        grid=(batch, nt),
        in_specs=[
            zspec(C_MQ // M_WIDTH), zspec(C_MK // M_WIDTH), zspec(C_MV // M_WIDTH), zspec(C_MO // M_WIDTH),
            pl.BlockSpec((t, LANES), lambda b, s: (row_blk(b, s), 0)),
            _const_spec((CONV_WIDTH, 2 * M_WIDTH)),
            _const_spec((1, 2 * M_WIDTH)),
            _const_spec((1, LANES)),
            _const_spec((1, M_WIDTH)),
        ],
        out_specs=pl.BlockSpec((t, M_WIDTH), lambda b, s: (row_blk(b, s), 0)),
        out_shape=jax.ShapeDtypeStruct((n, M_WIDTH), BF16),
        scratch_shapes=[
            pltpu.VMEM((t + SUBLANES, 2 * M_WIDTH), F32),
            pltpu.VMEM((t, M_WIDTH), BF16),
            pltpu.VMEM((t, M_WIDTH), F32),
            pltpu.VMEM((M_HEADS, M_HEAD_DIM, M_HEAD_DIM + LANES), F32),
            pltpu.VMEM((SUBLANES, LANES), F32),
        ],
        compiler_params=_params("arbitrary", "arbitrary"),
        name="mlstm",
    )(z, z, z, z, gates, conv_w, conv_b, gate_b, norm_g)


MOBA_V_ROWS = A_HEAD_DIM + 16
MOBA_ROWB_ROWS = 24
MOBA_GROUP = 34
LOG2E = math.log2(math.e)
FAST_SOFTMAX_MIN_DENOM = 1e-25


def _moba_items(n_blocks):
    items = []
    for qb in range(n_blocks):
        items.append((qb, qb, 0, 0))
        for j in range(qb):
            items.append((qb, j, min(qb - j, N_BIAS_TILES - 1), j + 1))
    n_groups = -(-len(items) // MOBA_GROUP)
    noop = (n_blocks - 1, 0, 0, MOBA_ROWB_ROWS - 1)
    items += [noop] * (n_groups * MOBA_GROUP + 2 - len(items))
    return n_groups, [jnp.asarray([it[c] for it in items], jnp.int32) for c in range(4)]


def _moba_kernel(it_q, it_blk, it_tile, it_row, qt_ref, k_ref, vt_ref, bias_ref, bound_ref, qg_ref, kg_ref, o_ref,
                 kn_s, vt_s, kmean_s, qh_s, rowb_s, acc_s, st_a, st_b, p_a, p_b, *, n_blocks, n_groups):
    BS = MOBA_BLOCK
    DA = A_HEAD_DIM
    seq = n_blocks * BS
    lane_k = lax.broadcasted_iota(jnp.int32, (BS, LANES), 1)
    head0_k = lane_k < DA

    ones_row = jnp.where(lax.broadcasted_iota(jnp.int32, (16, BS), 0) == 0, 1.0, 0.0).astype(BF16)
    for j in range(n_blocks):
        kf = k_ref[j * BS:(j + 1) * BS, :].astype(F32)
        k2 = kf * kf
        s0 = jnp.sum(jnp.where(head0_k, k2, 0.0), axis=-1, keepdims=True)
        s1 = jnp.sum(jnp.where(head0_k, 0.0, k2), axis=-1, keepdims=True)
        inv = jnp.where(head0_k, lax.rsqrt(s0 / DA + EPS), lax.rsqrt(s1 / DA + EPS))
        kn = kf * inv * kg_ref[...]
        kn_s[j * BS:(j + 1) * BS, :] = kn.astype(BF16)
        kmean_s[j:j + 1, :] = jnp.mean(kn, axis=0, keepdims=True)
        for h in range(2):
            vt_s[j, h, 0:DA, :] = vt_ref[h * DA:(h + 1) * DA, j * BS:(j + 1) * BS]
            vt_s[j, h, DA:MOBA_V_ROWS, :] = ones_row

    qf = qt_ref[...].astype(F32)
    sub_q = lax.broadcasted_iota(jnp.int32, qf.shape, 0)
    head0_q = sub_q < DA
    q2 = qf * qf
    ss0 = jnp.sum(jnp.where(head0_q, q2, 0.0), axis=0, keepdims=True)
    ss1 = jnp.sum(jnp.where(head0_q, 0.0, q2), axis=0, keepdims=True)
    qn = qf * jnp.where(head0_q, lax.rsqrt(ss0 / DA + EPS), lax.rsqrt(ss1 / DA + EPS)) * qg_ref[...]

    blk = lax.broadcasted_iota(jnp.int32, (n_blocks, seq), 0)
    past = blk < lax.broadcasted_iota(jnp.int32, (n_blocks, seq), 1) // BS
    lane_m = lax.broadcasted_iota(jnp.int32, (n_blocks, LANES), 1)
    kmean = kmean_s[...]
    kmean2 = jnp.concatenate([jnp.where(lane_m < DA, kmean, 0.0), jnp.where(lane_m < DA, 0.0, kmean)], axis=0)
    gates = jnp.dot(kmean2, qn, precision=HIGHEST, preferred_element_type=F32)
    for h in range(2):
        gate = jnp.where(past, gates[h * n_blocks:(h + 1) * n_blocks, :], -jnp.inf)
        rank = jnp.zeros((n_blocks, seq), jnp.int32)
        for j2 in range(n_blocks):
            other = gate[j2:j2 + 1, :]
            beats = (other > gate) | ((other == gate) & (j2 < blk))
            rank = rank + beats.astype(jnp.int32)
        bound = bound_ref[h, :, 0:1]
        selb = jnp.where(past & (rank < MOBA_TOPK), 0.0, NEG) - bound
        hmask_q = head0_q if h == 0 else jnp.logical_not(head0_q)
        qh = jnp.where(hmask_q, qn * (DA ** -0.5 * LOG2E), 0.0).astype(BF16)
        for qb in range(n_blocks):
            cols = slice(qb * BS, (qb + 1) * BS)
            qh_s[qb, h] = qh[:, cols]
            rowb_s[qb, h, 0:1, :] = jnp.broadcast_to(-bound, (1, BS))
            rowb_s[qb, h, 1:n_blocks + 1, :] = selb[:, cols]
            rowb_s[qb, h, n_blocks + 1:, :] = jnp.full((MOBA_ROWB_ROWS - n_blocks - 1, BS), NEG, F32)
    acc_s[...] = jnp.zeros_like(acc_s)

    def scores(h, qb, j, tile, row):
        kj = kn_s[pl.ds(pl.multiple_of(j * BS, BS), BS), :]
        st = jnp.dot(kj, qh_s[qb, h], preferred_element_type=F32)
        return st + bias_ref[h, tile] + rowb_s[qb, h, pl.ds(row, 1), :]

    def pv(h, j, p):
        return jnp.dot(vt_s[j, h], p, preferred_element_type=F32)

    def stage_scores(i, st_ref):
        for h in range(2):
            st_ref[h] = scores(h, it_q[i], it_blk[i], it_tile[i], it_row[i])

    def stage_exp(st_ref, p_ref):
        for h in range(2):
            p_ref[h] = jnp.exp2(st_ref[h]).astype(BF16)

    def stage_pv(i, p_ref):
        for h in range(2):
            acc_s[it_q[i], h] += pv(h, it_blk[i], p_ref[h])

    def group(m, carry):
        for u in range(0, MOBA_GROUP, 2):
            i = MOBA_GROUP * m + u
            stage_pv(i, p_a)
            stage_scores(i + 2, st_a)
            stage_exp(st_b, p_b)
            stage_pv(i + 1, p_b)
            stage_scores(i + 3, st_b)
            stage_exp(st_a, p_a)
        return carry

    stage_scores(0, st_a)
    stage_exp(st_a, p_a)
    stage_scores(1, st_b)
    lax.fori_loop(0, n_groups, group, 0)

    def finish(qb, l_min):
        outs = []
        for h in range(2):
            acc = acc_s[qb, h]
            l = acc[DA:DA + 1, :]
            outs.append(acc[0:DA, :] / l)
            l_min = jnp.minimum(l_min, jnp.min(l))
        o_ref[pl.ds(pl.multiple_of(qb * BS, BS), BS), :] = jnp.concatenate(outs, axis=0).T.astype(BF16)
        return l_min

    l_min = lax.fori_loop(0, n_blocks, finish, jnp.float32(jnp.inf))

    def online(qb, carry):
        state = []
        for h in range(2):
            st = scores(h, qb, qb, 0, 0)
            m = jnp.max(st, axis=0, keepdims=True)
            full = pv(h, qb, jnp.exp2(st - m).astype(BF16))
            state += [m, full]

        def past_body(j, state):
            tile = jnp.minimum(qb - j, N_BIAS_TILES - 1)
            new = []
            for h in range(2):
                m_old, full_old = state[2 * h:2 * h + 2]
                st = scores(h, qb, j, tile, j + 1)
                m_new = jnp.maximum(m_old, jnp.max(st, axis=0, keepdims=True))
                full = pv(h, j, jnp.exp2(st - m_new).astype(BF16))
                new += [m_new, jnp.exp2(m_old - m_new) * full_old + full]
            return tuple(new)

        state = lax.fori_loop(0, qb, past_body, tuple(state))
        outs = [state[2 * h + 1][0:DA, :] / state[2 * h + 1][DA:DA + 1, :] for h in range(2)]
        o_ref[pl.ds(pl.multiple_of(qb * BS, BS), BS), :] = jnp.concatenate(outs, axis=0).T.astype(BF16)
        return carry

    @pl.when(l_min < FAST_SOFTMAX_MIN_DENOM)
    def _redo():
        lax.fori_loop(0, n_blocks, online, 0)


def _moba_logit_bound(rel_bias, q_norm_g, k_norm_g):
    qk = A_HEAD_DIM * jnp.max(jnp.abs(q_norm_g)) * jnp.max(jnp.abs(k_norm_g)) * (A_HEAD_DIM ** -0.5) * 1.02
    b = (qk + jnp.max(rel_bias, axis=0)) * LOG2E
    return jnp.broadcast_to(b[:, None, None], (A_HEADS, 1, LANES)).astype(F32)


def _moba(z, zt, bias_tiles, bound, qg_col, kg_row, batch, seq):
    n = batch * seq
    nb = seq // MOBA_BLOCK
    bs = MOBA_BLOCK
    hp = A_HEADS // 2
    w2 = 2 * A_HEAD_DIM
    assert nb + 2 <= MOBA_ROWB_ROWS
    n_groups, items = _moba_items(nb)
    grid_spec = pltpu.PrefetchScalarGridSpec(
        num_scalar_prefetch=len(items),
        grid=(batch, hp),
        in_specs=[
            pl.BlockSpec((w2, seq), lambda b, p, *_: (p, b)),
            pl.BlockSpec((seq, w2), lambda b, p, *_: (b, C_AK // w2 + p)),
            pl.BlockSpec((w2, seq), lambda b, p, *_: (hp + p, b)),
            pl.BlockSpec((2, N_BIAS_TILES, bs, bs), lambda b, p, *_: (p, 0, 0, 0)),
            pl.BlockSpec((2, 1, LANES), lambda b, p, *_: (p, 0, 0)),
            pl.BlockSpec((w2, 1), lambda b, p, *_: (0, 0)),
            pl.BlockSpec((1, w2), lambda b, p, *_: (0, 0)),
        ],
        out_specs=pl.BlockSpec((seq, w2), lambda b, p, *_: (b, p)),
        scratch_shapes=[
            pltpu.VMEM((seq, w2), BF16),
            pltpu.VMEM((nb, 2, MOBA_V_ROWS, bs), BF16),
            pltpu.VMEM((nb, w2), F32),
            pltpu.VMEM((nb, 2, w2, bs), BF16),
            pltpu.VMEM((nb, 2, MOBA_ROWB_ROWS, bs), F32),
            pltpu.VMEM((nb, 2, MOBA_V_ROWS, bs), F32),
            pltpu.VMEM((2, bs, bs), F32),
            pltpu.VMEM((2, bs, bs), F32),
            pltpu.VMEM((2, bs, bs), BF16),
            pltpu.VMEM((2, bs, bs), BF16),
        ],
    )
    return pl.pallas_call(
        functools.partial(_moba_kernel, n_blocks=nb, n_groups=n_groups),
        grid_spec=grid_spec,
        out_shape=jax.ShapeDtypeStruct((n, A_WIDTH), BF16),
        compiler_params=_params("arbitrary", "arbitrary"),
        name="moba",
    )(*items, zt, z, zt, bias_tiles, bound, qg_col, kg_row)


def _t5_bucket(dist):
    n = jnp.maximum(dist, 0)
    max_exact = REL_BUCKETS // 2
    log_ratio = jnp.log(jnp.maximum(n, max_exact).astype(F32) / max_exact) / math.log(REL_MAX_DIST / max_exact)
    large = max_exact + (log_ratio * (REL_BUCKETS - max_exact)).astype(jnp.int32)
    large = jnp.minimum(large, REL_BUCKETS - 1)
    return jnp.where(n < max_exact, n, large)


def _bias_tiles_kernel(rb_ref, bucket_ref, o_ref):
    bucket = bucket_ref[0]
    hit = [bucket == b for b in range(REL_BUCKETS)]
    tk = lax.broadcasted_iota(jnp.int32, bucket.shape, 0)
    tq = lax.broadcasted_iota(jnp.int32, bucket.shape, 1)
    masked = (tk > tq) & (pl.program_id(0) == 0)
    for h in range(A_HEADS):
        acc = jnp.zeros(bucket.shape, F32)
        for b in range(REL_BUCKETS):
            acc = jnp.where(hit[b], rb_ref[b, h], acc)
        o_ref[h, 0] = jnp.where(masked, NEG, acc * LOG2E)


def _moba_bias_tiles(rel_bias):
    assert (N_BIAS_TILES - 1) * MOBA_BLOCK - (MOBA_BLOCK - 1) >= REL_MAX_DIST
    tk = jnp.arange(MOBA_BLOCK)[None, :, None]
    tq = jnp.arange(MOBA_BLOCK)[None, None, :]
    diff = jnp.arange(N_BIAS_TILES)[:, None, None]
    bucket = _t5_bucket(diff * MOBA_BLOCK + tq - tk).astype(jnp.int32)
    return pl.pallas_call(
        _bias_tiles_kernel,
        grid=(N_BIAS_TILES,),
        in_specs=[
            pl.BlockSpec(memory_space=pltpu.SMEM),
            pl.BlockSpec((1, MOBA_BLOCK, MOBA_BLOCK), lambda t: (t, 0, 0)),
        ],
        out_specs=pl.BlockSpec((A_HEADS, 1, MOBA_BLOCK, MOBA_BLOCK), lambda t: (0, t, 0, 0)),
        out_shape=jax.ShapeDtypeStruct((A_HEADS, N_BIAS_TILES, MOBA_BLOCK, MOBA_BLOCK), F32),
        compiler_params=_params("arbitrary"),
        name="moba_bias_tiles",
    )(rel_bias.astype(F32), bucket)


def _merge_kernel(x_ref, ha_ref, ob_ref, ga_ref, gb_ref, wa_ref, wb_ref, wo_ref, o_ref):
    ya = jnp.dot(ha_ref[...], wa_ref[...], preferred_element_type=F32)
    yb = jnp.dot(ob_ref[...], wb_ref[...], preferred_element_type=F32)
    y = jax.nn.sigmoid(ga_ref[...].astype(F32)) * ya + jax.nn.sigmoid(gb_ref[...].astype(F32)) * yb
    o_ref[...] = x_ref[...] + jnp.dot(y.astype(BF16), wo_ref[...], preferred_element_type=F32)


def _merge(x2, ha, ob, z, wa, wb, wo):
    n = x2.shape[0]
    tm = MERGE_TM
    return pl.pallas_call(
        _merge_kernel,
        grid=(n // tm,),
        in_specs=[
            pl.BlockSpec((tm, D_MODEL), lambda i: (i, 0)),
            pl.BlockSpec((tm, M_WIDTH), lambda i: (i, 0)),
            pl.BlockSpec((tm, A_WIDTH), lambda i: (i, 0)),
            pl.BlockSpec((tm, D_MODEL), lambda i: (i, C_GA // D_MODEL)),
            pl.BlockSpec((tm, D_MODEL), lambda i: (i, C_GB // D_MODEL)),
            _const_spec((M_WIDTH, D_MODEL)),
            _const_spec((A_WIDTH, D_MODEL)),
            _const_spec((D_MODEL, D_MODEL)),
        ],
        out_specs=pl.BlockSpec((tm, D_MODEL), lambda i: (i, 0)),
        out_shape=jax.ShapeDtypeStruct((n, D_MODEL), F32),
        compiler_params=_params("arbitrary"),
        name="merge",
    )(x2, ha, ob, z, z, wa, wb, wo)


def _ffn_kernel(x_ref, g_ref, wg_ref, wu_ref, wd_ref, o_ref, *, d_ff):
    x = x_ref[...]
    hn = _rms(x, g_ref[...]).astype(BF16)
    acc = x
    for c in range(0, d_ff, FFN_CHUNK):
        g = jnp.dot(hn, wg_ref[:, c:c + FFN_CHUNK], preferred_element_type=F32)
        u = jnp.dot(hn, wu_ref[:, c:c + FFN_CHUNK], preferred_element_type=F32)
        a = (g * jax.nn.sigmoid(g) * u).astype(BF16)
        acc = acc + jnp.dot(a, wd_ref[c:c + FFN_CHUNK, :], preferred_element_type=F32)
    o_ref[...] = acc


def _ffn(x2, g, wg, wu, wd):
    n = x2.shape[0]
    d_ff = wg.shape[1]
    assert d_ff % FFN_CHUNK == 0
    tm = FFN_TM
    return pl.pallas_call(
        functools.partial(_ffn_kernel, d_ff=d_ff),
        grid=(n // tm,),
        in_specs=[
            pl.BlockSpec((tm, D_MODEL), lambda i: (i, 0)),
            _const_spec((1, D_MODEL)),
            _const_spec((D_MODEL, d_ff)),
            _const_spec((D_MODEL, d_ff)),
            _const_spec((d_ff, D_MODEL)),
        ],
        out_specs=pl.BlockSpec((tm, D_MODEL), lambda i: (i, 0)),
        out_shape=jax.ShapeDtypeStruct((n, D_MODEL), F32),
        compiler_params=_params("arbitrary"),
        name="ffn_dense",
    )(x2, g, wg, wu, wd)


META_W0, META_W1, META_E0, META_E1, META_R0, META_R1 = range(6)


def _router_kernel(x_ref, g_ref, rw_ref, h_ref, meta_ref, cnt_ref, carry_s):
    @pl.when(pl.program_id(0) == 0)
    def _init():
        carry_s[...] = jnp.zeros_like(carry_s)

    h = _rms(x_ref[...], g_ref[...])
    h_ref[...] = h
    tm = h.shape[0]
    logits = jnp.dot(h, rw_ref[...], precision=HIGHEST, preferred_element_type=F32)
    lane = lax.broadcasted_iota(jnp.int32, (tm, LANES), 1)
    lg = jnp.where(lane < N_EXPERTS, logits, -jnp.inf)
    m1 = jnp.max(lg, axis=-1, keepdims=True)
    i1 = jnp.min(jnp.where(lg == m1, lane, LANES), axis=-1, keepdims=True)
    lg2 = jnp.where(lane == i1, -jnp.inf, lg)
    m2 = jnp.max(lg2, axis=-1, keepdims=True)
    i2 = jnp.min(jnp.where(lg2 == m2, lane, LANES), axis=-1, keepdims=True)
    e = jnp.exp(m2 - m1)
    w1 = 1.0 / (1.0 + e)
    w2 = e / (1.0 + e)
    hit1 = lane == i1
    hit2 = lane == i2
    onehot = jnp.where(hit1 | hit2, 1.0, 0.0)
    r = lax.broadcasted_iota(jnp.int32, (tm, tm), 0)
    c = lax.broadcasted_iota(jnp.int32, (tm, tm), 1)
    before = jnp.where(c < r, 1.0, 0.0).astype(BF16)
    carry = carry_s[0:1, :]
    pref = jnp.dot(before, onehot.astype(BF16), preferred_element_type=F32) + carry
    r1 = jnp.sum(jnp.where(hit1, pref, 0.0), axis=-1, keepdims=True)
    r2 = jnp.sum(jnp.where(hit2, pref, 0.0), axis=-1, keepdims=True)
    new_carry = carry + jnp.sum(onehot, axis=0, keepdims=True)
    carry_s[...] = jnp.broadcast_to(new_carry, carry_s.shape)
    cnt_ref[...] = jnp.broadcast_to(new_carry, cnt_ref.shape)
    meta = jnp.zeros((tm, LANES), F32)
    for idx, val in ((META_W0, w1), (META_W1, w2), (META_E0, i1.astype(F32)), (META_E1, i2.astype(F32)),
                     (META_R0, r1), (META_R1, r2)):
        meta = jnp.where(lane == idx, val, meta)
    meta_ref[...] = meta


def _router(x2, g, rw):
    n = x2.shape[0]
    tm = ROUTER_TM
    return pl.pallas_call(
        _router_kernel,
        grid=(n // tm,),
        in_specs=[
            pl.BlockSpec((tm, D_MODEL), lambda i: (i, 0)),
            _const_spec((1, D_MODEL)),
            _const_spec((D_MODEL, LANES)),
        ],
        out_specs=[
            pl.BlockSpec((tm, D_MODEL), lambda i: (i, 0)),
            pl.BlockSpec((tm, LANES), lambda i: (i, 0)),
            _const_spec((SUBLANES, LANES)),
        ],
        out_shape=[
            jax.ShapeDtypeStruct((n, D_MODEL), F32),
            jax.ShapeDtypeStruct((n, LANES), F32),
            jax.ShapeDtypeStruct((SUBLANES, LANES), F32),
        ],
        scratch_shapes=[pltpu.VMEM((SUBLANES, LANES), F32)],
        compiler_params=_params("arbitrary"),
        name="moe_router",
    )(x2, g, rw)


def _row_copy(src_ref, src_row, dst_ref, dst_row, sem):
    return pltpu.make_async_copy(src_ref.at[pl.ds(src_row, 1)], dst_ref.at[pl.ds(dst_row, 1)], sem)


def _scatter_kernel(pos_ref, pad_ref, h_ref, xs_ref, zero_s, sem, *, rows, pads):
    @pl.when(pl.program_id(0) == 0)
    def _init():
        zero_s[...] = jnp.zeros_like(zero_s)

    for r in range(rows):
        _row_copy(h_ref, r, xs_ref, pos_ref[0, 0, r], sem).start(priority=0)
        _row_copy(h_ref, r, xs_ref, pos_ref[0, 0, rows + r], sem).start(priority=1)
    for r in range(pads):
        _row_copy(zero_s, 0, xs_ref, pad_ref[0, 0, r], sem).start(priority=r % 2)
    for _ in range(2):
        pltpu.make_async_copy(h_ref, xs_ref.at[pl.ds(0, rows)], sem).wait()
    pltpu.make_async_copy(h_ref.at[pl.ds(0, pads)], xs_ref.at[pl.ds(0, pads)], sem).wait()


def _scatter(pos, pad_pos, h, n_rows):
    n = h.shape[0]
    r = SCATTER_R
    steps = n // r
    pads = pad_pos.shape[0] // steps
    assert pads * steps == pad_pos.shape[0] and 2 * n + pad_pos.shape[0] == n_rows
    return pl.pallas_call(
        functools.partial(_scatter_kernel, rows=r, pads=pads),
        grid=(steps,),
        in_specs=[
            pl.BlockSpec((1, 1, 2 * r), lambda i: (i, 0, 0), memory_space=pltpu.SMEM),
            pl.BlockSpec((1, 1, pads), lambda i: (i, 0, 0), memory_space=pltpu.SMEM),
            pl.BlockSpec((r, D_MODEL), lambda i: (i, 0)),
        ],
        out_specs=pl.BlockSpec(memory_space=pl.ANY),
        out_shape=jax.ShapeDtypeStruct((n_rows, D_MODEL), F32),
        scratch_shapes=[pltpu.VMEM((SUBLANES, D_MODEL), F32), pltpu.SemaphoreType.DMA(())],
        compiler_params=_params("arbitrary"),
        name="moe_scatter",
    )(pos, pad_pos.reshape(steps, 1, pads), h)


def _moe_kernel(blk_ref, exp_ref, valid_ref, xs_ref, wg_ref, wu_ref, wd_ref, y_ref, xb_s, acc_s, *, n_ff):
    del blk_ref, exp_ref
    i = pl.program_id(0)
    f = pl.program_id(1)

    @pl.when((valid_ref[i] == 0) & (f == n_ff - 1))
    def _idle():
        y_ref[...] = jnp.zeros_like(y_ref)

    @pl.when(valid_ref[i] == 1)
    def _tile():
        @pl.when(f == 0)
        def _cast():
            xb_s[...] = xs_ref[...].astype(BF16)

        xb = xb_s[...]
        d = None
        for c in range(0, wg_ref.shape[2], FFN_CHUNK):
            g = jnp.dot(xb, wg_ref[0, :, c:c + FFN_CHUNK], preferred_element_type=F32)
            u = jnp.dot(xb, wu_ref[0, :, c:c + FFN_CHUNK], preferred_element_type=F32)
            a = (g * jax.nn.sigmoid(g) * u).astype(BF16)
            dc = jnp.dot(a, wd_ref[0, c:c + FFN_CHUNK, :], preferred_element_type=F32)
            d = dc if d is None else d + dc

        @pl.when(f == 0)
        def _first():
            acc_s[...] = d

        @pl.when(f > 0)
        def _rest():
            acc_s[...] += d

        @pl.when(f == n_ff - 1)
        def _out():
            y_ref[...] = acc_s[...]


def _moe_experts(tile_blk, tile_exp, tile_valid, xs, wg, wu, wd):
    rows = xs.shape[0]
    d_ff = wg.shape[2]
    tm, tf = MOE_TM, MOE_TF
    n_ff = d_ff // tf
    assert rows % tm == 0 and d_ff % tf == 0
    grid_spec = pltpu.PrefetchScalarGridSpec(
        num_scalar_prefetch=3,
        grid=(rows // tm, n_ff),
        in_specs=[
            pl.BlockSpec((tm, D_MODEL), lambda i, f, blk, ex, va: (blk[i], 0)),
            pl.BlockSpec((1, D_MODEL, tf), lambda i, f, blk, ex, va: (ex[i], 0, f)),
            pl.BlockSpec((1, D_MODEL, tf), lambda i, f, blk, ex, va: (ex[i], 0, f)),
            pl.BlockSpec((1, tf, D_MODEL), lambda i, f, blk, ex, va: (ex[i], f, 0)),
        ],
        out_specs=pl.BlockSpec((tm, D_MODEL), lambda i, f, blk, ex, va: (i, 0)),
        scratch_shapes=[pltpu.VMEM((tm, D_MODEL), BF16), pltpu.VMEM((tm, D_MODEL), F32)],
    )
    return pl.pallas_call(
        functools.partial(_moe_kernel, n_ff=n_ff),
        grid_spec=grid_spec,
        out_shape=jax.ShapeDtypeStruct((rows, D_MODEL), F32),
        compiler_params=_params("arbitrary", "arbitrary"),
        name="moe_experts",
    )(tile_blk, tile_exp, tile_valid, xs, wg, wu, wd)


def _combine_kernel(pos_ref, x_ref, meta_ref, ys_ref, o_ref, ybuf, sem, *, rows):
    for r in range(rows):
        _row_copy(ys_ref, pos_ref[0, 0, r], ybuf, r, sem).start(priority=0)
        _row_copy(ys_ref, pos_ref[0, 0, rows + r], ybuf, rows + r, sem).start(priority=1)
    pltpu.make_async_copy(ys_ref.at[pl.ds(0, 2 * rows)], ybuf, sem).wait()
    meta = meta_ref[...]
    w0 = meta[:, META_W0:META_W0 + 1]
    w1 = meta[:, META_W1:META_W1 + 1]
    o_ref[...] = x_ref[...] + w0 * ybuf[0:rows, :] + w1 * ybuf[rows:2 * rows, :]


def _combine(pos, x2, meta, ys):
    n = x2.shape[0]
    r = SCATTER_R
    return pl.pallas_call(
        functools.partial(_combine_kernel, rows=r),
        grid=(n // r,),
        in_specs=[
            pl.BlockSpec((1, 1, 2 * r), lambda i: (i, 0, 0), memory_space=pltpu.SMEM),
            pl.BlockSpec((r, D_MODEL), lambda i: (i, 0)),
            pl.BlockSpec((r, LANES), lambda i: (i, 0)),
            pl.BlockSpec(memory_space=pl.ANY),
        ],
        out_specs=pl.BlockSpec((r, D_MODEL), lambda i: (i, 0)),
        out_shape=jax.ShapeDtypeStruct((n, D_MODEL), F32),
        scratch_shapes=[pltpu.VMEM((2 * r, D_MODEL), F32), pltpu.SemaphoreType.DMA(())],
        compiler_params=_params("arbitrary"),
        name="moe_combine",
    )(pos, x2, meta, ys)


def _moe(x2, g, router_w, wg, wu, wd):
    n = x2.shape[0]
    tm = MOE_TM
    rw = jnp.zeros((D_MODEL, LANES), F32).at[:, :N_EXPERTS].set(router_w)
    h, meta, cnt = _router(x2, g, rw)
    counts = cnt[0, :N_EXPERTS].astype(jnp.int32)
    padded = ((counts + tm - 1) // tm) * tm
    ends = jnp.cumsum(padded)
    offs = ends - padded
    n_tiles = (2 * n) // tm + N_EXPERTS
    e0 = meta[:, META_E0].astype(jnp.int32)
    e1 = meta[:, META_E1].astype(jnp.int32)
    pos0 = offs[e0] + meta[:, META_R0].astype(jnp.int32)
    pos1 = offs[e1] + meta[:, META_R1].astype(jnp.int32)
    r = SCATTER_R
    pos = jnp.concatenate([pos0.reshape(n // r, 1, r), pos1.reshape(n // r, 1, r)], axis=-1)
    starts = jnp.arange(n_tiles, dtype=jnp.int32) * tm
    n_valid = ends[-1] // tm
    tile_valid = (starts < ends[-1]).astype(jnp.int32)
    tile_blk = jnp.minimum(jnp.arange(n_tiles, dtype=jnp.int32), n_valid - 1)
    tile_exp = jnp.sum((tile_blk[:, None] * tm >= ends[None, :]).astype(jnp.int32), axis=1)
    n_rows = n_tiles * tm
    seg_base = jnp.concatenate([offs + counts, ends[-1:]])
    seg_len = jnp.concatenate([padded - counts, n_rows - ends[-1:]])
    seg_end = jnp.cumsum(seg_len)
    k = jnp.arange(n_rows - 2 * n, dtype=jnp.int32)
    seg = jnp.sum((k[:, None] >= seg_end[None, :]).astype(jnp.int32), axis=1)
    pad_pos = seg_base[seg] + k - (seg_end - seg_len)[seg]
    xs = _scatter(pos, pad_pos, h, n_rows)
    ys = _moe_experts(tile_blk, tile_exp, tile_valid, xs, wg, wu, wd)
    return _combine(pos, x2, meta, ys)


def _mixer(x2, batch, seq, rel_bias, bias_tiles, norm_g, w_in, conv_w, conv_b, igate_b, fgate_b, mlstm_norm_g,
           q_norm_g, k_norm_g, w_branch_a, w_branch_b, w_out):
    sizes = (M_WIDTH, M_WIDTH, M_WIDTH, M_WIDTH, M_HEADS, M_HEADS, A_WIDTH, A_WIDTH, A_WIDTH, D_MODEL, D_MODEL)
    cuts = [0]
    for s in sizes:
        cuts.append(cuts[-1] + s)
    mq, mk, mv, mo, mi, mf, aq, ak, av, ga, gb = [w_in[:, cuts[i]:cuts[i + 1]] for i in range(len(sizes))]
    wn = jnp.concatenate([ga, gb, mq, mk, mv, mo, ak], axis=1).astype(BF16)
    wt = jnp.concatenate([aq, av], axis=1).T.astype(BF16)
    wgate = jnp.zeros((D_MODEL, LANES), F32).at[:, :M_HEADS].set(mi).at[:, M_HEADS:2 * M_HEADS].set(mf).astype(BF16)
    gate_b = jnp.zeros((1, LANES), F32).at[0, :M_HEADS].set(igate_b).at[0, M_HEADS:2 * M_HEADS].set(fgate_b)
    z, zt, gates = _in_proj(x2, norm_g.reshape(1, D_MODEL), wn, wt, wgate)
    ha = _mlstm(z, gates, conv_w, conv_b.reshape(1, -1), gate_b, mlstm_norm_g.reshape(1, -1), batch, seq)
    qg_col = jnp.tile(q_norm_g, 2).reshape(2 * A_HEAD_DIM, 1)
    kg_row = jnp.tile(k_norm_g, 2).reshape(1, 2 * A_HEAD_DIM)
    ob = _moba(z, zt, bias_tiles, _moba_logit_bound(rel_bias, q_norm_g, k_norm_g), qg_col, kg_row, batch, seq)
    return _merge(x2, ha, ob, z, w_branch_a.astype(BF16), w_branch_b.astype(BF16), w_out.astype(BF16))


def kernel(x, rel_bias, mix_norm_g, w_in, conv_w, conv_b, igate_b, fgate_b, mlstm_norm_g, q_norm_g, k_norm_g,
           w_branch_a, w_branch_b, w_out, ffn_norm_g, dense_w_gate, dense_w_up, dense_w_down, router_w,
           expert_w_gate, expert_w_up, expert_w_down):
    batch, seq, d = x.shape
    depth = w_in.shape[0]
    assert d == D_MODEL and seq % MLSTM_T == 0 and seq % MOBA_BLOCK == 0
    assert (batch * seq) % MOE_TM == 0
    x2 = x.reshape(batch * seq, d)
    bias_tiles = _moba_bias_tiles(rel_bias)
    for layer in range(depth):
        x2 = _mixer(x2, batch, seq, rel_bias, bias_tiles, mix_norm_g[layer], w_in[layer], conv_w[layer], conv_b[layer],
                    igate_b[layer], fgate_b[layer], mlstm_norm_g[layer], q_norm_g[layer], k_norm_g[layer],
                    w_branch_a[layer], w_branch_b[layer], w_out[layer])
        g = ffn_norm_g[layer].reshape(1, d)
        j = layer // 2
        if layer % 2 == 0:
            x2 = _ffn(x2, g, dense_w_gate[j].astype(BF16), dense_w_up[j].astype(BF16), dense_w_down[j].astype(BF16))
        else:
            x2 = _moe(x2, g, router_w[j], expert_w_gate[j].astype(BF16), expert_w_up[j].astype(BF16),
                      expert_w_down[j].astype(BF16))
    return x2.reshape(batch, seq, d)
```

```python
import functools
import math

import jax
import jax.numpy as jnp
from jax import lax
from jax.experimental import pallas as pl
from jax.experimental.pallas import tpu as pltpu

F32 = jnp.float32
BF16 = jnp.bfloat16
HIGHEST = lax.Precision.HIGHEST

D_MODEL = 1024
M_HEADS = 4
M_HEAD_DIM = 128
M_WIDTH = M_HEADS * M_HEAD_DIM
CONV_WIDTH = 4
A_HEADS = 8
A_HEAD_DIM = 64
A_WIDTH = A_HEADS * A_HEAD_DIM
MOBA_BLOCK = 256
MOBA_TOPK = 3
REL_BUCKETS = 32
REL_MAX_DIST = 1024
N_EXPERTS = 8
EPS = 1e-6

LANES = 128
SUBLANES = 8
NEG = -1e30
VMEM_LIMIT = 56 * 1024 * 1024

C_GA = 0
C_GB = D_MODEL
C_MQ = 2 * D_MODEL
C_MK = C_MQ + M_WIDTH
C_MV = C_MK + M_WIDTH
C_MO = C_MV + M_WIDTH
C_AK = C_MO + M_WIDTH
NAT_WIDTH = C_AK + A_WIDTH
N_BIAS_TILES = 6

PROJ_TM = 512
MLSTM_T = 512
MLSTM_CHUNK = 128
MERGE_TM = 512
FFN_TM = 512
FFN_CHUNK = 256
ROUTER_TM = 512
MOE_TM = 512
MOE_TF = 1792
SCATTER_R = 256


def _params(*sem):
    return pltpu.CompilerParams(dimension_semantics=sem, vmem_limit_bytes=VMEM_LIMIT)


def _const_spec(shape):
    nd = len(shape)
    return pl.BlockSpec(shape, lambda *_: (0,) * nd)


def _rms(x, g):
    return x * lax.rsqrt(jnp.mean(x * x, axis=-1, keepdims=True) + EPS) * g


ROW_SLAB = 128


def _in_proj_kernel(x_ref, g_ref, wn_ref, wt_ref, wg_ref, cw_ref, cb_ref, qg_ref, kg_ref, z_ref, zt_ref, gt_ref,
                    hn_s, qk_buf, k_buf, *, tiles_per_seq):
    tm = x_ref.shape[0]
    DA = A_HEAD_DIM

    @pl.when(pl.program_id(0) % tiles_per_seq == 0)
    def _sequence_start():
        qk_buf[0:SUBLANES, :] = jnp.zeros((SUBLANES, 2 * M_WIDTH), F32)

    hn_s[...] = _rms(x_ref[...], g_ref[...]).astype(BF16)

    def natural(c):
        return jnp.dot(hn_s[...], wn_ref[:, c:c + 512], preferred_element_type=F32)

    def transposed(c):
        return lax.dot_general(wt_ref[c:c + 256, :], hn_s[...], (((1,), (1,)), ((), ())),
                               preferred_element_type=F32)

    qk_buf[SUBLANES:SUBLANES + tm, 0:M_WIDTH] = natural(C_MQ)
    qk_buf[SUBLANES:SUBLANES + tm, M_WIDTH:] = natural(C_MK)
    for r in range(0, tm, ROW_SLAB):
        acc = cb_ref[...] + cw_ref[CONV_WIDTH - 1:CONV_WIDTH, :] * qk_buf[SUBLANES + r:SUBLANES + r + ROW_SLAB, :]
        for j in range(CONV_WIDTH - 1):
            off = SUBLANES - (CONV_WIDTH - 1) + j + r
            acc = acc + cw_ref[j:j + 1, :] * qk_buf[off:off + ROW_SLAB, :]
        qk = acc * jax.nn.sigmoid(acc)
        z_ref[r:r + ROW_SLAB, C_MQ:C_MQ + M_WIDTH] = qk[:, :M_WIDTH].astype(BF16)
        z_ref[r:r + ROW_SLAB, C_MK:C_MK + M_WIDTH] = (qk[:, M_WIDTH:] * (M_HEAD_DIM ** -0.5)).astype(BF16)
    qk_buf[0:SUBLANES, :] = qk_buf[tm:tm + SUBLANES, :]

    k_buf[...] = natural(C_AK)
    head0 = lax.broadcasted_iota(jnp.int32, (ROW_SLAB, LANES), 1) < DA
    for r in range(0, tm, ROW_SLAB):
        for t in range(A_WIDTH // LANES):
            cols = slice(t * LANES, (t + 1) * LANES)
            kt = k_buf[r:r + ROW_SLAB, cols]
            k2 = kt * kt
            s0 = jnp.sum(jnp.where(head0, k2, 0.0), axis=-1, keepdims=True)
            s1 = jnp.sum(jnp.where(head0, 0.0, k2), axis=-1, keepdims=True)
            kn = kt * jnp.where(head0, lax.rsqrt(s0 / DA + EPS), lax.rsqrt(s1 / DA + EPS)) * kg_ref[:, cols]
            z_ref[r:r + ROW_SLAB, C_AK + t * LANES:C_AK + (t + 1) * LANES] = kn.astype(BF16)

    for c in range(0, A_WIDTH, 256):
        q3 = transposed(c).reshape(256 // DA, DA, tm)
        inv = lax.rsqrt(jnp.mean(q3 * q3, axis=1, keepdims=True) + EPS)
        qn = q3 * inv * (qg_ref[...] * (DA ** -0.5 * LOG2E))
        zt_ref[c:c + 256, :] = qn.reshape(256, tm).astype(BF16)
    for c in (C_GA, C_GA + 512, C_GB, C_GB + 512, C_MV, C_MO):
        z_ref[:, c:c + 512] = natural(c).astype(BF16)
    for c in range(A_WIDTH, 2 * A_WIDTH, 256):
        zt_ref[c:c + 256, :] = transposed(c).astype(BF16)
    gt_ref[...] = jnp.dot(hn_s[...], wg_ref[...], preferred_element_type=F32)


def _in_proj(x2, g, wn, wt, wg, conv_w, conv_b, qg, kg, tiles_per_seq):
    n = x2.shape[0]
    tm = PROJ_TM
    return pl.pallas_call(
        functools.partial(_in_proj_kernel, tiles_per_seq=tiles_per_seq),
        grid=(n // tm,),
        in_specs=[
            pl.BlockSpec((tm, D_MODEL), lambda i: (i, 0)),
            _const_spec((1, D_MODEL)),
            _const_spec((D_MODEL, NAT_WIDTH)),
            _const_spec((2 * A_WIDTH, D_MODEL)),
            _const_spec((D_MODEL, LANES)),
            _const_spec((CONV_WIDTH, 2 * M_WIDTH)),
            _const_spec((1, 2 * M_WIDTH)),
            _const_spec((1, A_HEAD_DIM, 1)),
            _const_spec((1, A_WIDTH)),
        ],
        out_specs=[
            pl.BlockSpec((tm, NAT_WIDTH), lambda i: (i, 0)),
            pl.BlockSpec((2 * A_WIDTH, tm), lambda i: (0, i)),
            pl.BlockSpec((tm, LANES), lambda i: (i, 0)),
        ],
        out_shape=[
            jax.ShapeDtypeStruct((n, NAT_WIDTH), BF16),
            jax.ShapeDtypeStruct((2 * A_WIDTH, n), BF16),
            jax.ShapeDtypeStruct((n, LANES), F32),
        ],
        scratch_shapes=[pltpu.VMEM((tm, D_MODEL), BF16), pltpu.VMEM((tm + SUBLANES, 2 * M_WIDTH), F32),
                        pltpu.VMEM((tm, A_WIDTH), F32)],
        compiler_params=_params("arbitrary"),
        name="in_proj",
    )(x2, g, wn, wt, wg, conv_w, conv_b, qg, kg)


def _mlstm_kernel(zq_ref, zk_ref, zv_ref, zo_ref, gt_ref, gb_ref, ng_ref, o_ref, c_s, m_s, *, t_blk, chunk):
    L = chunk
    DH = M_HEAD_DIM
    assert L == LANES and DH == LANES

    @pl.when(pl.program_id(1) == 0)
    def _init():
        c_s[...] = jnp.zeros_like(c_s)
        m_s[...] = jnp.zeros_like(m_s)

    row = lax.broadcasted_iota(jnp.int32, (L, L), 0)
    col = lax.broadcasted_iota(jnp.int32, (L, L), 1)
    causal = col <= row
    tri = causal.astype(F32)
    lane = lax.broadcasted_iota(jnp.int32, (L, LANES), 1)
    ones_blk = jnp.ones((L, LANES), BF16)

    def chunk_body(c, carry):
        r0 = c * L
        g_pre = gt_ref[pl.ds(r0, L), :] + gb_ref[...]
        log_f = jnp.minimum(g_pre, 0.0) - jnp.log1p(jnp.exp(-jnp.abs(g_pre)))
        bcum = jnp.dot(tri, log_f, precision=HIGHEST, preferred_element_type=F32)
        gm = jnp.where(lane < M_HEADS, g_pre, bcum)
        gm_t = gm.T
        c_all = [c_s[h] for h in range(M_HEADS)]
        m_all = m_s[...]
        heads = range(M_HEADS)
        hs = [slice(h * DH, (h + 1) * DH) for h in heads]
        qh = [zq_ref[pl.ds(r0, L), hs[h]] for h in heads]
        kb = [zk_ref[pl.ds(r0, L), hs[h]] for h in heads]
        kf = [k.astype(F32) for k in kb]
        v_ext = [jnp.concatenate([zv_ref[pl.ds(r0, L), hs[h]], ones_blk], axis=1) for h in heads]
        qk = [lax.dot_general(qh[h], kb[h], (((1,), (1,)), ((), ())), preferred_element_type=F32)
              for h in heads]
        inter = [jnp.dot(qh[h], c_all[h].astype(BF16), preferred_element_type=F32) for h in heads]
        mt, w_state, s_bf, kw_t, decay, m_new = [], [], [], [], [], []
        for h in heads:
            bb = jnp.broadcast_to(gm[:, M_HEADS + h:M_HEADS + h + 1], (L, LANES))
            ii = jnp.broadcast_to(gm[:, h:h + 1], (L, LANES))
            bb_row = gm_t[M_HEADS + h:M_HEADS + h + 1, :]
            ii_row = gm_t[h:h + 1, :]
            m_old = m_all[h:h + 1, :]
            dlog = jnp.where(causal, bb - (bb_row - ii_row), -jnp.inf)
            a = bb + m_old
            mt.append(jnp.maximum(a, jnp.broadcast_to(jnp.max(dlog, axis=-1, keepdims=True), (L, LANES))))
            w_state.append(jnp.exp(a - mt[h]))
            s_bf.append((qk[h] * jnp.exp(dlog - mt[h])).astype(BF16))
            b_last = bb[L - 1:L, :]
            g = b_last - bb + ii
            m_new.append(jnp.maximum(b_last + m_old, jnp.max(g, axis=0, keepdims=True)))
            decay.append(jnp.exp(b_last + m_old - m_new[h]))
            kw_t.append((kf[h] * jnp.exp(g - m_new[h])).T.astype(BF16))
        intra = [jnp.dot(s_bf[h], v_ext[h], preferred_element_type=F32) for h in heads]
        upd = [jnp.dot(kw_t[h], v_ext[h], preferred_element_type=F32) for h in heads]
        for h in heads:
            num = intra[h][:, :DH] + w_state[h] * inter[h][:, :DH]
            den = intra[h][:, DH:] + w_state[h] * inter[h][:, DH:]
            h_t = num / jnp.maximum(jnp.abs(den), jnp.exp(-mt[h]))
            hc = jax.nn.sigmoid(zo_ref[pl.ds(r0, L), hs[h]].astype(F32)) * h_t
            o_ref[pl.ds(r0, L), hs[h]] = _rms(hc, ng_ref[:, hs[h]]).astype(BF16)
        for h in heads:
            c_s[h] = jnp.concatenate([decay[h], decay[h]], axis=1) * c_all[h] + upd[h]
            m_s[h:h + 1, :] = m_new[h]
        return carry

    for c in range(t_blk // L):
        chunk_body(c, 0)


def _mlstm(z, gates, gate_b, norm_g, batch, seq):
    n = batch * seq
    t = MLSTM_T
    nt = seq // t
    row_blk = lambda b, s: b * nt + s
    zspec = lambda cb: pl.BlockSpec((t, M_WIDTH), lambda b, s: (row_blk(b, s), cb))
    return pl.pallas_call(
        functools.partial(_mlstm_kernel, t_blk=t, chunk=MLSTM_CHUNK),
        grid=(batch, nt),
        in_specs=[
            zspec(C_MQ // M_WIDTH), zspec(C_MK // M_WIDTH), zspec(C_MV // M_WIDTH), zspec(C_MO // M_WIDTH),
            pl.BlockSpec((t, LANES), lambda b, s: (row_blk(b, s), 0)),
            _const_spec((1, LANES)),
            _const_spec((1, M_WIDTH)),
        ],
        out_specs=pl.BlockSpec((t, M_WIDTH), lambda b, s: (row_blk(b, s), 0)),
        out_shape=jax.ShapeDtypeStruct((n, M_WIDTH), BF16),
        scratch_shapes=[
            pltpu.VMEM((M_HEADS, M_HEAD_DIM, M_HEAD_DIM + LANES), F32),
            pltpu.VMEM((SUBLANES, LANES), F32),
        ],
        compiler_params=_params("arbitrary", "arbitrary"),
        name="mlstm",
    )(z, z, z, z, gates, gate_b, norm_g)


MOBA_V_ROWS = A_HEAD_DIM + 16
MOBA_ROWB_ROWS = 24
MOBA_GROUP = 34
LOG2E = math.log2(math.e)
FAST_SOFTMAX_MIN_DENOM = 1e-25


def _moba_items(n_blocks):
    items = []
    for qb in range(n_blocks):
        items.append((qb, qb, 0, 0))
        for j in range(qb):
            items.append((qb, j, min(qb - j, N_BIAS_TILES - 1), j + 1))
    n_groups = -(-len(items) // MOBA_GROUP)
    noop = (n_blocks - 1, 0, 0, MOBA_ROWB_ROWS - 1)
    items += [noop] * (n_groups * MOBA_GROUP + 2 - len(items))
    return n_groups, [jnp.asarray([it[c] for it in items], jnp.int32) for c in range(4)]


def _moba_kernel(it_q, it_blk, it_tile, it_row, qt_ref, k_ref, vt_ref, bias_ref, bound_ref, o_ref,
                 vt_s, kmean_s, qh_s, rowb_s, acc_s, st_a, st_b, p_a, p_b, *, n_blocks, n_groups):
    BS = MOBA_BLOCK
    DA = A_HEAD_DIM
    seq = n_blocks * BS

    ones_row = jnp.where(lax.broadcasted_iota(jnp.int32, (16, BS), 0) == 0, 1.0, 0.0).astype(BF16)
    for j in range(n_blocks):
        kmean_s[j:j + 1, :] = jnp.mean(k_ref[j * BS:(j + 1) * BS, :].astype(F32), axis=0, keepdims=True)
        for h in range(2):
            vt_s[j, h, 0:DA, :] = vt_ref[h * DA:(h + 1) * DA, j * BS:(j + 1) * BS]
            vt_s[j, h, DA:MOBA_V_ROWS, :] = ones_row

    qn = qt_ref[...].astype(F32)
    head0_q = lax.broadcasted_iota(jnp.int32, qn.shape, 0) < DA

    blk = lax.broadcasted_iota(jnp.int32, (n_blocks, BS), 0)
    lane_m = lax.broadcasted_iota(jnp.int32, (n_blocks, LANES), 1)
    kmean = kmean_s[...]
    kmean2 = jnp.concatenate([jnp.where(lane_m < DA, kmean, 0.0), jnp.where(lane_m < DA, 0.0, kmean)], axis=0)
    k_hi = kmean2.astype(BF16)
    k_mid = (kmean2 - k_hi.astype(F32)).astype(BF16)
    k_lo = (kmean2 - k_hi.astype(F32) - k_mid.astype(F32)).astype(BF16)
    split = jnp.dot(jnp.concatenate([k_hi, k_mid, k_lo], axis=0), qt_ref[...], preferred_element_type=F32)
    gates = split[0:2 * n_blocks] + split[2 * n_blocks:4 * n_blocks] + split[4 * n_blocks:6 * n_blocks]
    for h in range(2):
        bound = bound_ref[h, :, 0:1]
        hmask_q = head0_q if h == 0 else jnp.logical_not(head0_q)
        qh = jnp.where(hmask_q, qn, 0.0).astype(BF16)
        for qb in range(n_blocks):
            cols = slice(qb * BS, (qb + 1) * BS)
            past = blk < qb
            gate = jnp.where(past, gates[h * n_blocks:(h + 1) * n_blocks, cols], -jnp.inf)
            rank = jnp.zeros((n_blocks, BS), jnp.int32)
            for j2 in range(qb):
                other = gate[j2:j2 + 1, :]
                beats = (other > gate) | ((other == gate) & (j2 < blk))
                rank = rank + beats.astype(jnp.int32)
            selb = jnp.where(past & (rank < MOBA_TOPK), 0.0, NEG) - bound
            qh_s[qb, h] = qh[:, cols]
            rowb_s[qb, h, 0:1, :] = jnp.broadcast_to(-bound, (1, BS))
            rowb_s[qb, h, 1:n_blocks + 1, :] = selb
            rowb_s[qb, h, n_blocks + 1:, :] = jnp.full((MOBA_ROWB_ROWS - n_blocks - 1, BS), NEG, F32)
    acc_s[...] = jnp.zeros_like(acc_s)

    def scores(h, qb, j, tile, row):
        kj = k_ref[pl.ds(pl.multiple_of(j * BS, BS), BS), :]
        st = jnp.dot(kj, qh_s[qb, h], preferred_element_type=F32)
        return st + bias_ref[h, tile] + rowb_s[qb, h, pl.ds(row, 1), :]

    def pv(h, j, p):
        return jnp.dot(vt_s[j, h], p, preferred_element_type=F32)

    def stage_scores(i, st_ref):
        for h in range(2):
            st_ref[h] = scores(h, it_q[i], it_blk[i], it_tile[i], it_row[i])

    def stage_exp(st_ref, p_ref):
        for h in range(2):
            p_ref[h] = jnp.exp2(st_ref[h]).astype(BF16)

    def stage_pv(i, p_ref):
        for h in range(2):
            acc_s[it_q[i], h] += pv(h, it_blk[i], p_ref[h])

    def group(m, carry):
        for u in range(0, MOBA_GROUP, 2):
            i = MOBA_GROUP * m + u
            stage_pv(i, p_a)
            stage_scores(i + 2, st_a)
            stage_exp(st_b, p_b)
            stage_pv(i + 1, p_b)
            stage_scores(i + 3, st_b)
            stage_exp(st_a, p_a)
        return carry

    stage_scores(0, st_a)
    stage_exp(st_a, p_a)
    stage_scores(1, st_b)
    lax.fori_loop(0, n_groups, group, 0)

    def finish(qb, l_min):
        outs = []
        for h in range(2):
            acc = acc_s[qb, h]
            l = acc[DA:DA + 1, :]
            outs.append(acc[0:DA, :] / l)
            l_min = jnp.minimum(l_min, jnp.min(l))
        o_ref[pl.ds(pl.multiple_of(qb * BS, BS), BS), :] = jnp.concatenate(outs, axis=0).T.astype(BF16)
        return l_min

    l_min = lax.fori_loop(0, n_blocks, finish, jnp.float32(jnp.inf))

    def online(qb, carry):
        state = []
        for h in range(2):
            st = scores(h, qb, qb, 0, 0)
            m = jnp.max(st, axis=0, keepdims=True)
            p = jnp.exp2(st - m)
            state += [m, jnp.sum(p, axis=0, keepdims=True), pv(h, qb, p.astype(BF16))[0:DA, :]]

        def past_body(j, state):
            tile = jnp.minimum(qb - j, N_BIAS_TILES - 1)
            new = []
            for h in range(2):
                m_old, l_old, acc_old = state[3 * h:3 * h + 3]
                st = scores(h, qb, j, tile, j + 1)
                m_new = jnp.maximum(m_old, jnp.max(st, axis=0, keepdims=True))
                alpha = jnp.exp2(m_old - m_new)
                p = jnp.exp2(st - m_new)
                new += [m_new, alpha * l_old + jnp.sum(p, axis=0, keepdims=True),
                        alpha * acc_old + pv(h, j, p.astype(BF16))[0:DA, :]]
            return tuple(new)

        state = lax.fori_loop(0, qb, past_body, tuple(state))
        outs = [state[3 * h + 2] / state[3 * h + 1] for h in range(2)]
        o_ref[pl.ds(pl.multiple_of(qb * BS, BS), BS), :] = jnp.concatenate(outs, axis=0).T.astype(BF16)
        return carry

    @pl.when(l_min < FAST_SOFTMAX_MIN_DENOM)
    def _redo():
        lax.fori_loop(0, n_blocks, online, 0)


def _moba_logit_bound(rel_bias, q_norm_g, k_norm_g):
    qk = A_HEAD_DIM * jnp.max(jnp.abs(q_norm_g)) * jnp.max(jnp.abs(k_norm_g)) * (A_HEAD_DIM ** -0.5) * 1.02
    b = (qk + jnp.max(rel_bias, axis=0)) * LOG2E
    return jnp.broadcast_to(b[:, None, None], (A_HEADS, 1, LANES)).astype(F32)


def _moba(z, zt, bias_tiles, bound, batch, seq):
    n = batch * seq
    nb = seq // MOBA_BLOCK
    bs = MOBA_BLOCK
    hp = A_HEADS // 2
    w2 = 2 * A_HEAD_DIM
    assert nb + 2 <= MOBA_ROWB_ROWS
    n_groups, items = _moba_items(nb)
    grid_spec = pltpu.PrefetchScalarGridSpec(
        num_scalar_prefetch=len(items),
        grid=(batch, hp),
        in_specs=[
            pl.BlockSpec((w2, seq), lambda b, p, *_: (p, b)),
            pl.BlockSpec((seq, w2), lambda b, p, *_: (b, C_AK // w2 + p)),
            pl.BlockSpec((w2, seq), lambda b, p, *_: (hp + p, b)),
            pl.BlockSpec((2, N_BIAS_TILES, bs, bs), lambda b, p, *_: (p, 0, 0, 0)),
            pl.BlockSpec((2, 1, LANES), lambda b, p, *_: (p, 0, 0)),
        ],
        out_specs=pl.BlockSpec((seq, w2), lambda b, p, *_: (b, p)),
        scratch_shapes=[
            pltpu.VMEM((nb, 2, MOBA_V_ROWS, bs), BF16),
            pltpu.VMEM((nb, w2), F32),
            pltpu.VMEM((nb, 2, w2, bs), BF16),
            pltpu.VMEM((nb, 2, MOBA_ROWB_ROWS, bs), F32),
            pltpu.VMEM((nb, 2, MOBA_V_ROWS, bs), F32),
            pltpu.VMEM((2, bs, bs), F32),
            pltpu.VMEM((2, bs, bs), F32),
            pltpu.VMEM((2, bs, bs), BF16),
            pltpu.VMEM((2, bs, bs), BF16),
        ],
    )
    return pl.pallas_call(
        functools.partial(_moba_kernel, n_blocks=nb, n_groups=n_groups),
        grid_spec=grid_spec,
        out_shape=jax.ShapeDtypeStruct((n, A_WIDTH), BF16),
        compiler_params=_params("arbitrary", "arbitrary"),
        name="moba",
    )(*items, zt, z, zt, bias_tiles, bound)


def _t5_bucket(dist):
    n = jnp.maximum(dist, 0)
    max_exact = REL_BUCKETS // 2
    log_ratio = jnp.log(jnp.maximum(n, max_exact).astype(F32) / max_exact) / math.log(REL_MAX_DIST / max_exact)
    large = max_exact + (log_ratio * (REL_BUCKETS - max_exact)).astype(jnp.int32)
    large = jnp.minimum(large, REL_BUCKETS - 1)
    return jnp.where(n < max_exact, n, large)


def _bias_tiles_kernel(rb_ref, bucket_ref, o_ref):
    bucket = bucket_ref[0]
    hit = [bucket == b for b in range(REL_BUCKETS)]
    tk = lax.broadcasted_iota(jnp.int32, bucket.shape, 0)
    tq = lax.broadcasted_iota(jnp.int32, bucket.shape, 1)
    masked = (tk > tq) & (pl.program_id(0) == 0)
    for h in range(A_HEADS):
        acc = jnp.zeros(bucket.shape, F32)
        for b in range(REL_BUCKETS):
            acc = jnp.where(hit[b], rb_ref[b, h], acc)
        o_ref[h, 0] = jnp.where(masked, NEG, acc * LOG2E)


def _moba_bias_tiles(rel_bias):
    assert (N_BIAS_TILES - 1) * MOBA_BLOCK - (MOBA_BLOCK - 1) >= REL_MAX_DIST
    tk = jnp.arange(MOBA_BLOCK)[None, :, None]
    tq = jnp.arange(MOBA_BLOCK)[None, None, :]
    diff = jnp.arange(N_BIAS_TILES)[:, None, None]
    bucket = _t5_bucket(diff * MOBA_BLOCK + tq - tk).astype(jnp.int32)
    return pl.pallas_call(
        _bias_tiles_kernel,
        grid=(N_BIAS_TILES,),
        in_specs=[
            pl.BlockSpec(memory_space=pltpu.SMEM),
            pl.BlockSpec((1, MOBA_BLOCK, MOBA_BLOCK), lambda t: (t, 0, 0)),
        ],
        out_specs=pl.BlockSpec((A_HEADS, 1, MOBA_BLOCK, MOBA_BLOCK), lambda t: (0, t, 0, 0)),
        out_shape=jax.ShapeDtypeStruct((A_HEADS, N_BIAS_TILES, MOBA_BLOCK, MOBA_BLOCK), F32),
        compiler_params=_params("arbitrary"),
        name="moba_bias_tiles",
    )(rel_bias.astype(F32), bucket)


def _merge_kernel(x_ref, ha_ref, ob_ref, ga_ref, gb_ref, wa_ref, wb_ref, wo_ref, o_ref):
    ya = jnp.dot(ha_ref[...], wa_ref[...], preferred_element_type=F32)
    yb = jnp.dot(ob_ref[...], wb_ref[...], preferred_element_type=F32)
    y = jax.nn.sigmoid(ga_ref[...].astype(F32)) * ya + jax.nn.sigmoid(gb_ref[...].astype(F32)) * yb
    o_ref[...] = x_ref[...] + jnp.dot(y.astype(BF16), wo_ref[...], preferred_element_type=F32)


def _merge(x2, ha, ob, z, wa, wb, wo):
    n = x2.shape[0]
    tm = MERGE_TM
    return pl.pallas_call(
        _merge_kernel,
        grid=(n // tm,),
        in_specs=[
            pl.BlockSpec((tm, D_MODEL), lambda i: (i, 0)),
            pl.BlockSpec((tm, M_WIDTH), lambda i: (i, 0)),
            pl.BlockSpec((tm, A_WIDTH), lambda i: (i, 0)),
            pl.BlockSpec((tm, D_MODEL), lambda i: (i, C_GA // D_MODEL)),
            pl.BlockSpec((tm, D_MODEL), lambda i: (i, C_GB // D_MODEL)),
            _const_spec((M_WIDTH, D_MODEL)),
            _const_spec((A_WIDTH, D_MODEL)),
            _const_spec((D_MODEL, D_MODEL)),
        ],
        out_specs=pl.BlockSpec((tm, D_MODEL), lambda i: (i, 0)),
        out_shape=jax.ShapeDtypeStruct((n, D_MODEL), F32),
        compiler_params=_params("arbitrary"),
        name="merge",
    )(x2, ha, ob, z, z, wa, wb, wo)


def _ffn_kernel(x_ref, g_ref, wg_ref, wu_ref, wd_ref, o_ref, *, d_ff):
    x = x_ref[...]
    hn = _rms(x, g_ref[...]).astype(BF16)
    acc = x
    for c in range(0, d_ff, FFN_CHUNK):
        g = jnp.dot(hn, wg_ref[:, c:c + FFN_CHUNK], preferred_element_type=F32)
        u = jnp.dot(hn, wu_ref[:, c:c + FFN_CHUNK], preferred_element_type=F32)
        a = (g * jax.nn.sigmoid(g) * u).astype(BF16)
        acc = acc + jnp.dot(a, wd_ref[c:c + FFN_CHUNK, :], preferred_element_type=F32)
    o_ref[...] = acc


def _ffn(x2, g, wg, wu, wd):
    n = x2.shape[0]
    d_ff = wg.shape[1]
    assert d_ff % FFN_CHUNK == 0
    tm = FFN_TM
    return pl.pallas_call(
        functools.partial(_ffn_kernel, d_ff=d_ff),
        grid=(n // tm,),
        in_specs=[
            pl.BlockSpec((tm, D_MODEL), lambda i: (i, 0)),
            _const_spec((1, D_MODEL)),
            _const_spec((D_MODEL, d_ff)),
            _const_spec((D_MODEL, d_ff)),
            _const_spec((d_ff, D_MODEL)),
        ],
        out_specs=pl.BlockSpec((tm, D_MODEL), lambda i: (i, 0)),
        out_shape=jax.ShapeDtypeStruct((n, D_MODEL), F32),
        compiler_params=_params("arbitrary"),
        name="ffn_dense",
    )(x2, g, wg, wu, wd)


META_W0, META_W1, META_E0, META_E1, META_R0, META_R1 = range(6)


def _router_kernel(x_ref, g_ref, rw_ref, h_ref, meta_ref, cnt_ref, carry_s):
    @pl.when(pl.program_id(0) == 0)
    def _init():
        carry_s[...] = jnp.zeros_like(carry_s)

    h = _rms(x_ref[...], g_ref[...])
    h_ref[...] = h
    tm = h.shape[0]
    logits = jnp.dot(h, rw_ref[...], precision=HIGHEST, preferred_element_type=F32)
    lane = lax.broadcasted_iota(jnp.int32, (tm, LANES), 1)
    lg = jnp.where(lane < N_EXPERTS, logits, -jnp.inf)
    m1 = jnp.max(lg, axis=-1, keepdims=True)
    i1 = jnp.min(jnp.where(lg == m1, lane, LANES), axis=-1, keepdims=True)
    lg2 = jnp.where(lane == i1, -jnp.inf, lg)
    m2 = jnp.max(lg2, axis=-1, keepdims=True)
    i2 = jnp.min(jnp.where(lg2 == m2, lane, LANES), axis=-1, keepdims=True)
    e = jnp.exp(m2 - m1)
    w1 = 1.0 / (1.0 + e)
    w2 = e / (1.0 + e)
    hit1 = lane == i1
    hit2 = lane == i2
    onehot = jnp.where(hit1 | hit2, 1.0, 0.0)
    r = lax.broadcasted_iota(jnp.int32, (tm, tm), 0)
    c = lax.broadcasted_iota(jnp.int32, (tm, tm), 1)
    before = jnp.where(c < r, 1.0, 0.0).astype(BF16)
    carry = carry_s[0:1, :]
    pref = jnp.dot(before, onehot.astype(BF16), preferred_element_type=F32) + carry
    r1 = jnp.sum(jnp.where(hit1, pref, 0.0), axis=-1, keepdims=True)
    r2 = jnp.sum(jnp.where(hit2, pref, 0.0), axis=-1, keepdims=True)
    new_carry = carry + jnp.sum(onehot, axis=0, keepdims=True)
    carry_s[...] = jnp.broadcast_to(new_carry, carry_s.shape)
    cnt_ref[...] = jnp.broadcast_to(new_carry, cnt_ref.shape)
    meta = jnp.zeros((tm, LANES), F32)
    for idx, val in ((META_W0, w1), (META_W1, w2), (META_E0, i1.astype(F32)), (META_E1, i2.astype(F32)),
                     (META_R0, r1), (META_R1, r2)):
        meta = jnp.where(lane == idx, val, meta)
    meta_ref[...] = meta


def _router(x2, g, rw):
    n = x2.shape[0]
    tm = ROUTER_TM
    return pl.pallas_call(
        _router_kernel,
        grid=(n // tm,),
        in_specs=[
            pl.BlockSpec((tm, D_MODEL), lambda i: (i, 0)),
            _const_spec((1, D_MODEL)),
            _const_spec((D_MODEL, LANES)),
        ],
        out_specs=[
            pl.BlockSpec((tm, D_MODEL), lambda i: (i, 0)),
            pl.BlockSpec((tm, LANES), lambda i: (i, 0)),
            _const_spec((SUBLANES, LANES)),
        ],
        out_shape=[
            jax.ShapeDtypeStruct((n, D_MODEL), F32),
            jax.ShapeDtypeStruct((n, LANES), F32),
            jax.ShapeDtypeStruct((SUBLANES, LANES), F32),
        ],
        scratch_shapes=[pltpu.VMEM((SUBLANES, LANES), F32)],
        compiler_params=_params("arbitrary"),
        name="moe_router",
    )(x2, g, rw)


def _row_copy(src_ref, src_row, dst_ref, dst_row, sem):
    return pltpu.make_async_copy(src_ref.at[pl.ds(src_row, 1)], dst_ref.at[pl.ds(dst_row, 1)], sem)


def _scatter_kernel(pos_ref, pad_ref, h_ref, xs_ref, zero_s, sem, *, rows, pads):
    @pl.when(pl.program_id(0) == 0)
    def _init():
        zero_s[...] = jnp.zeros_like(zero_s)

    for r in range(rows):
        _row_copy(h_ref, r, xs_ref, pos_ref[0, 0, r], sem).start(priority=0)
        _row_copy(h_ref, r, xs_ref, pos_ref[0, 0, rows + r], sem).start(priority=1)
    for r in range(pads):
        _row_copy(zero_s, 0, xs_ref, pad_ref[0, 0, r], sem).start(priority=r % 2)
    for _ in range(2):
        pltpu.make_async_copy(h_ref, xs_ref.at[pl.ds(0, rows)], sem).wait()
    pltpu.make_async_copy(h_ref.at[pl.ds(0, pads)], xs_ref.at[pl.ds(0, pads)], sem).wait()


def _scatter(pos, pad_pos, h, n_rows):
    n = h.shape[0]
    r = SCATTER_R
    steps = n // r
    pads = pad_pos.shape[0] // steps
    assert pads * steps == pad_pos.shape[0] and 2 * n + pad_pos.shape[0] == n_rows
    return pl.pallas_call(
        functools.partial(_scatter_kernel, rows=r, pads=pads),
        grid=(steps,),
        in_specs=[
            pl.BlockSpec((1, 1, 2 * r), lambda i: (i, 0, 0), memory_space=pltpu.SMEM),
            pl.BlockSpec((1, 1, pads), lambda i: (i, 0, 0), memory_space=pltpu.SMEM),
            pl.BlockSpec((r, D_MODEL), lambda i: (i, 0)),
        ],
        out_specs=pl.BlockSpec(memory_space=pl.ANY),
        out_shape=jax.ShapeDtypeStruct((n_rows, D_MODEL), F32),
        scratch_shapes=[pltpu.VMEM((SUBLANES, D_MODEL), F32), pltpu.SemaphoreType.DMA(())],
        compiler_params=_params("arbitrary"),
        name="moe_scatter",
    )(pos, pad_pos.reshape(steps, 1, pads), h)


def _moe_kernel(blk_ref, exp_ref, valid_ref, xs_ref, wg_ref, wu_ref, wd_ref, y_ref, xb_s, acc_s, *, n_ff):
    del blk_ref, exp_ref
    i = pl.program_id(0)
    f = pl.program_id(1)

    @pl.when((valid_ref[i] == 0) & (f == n_ff - 1))
    def _idle():
        y_ref[...] = jnp.zeros_like(y_ref)

    @pl.when(valid_ref[i] == 1)
    def _tile():
        @pl.when(f == 0)
        def _cast():
            xb_s[...] = xs_ref[...].astype(BF16)

        xb = xb_s[...]
        d = None
        for c in range(0, wg_ref.shape[2], FFN_CHUNK):
            g = jnp.dot(xb, wg_ref[0, :, c:c + FFN_CHUNK], preferred_element_type=F32)
            u = jnp.dot(xb, wu_ref[0, :, c:c + FFN_CHUNK], preferred_element_type=F32)
            a = (g * jax.nn.sigmoid(g) * u).astype(BF16)
            dc = jnp.dot(a, wd_ref[0, c:c + FFN_CHUNK, :], preferred_element_type=F32)
            d = dc if d is None else d + dc

        @pl.when(f == 0)
        def _first():
            acc_s[...] = d

        @pl.when(f > 0)
        def _rest():
            acc_s[...] += d

        @pl.when(f == n_ff - 1)
        def _out():
            y_ref[...] = acc_s[...]


def _moe_experts(tile_blk, tile_exp, tile_valid, xs, wg, wu, wd):
    rows = xs.shape[0]
    d_ff = wg.shape[2]
    tm, tf = MOE_TM, MOE_TF
    n_ff = d_ff // tf
    assert rows % tm == 0 and d_ff % tf == 0
    grid_spec = pltpu.PrefetchScalarGridSpec(
        num_scalar_prefetch=3,
        grid=(rows // tm, n_ff),
        in_specs=[
            pl.BlockSpec((tm, D_MODEL), lambda i, f, blk, ex, va: (blk[i], 0)),
            pl.BlockSpec((1, D_MODEL, tf), lambda i, f, blk, ex, va: (ex[i], 0, f)),
            pl.BlockSpec((1, D_MODEL, tf), lambda i, f, blk, ex, va: (ex[i], 0, f)),
            pl.BlockSpec((1, tf, D_MODEL), lambda i, f, blk, ex, va: (ex[i], f, 0)),
        ],
        out_specs=pl.BlockSpec((tm, D_MODEL), lambda i, f, blk, ex, va: (i, 0)),
        scratch_shapes=[pltpu.VMEM((tm, D_MODEL), BF16), pltpu.VMEM((tm, D_MODEL), F32)],
    )
    return pl.pallas_call(
        functools.partial(_moe_kernel, n_ff=n_ff),
        grid_spec=grid_spec,
        out_shape=jax.ShapeDtypeStruct((rows, D_MODEL), F32),
        compiler_params=_params("arbitrary", "arbitrary"),
        name="moe_experts",
    )(tile_blk, tile_exp, tile_valid, xs, wg, wu, wd)


def _combine_kernel(pos_ref, x_ref, meta_ref, ys_ref, o_ref, ybuf, sem, *, rows):
    for r in range(rows):
        _row_copy(ys_ref, pos_ref[0, 0, r], ybuf, r, sem).start(priority=0)
        _row_copy(ys_ref, pos_ref[0, 0, rows + r], ybuf, rows + r, sem).start(priority=1)
    pltpu.make_async_copy(ys_ref.at[pl.ds(0, 2 * rows)], ybuf, sem).wait()
    meta = meta_ref[...]
    w0 = meta[:, META_W0:META_W0 + 1]
    w1 = meta[:, META_W1:META_W1 + 1]
    o_ref[...] = x_ref[...] + w0 * ybuf[0:rows, :] + w1 * ybuf[rows:2 * rows, :]


def _combine(pos, x2, meta, ys):
    n = x2.shape[0]
    r = SCATTER_R
    return pl.pallas_call(
        functools.partial(_combine_kernel, rows=r),
        grid=(n // r,),
        in_specs=[
            pl.BlockSpec((1, 1, 2 * r), lambda i: (i, 0, 0), memory_space=pltpu.SMEM),
            pl.BlockSpec((r, D_MODEL), lambda i: (i, 0)),
            pl.BlockSpec((r, LANES), lambda i: (i, 0)),
            pl.BlockSpec(memory_space=pl.ANY),
        ],
        out_specs=pl.BlockSpec((r, D_MODEL), lambda i: (i, 0)),
        out_shape=jax.ShapeDtypeStruct((n, D_MODEL), F32),
        scratch_shapes=[pltpu.VMEM((2 * r, D_MODEL), F32), pltpu.SemaphoreType.DMA(())],
        compiler_params=_params("arbitrary"),
        name="moe_combine",
    )(pos, x2, meta, ys)


def _moe(x2, g, router_w, wg, wu, wd):
    n = x2.shape[0]
    tm = MOE_TM
    rw = jnp.zeros((D_MODEL, LANES), F32).at[:, :N_EXPERTS].set(router_w)
    h, meta, cnt = _router(x2, g, rw)
    counts = cnt[0, :N_EXPERTS].astype(jnp.int32)
    padded = ((counts + tm - 1) // tm) * tm
    ends = jnp.cumsum(padded)
    offs = ends - padded
    n_tiles = (2 * n) // tm + N_EXPERTS
    e0 = meta[:, META_E0].astype(jnp.int32)
    e1 = meta[:, META_E1].astype(jnp.int32)
    pos0 = offs[e0] + meta[:, META_R0].astype(jnp.int32)
    pos1 = offs[e1] + meta[:, META_R1].astype(jnp.int32)
    r = SCATTER_R
    pos = jnp.concatenate([pos0.reshape(n // r, 1, r), pos1.reshape(n // r, 1, r)], axis=-1)
    starts = jnp.arange(n_tiles, dtype=jnp.int32) * tm
    n_valid = ends[-1] // tm
    tile_valid = (starts < ends[-1]).astype(jnp.int32)
    tile_blk = jnp.minimum(jnp.arange(n_tiles, dtype=jnp.int32), n_valid - 1)
    tile_exp = jnp.sum((tile_blk[:, None] * tm >= ends[None, :]).astype(jnp.int32), axis=1)
    n_rows = n_tiles * tm
    seg_base = jnp.concatenate([offs + counts, ends[-1:]])
    seg_len = jnp.concatenate([padded - counts, n_rows - ends[-1:]])
    seg_end = jnp.cumsum(seg_len)
    k = jnp.arange(n_rows - 2 * n, dtype=jnp.int32)
    seg = jnp.sum((k[:, None] >= seg_end[None, :]).astype(jnp.int32), axis=1)
    pad_pos = seg_base[seg] + k - (seg_end - seg_len)[seg]
    xs = _scatter(pos, pad_pos, h, n_rows)
    ys = _moe_experts(tile_blk, tile_exp, tile_valid, xs, wg, wu, wd)
    return _combine(pos, x2, meta, ys)


def _mixer(x2, batch, seq, rel_bias, bias_tiles, norm_g, w_in, conv_w, conv_b, igate_b, fgate_b, mlstm_norm_g,
           q_norm_g, k_norm_g, w_branch_a, w_branch_b, w_out):
    sizes = (M_WIDTH, M_WIDTH, M_WIDTH, M_WIDTH, M_HEADS, M_HEADS, A_WIDTH, A_WIDTH, A_WIDTH, D_MODEL, D_MODEL)
    cuts = [0]
    for s in sizes:
        cuts.append(cuts[-1] + s)
    mq, mk, mv, mo, mi, mf, aq, ak, av, ga, gb = [w_in[:, cuts[i]:cuts[i + 1]] for i in range(len(sizes))]
    wn = jnp.concatenate([ga, gb, mq, mk, mv, mo, ak], axis=1).astype(BF16)
    wt = jnp.concatenate([aq, av], axis=1).T.astype(BF16)
    wgate = jnp.zeros((D_MODEL, LANES), F32).at[:, :M_HEADS].set(mi).at[:, M_HEADS:2 * M_HEADS].set(mf).astype(BF16)
    gate_b = jnp.zeros((1, LANES), F32).at[0, :M_HEADS].set(igate_b).at[0, M_HEADS:2 * M_HEADS].set(fgate_b)
    z, zt, gates = _in_proj(x2, norm_g.reshape(1, D_MODEL), wn, wt, wgate, conv_w, conv_b.reshape(1, -1),
                            q_norm_g.reshape(1, A_HEAD_DIM, 1), jnp.tile(k_norm_g, A_HEADS).reshape(1, A_WIDTH),
                            seq // PROJ_TM)
    ha = _mlstm(z, gates, gate_b, mlstm_norm_g.reshape(1, -1), batch, seq)
    ob = _moba(z, zt, bias_tiles, _moba_logit_bound(rel_bias, q_norm_g, k_norm_g), batch, seq)
    return _merge(x2, ha, ob, z, w_branch_a.astype(BF16), w_branch_b.astype(BF16), w_out.astype(BF16))


def kernel(x, rel_bias, mix_norm_g, w_in, conv_w, conv_b, igate_b, fgate_b, mlstm_norm_g, q_norm_g, k_norm_g,
           w_branch_a, w_branch_b, w_out, ffn_norm_g, dense_w_gate, dense_w_up, dense_w_down, router_w,
           expert_w_gate, expert_w_up, expert_w_down):
    batch, seq, d = x.shape
    depth = w_in.shape[0]
    assert d == D_MODEL and seq % MLSTM_T == 0 and seq % MOBA_BLOCK == 0 and seq % PROJ_TM == 0
    assert (batch * seq) % MOE_TM == 0
    x2 = x.reshape(batch * seq, d)
    bias_tiles = _moba_bias_tiles(rel_bias)
    for layer in range(depth):
        x2 = _mixer(x2, batch, seq, rel_bias, bias_tiles, mix_norm_g[layer], w_in[layer], conv_w[layer], conv_b[layer],
                    igate_b[layer], fgate_b[layer], mlstm_norm_g[layer], q_norm_g[layer], k_norm_g[layer],
                    w_branch_a[layer], w_branch_b[layer], w_out[layer])
        g = ffn_norm_g[layer].reshape(1, d)
        j = layer // 2
        if layer % 2 == 0:
            x2 = _ffn(x2, g, dense_w_gate[j].astype(BF16), dense_w_up[j].astype(BF16), dense_w_down[j].astype(BF16))
        else:
            x2 = _moe(x2, g, router_w[j], expert_w_gate[j].astype(BF16), expert_w_up[j].astype(BF16),
                      expert_w_down[j].astype(BF16))
    return x2.reshape(batch, seq, d)
```

```python
import functools
import math

import jax
import jax.numpy as jnp
from jax import lax
from jax.experimental import pallas as pl
from jax.experimental.pallas import tpu as pltpu

F32 = jnp.float32
BF16 = jnp.bfloat16
HIGHEST = lax.Precision.HIGHEST

D_MODEL = 1024
M_HEADS = 4
M_HEAD_DIM = 128
M_WIDTH = M_HEADS * M_HEAD_DIM
CONV_WIDTH = 4
A_HEADS = 8
A_HEAD_DIM = 64
A_WIDTH = A_HEADS * A_HEAD_DIM
MOBA_BLOCK = 256
MOBA_TOPK = 3
REL_BUCKETS = 32
REL_MAX_DIST = 1024
N_EXPERTS = 8
EPS = 1e-6

LANES = 128
SUBLANES = 8
NEG = -1e30
VMEM_LIMIT = 56 * 1024 * 1024

C_GA = 0
C_GB = D_MODEL
C_MQ = 2 * D_MODEL
C_MK = C_MQ + M_WIDTH
C_MV = C_MK + M_WIDTH
C_MO = C_MV + M_WIDTH
C_AK = C_MO + M_WIDTH
NAT_WIDTH = C_AK + A_WIDTH
N_BIAS_TILES = 6

PROJ_TM = 512
MLSTM_T = 512
MLSTM_CHUNK = 128
MERGE_TM = 512
FFN_TM = 512
FFN_CHUNK = 256
ROUTER_TM = 512
MOE_TM = 512
MOE_TF = 1792
SCATTER_R = 512


def _params(*sem):
    return pltpu.CompilerParams(dimension_semantics=sem, vmem_limit_bytes=VMEM_LIMIT)


def _const_spec(shape):
    nd = len(shape)
    return pl.BlockSpec(shape, lambda *_: (0,) * nd)


def _rms(x, g):
    return x * lax.rsqrt(jnp.mean(x * x, axis=-1, keepdims=True) + EPS) * g


ROW_SLAB = 128


def _in_proj_kernel(x_ref, g_ref, wn_ref, wt_ref, wg_ref, cw_ref, cb_ref, qg_ref, kg_ref, z_ref, zt_ref, gt_ref,
                    hn_s, qk_buf, k_buf, *, tiles_per_seq):
    tm = x_ref.shape[0]
    DA = A_HEAD_DIM

    @pl.when(pl.program_id(0) % tiles_per_seq == 0)
    def _sequence_start():
        qk_buf[0:SUBLANES, :] = jnp.zeros((SUBLANES, 2 * M_WIDTH), F32)

    hn_s[...] = _rms(x_ref[...], g_ref[...]).astype(BF16)

    def natural(c):
        return jnp.dot(hn_s[...], wn_ref[:, c:c + 512], preferred_element_type=F32)

    def transposed(c):
        return lax.dot_general(wt_ref[c:c + 256, :], hn_s[...], (((1,), (1,)), ((), ())),
                               preferred_element_type=F32)

    qk_buf[SUBLANES:SUBLANES + tm, 0:M_WIDTH] = natural(C_MQ)
    qk_buf[SUBLANES:SUBLANES + tm, M_WIDTH:] = natural(C_MK)
    for r in range(0, tm, ROW_SLAB):
        acc = cb_ref[...] + cw_ref[CONV_WIDTH - 1:CONV_WIDTH, :] * qk_buf[SUBLANES + r:SUBLANES + r + ROW_SLAB, :]
        for j in range(CONV_WIDTH - 1):
            off = SUBLANES - (CONV_WIDTH - 1) + j + r
            acc = acc + cw_ref[j:j + 1, :] * qk_buf[off:off + ROW_SLAB, :]
        qk = acc * jax.nn.sigmoid(acc)
        z_ref[r:r + ROW_SLAB, C_MQ:C_MQ + M_WIDTH] = qk[:, :M_WIDTH].astype(BF16)
        z_ref[r:r + ROW_SLAB, C_MK:C_MK + M_WIDTH] = (qk[:, M_WIDTH:] * (M_HEAD_DIM ** -0.5)).astype(BF16)
    qk_buf[0:SUBLANES, :] = qk_buf[tm:tm + SUBLANES, :]

    k_buf[...] = natural(C_AK)
    head0 = lax.broadcasted_iota(jnp.int32, (ROW_SLAB, LANES), 1) < DA
    for r in range(0, tm, ROW_SLAB):
        for t in range(A_WIDTH // LANES):
            cols = slice(t * LANES, (t + 1) * LANES)
            kt = k_buf[r:r + ROW_SLAB, cols]
            k2 = kt * kt
            s0 = jnp.sum(jnp.where(head0, k2, 0.0), axis=-1, keepdims=True)
            s1 = jnp.sum(jnp.where(head0, 0.0, k2), axis=-1, keepdims=True)
            kn = kt * jnp.where(head0, lax.rsqrt(s0 / DA + EPS), lax.rsqrt(s1 / DA + EPS)) * kg_ref[:, cols]
            z_ref[r:r + ROW_SLAB, C_AK + t * LANES:C_AK + (t + 1) * LANES] = kn.astype(BF16)

    for c in range(0, A_WIDTH, 256):
        q3 = transposed(c).reshape(256 // DA, DA, tm)
        inv = lax.rsqrt(jnp.mean(q3 * q3, axis=1, keepdims=True) + EPS)
        qn = q3 * inv * (qg_ref[...] * (DA ** -0.5 * LOG2E))
        zt_ref[c:c + 256, :] = qn.reshape(256, tm).astype(BF16)
    for c in (C_GA, C_GA + 512, C_GB, C_GB + 512, C_MV, C_MO):
        z_ref[:, c:c + 512] = natural(c).astype(BF16)
    for c in range(A_WIDTH, 2 * A_WIDTH, 256):
        zt_ref[c:c + 256, :] = transposed(c).astype(BF16)
    gt_ref[...] = jnp.dot(hn_s[...], wg_ref[...], preferred_element_type=F32)


def _in_proj(x2, g, wn, wt, wg, conv_w, conv_b, qg, kg, tiles_per_seq):
    n = x2.shape[0]
    tm = PROJ_TM
    return pl.pallas_call(
        functools.partial(_in_proj_kernel, tiles_per_seq=tiles_per_seq),
        grid=(n // tm,),
        in_specs=[
            pl.BlockSpec((tm, D_MODEL), lambda i: (i, 0)),
            _const_spec((1, D_MODEL)),
            _const_spec((D_MODEL, NAT_WIDTH)),
            _const_spec((2 * A_WIDTH, D_MODEL)),
            _const_spec((D_MODEL, LANES)),
            _const_spec((CONV_WIDTH, 2 * M_WIDTH)),
            _const_spec((1, 2 * M_WIDTH)),
            _const_spec((1, A_HEAD_DIM, 1)),
            _const_spec((1, A_WIDTH)),
        ],
        out_specs=[
            pl.BlockSpec((tm, NAT_WIDTH), lambda i: (i, 0)),
            pl.BlockSpec((2 * A_WIDTH, tm), lambda i: (0, i)),
            pl.BlockSpec((tm, LANES), lambda i: (i, 0)),
        ],
        out_shape=[
            jax.ShapeDtypeStruct((n, NAT_WIDTH), BF16),
            jax.ShapeDtypeStruct((2 * A_WIDTH, n), BF16),
            jax.ShapeDtypeStruct((n, LANES), F32),
        ],
        scratch_shapes=[pltpu.VMEM((tm, D_MODEL), BF16), pltpu.VMEM((tm + SUBLANES, 2 * M_WIDTH), F32),
                        pltpu.VMEM((tm, A_WIDTH), F32)],
        compiler_params=_params("arbitrary"),
        name="in_proj",
    )(x2, g, wn, wt, wg, conv_w, conv_b, qg, kg)


def _mlstm_kernel(zq_ref, zk_ref, zv_ref, zo_ref, gt_ref, gb_ref, ng_ref, o_ref, c_s, m_s, *, t_blk, chunk):
    L = chunk
    DH = M_HEAD_DIM
    assert L == LANES and DH == LANES

    @pl.when(pl.program_id(1) == 0)
    def _init():
        c_s[...] = jnp.zeros_like(c_s)
        m_s[...] = jnp.zeros_like(m_s)

    row = lax.broadcasted_iota(jnp.int32, (L, L), 0)
    col = lax.broadcasted_iota(jnp.int32, (L, L), 1)
    causal = col <= row
    tri = causal.astype(F32)
    lane = lax.broadcasted_iota(jnp.int32, (L, LANES), 1)
    ones_blk = jnp.ones((L, LANES), BF16)

    def chunk_body(c, carry):
        r0 = c * L
        g_pre = gt_ref[pl.ds(r0, L), :] + gb_ref[...]
        log_f = jnp.minimum(g_pre, 0.0) - jnp.log1p(jnp.exp(-jnp.abs(g_pre)))
        bcum = jnp.dot(tri, log_f, precision=HIGHEST, preferred_element_type=F32)
        gm = jnp.where(lane < M_HEADS, g_pre, bcum)
        gm_t = gm.T
        c_all = [c_s[h] for h in range(M_HEADS)]
        m_all = m_s[...]
        heads = range(M_HEADS)
        hs = [slice(h * DH, (h + 1) * DH) for h in heads]
        qh = [zq_ref[pl.ds(r0, L), hs[h]] for h in heads]
        kb = [zk_ref[pl.ds(r0, L), hs[h]] for h in heads]
        kf = [k.astype(F32) for k in kb]
        v_ext = [jnp.concatenate([zv_ref[pl.ds(r0, L), hs[h]], ones_blk], axis=1) for h in heads]
        qk = [lax.dot_general(qh[h], kb[h], (((1,), (1,)), ((), ())), preferred_element_type=F32)
              for h in heads]
        inter = [jnp.dot(qh[h], c_all[h].astype(BF16), preferred_element_type=F32) for h in heads]
        mt, w_state, s_bf, kw_t, decay, m_new = [], [], [], [], [], []
        for h in heads:
            bb = jnp.broadcast_to(gm[:, M_HEADS + h:M_HEADS + h + 1], (L, LANES))
            ii = jnp.broadcast_to(gm[:, h:h + 1], (L, LANES))
            bb_row = gm_t[M_HEADS + h:M_HEADS + h + 1, :]
            ii_row = gm_t[h:h + 1, :]
            m_old = m_all[h:h + 1, :]
            dlog = jnp.where(causal, bb - (bb_row - ii_row), -jnp.inf)
            a = bb + m_old
            mt.append(jnp.maximum(a, jnp.broadcast_to(jnp.max(dlog, axis=-1, keepdims=True), (L, LANES))))
            w_state.append(jnp.exp(a - mt[h]))
            s_bf.append((qk[h] * jnp.exp(dlog - mt[h])).astype(BF16))
            b_last = bb[L - 1:L, :]
            g = b_last - bb + ii
            m_new.append(jnp.maximum(b_last + m_old, jnp.max(g, axis=0, keepdims=True)))
            decay.append(jnp.exp(b_last + m_old - m_new[h]))
            kw_t.append((kf[h] * jnp.exp(g - m_new[h])).T.astype(BF16))
        intra = [jnp.dot(s_bf[h], v_ext[h], preferred_element_type=F32) for h in heads]
        upd = [jnp.dot(kw_t[h], v_ext[h], preferred_element_type=F32) for h in heads]
        for h in heads:
            num = intra[h][:, :DH] + w_state[h] * inter[h][:, :DH]
            den = intra[h][:, DH:] + w_state[h] * inter[h][:, DH:]
            h_t = num / jnp.maximum(jnp.abs(den), jnp.exp(-mt[h]))
            hc = jax.nn.sigmoid(zo_ref[pl.ds(r0, L), hs[h]].astype(F32)) * h_t
            o_ref[pl.ds(r0, L), hs[h]] = _rms(hc, ng_ref[:, hs[h]]).astype(BF16)
        for h in heads:
            c_s[h] = jnp.concatenate([decay[h], decay[h]], axis=1) * c_all[h] + upd[h]
            m_s[h:h + 1, :] = m_new[h]
        return carry

    for c in range(t_blk // L):
        chunk_body(c, 0)


def _mlstm(z, gates, gate_b, norm_g, batch, seq):
    n = batch * seq
    t = MLSTM_T
    nt = seq // t
    row_blk = lambda b, s: b * nt + s
    zspec = lambda cb: pl.BlockSpec((t, M_WIDTH), lambda b, s: (row_blk(b, s), cb))
    return pl.pallas_call(
        functools.partial(_mlstm_kernel, t_blk=t, chunk=MLSTM_CHUNK),
        grid=(batch, nt),
        in_specs=[
            zspec(C_MQ // M_WIDTH), zspec(C_MK // M_WIDTH), zspec(C_MV // M_WIDTH), zspec(C_MO // M_WIDTH),
            pl.BlockSpec((t, LANES), lambda b, s: (row_blk(b, s), 0)),
            _const_spec((1, LANES)),
            _const_spec((1, M_WIDTH)),
        ],
        out_specs=pl.BlockSpec((t, M_WIDTH), lambda b, s: (row_blk(b, s), 0)),
        out_shape=jax.ShapeDtypeStruct((n, M_WIDTH), BF16),
        scratch_shapes=[
            pltpu.VMEM((M_HEADS, M_HEAD_DIM, M_HEAD_DIM + LANES), F32),
            pltpu.VMEM((SUBLANES, LANES), F32),
        ],
        compiler_params=_params("arbitrary", "arbitrary"),
        name="mlstm",
    )(z, z, z, z, gates, gate_b, norm_g)


MOBA_V_ROWS = A_HEAD_DIM + 16
MOBA_ROWB_ROWS = 24
MOBA_GROUP = 34
LOG2E = math.log2(math.e)
FAST_SOFTMAX_MIN_DENOM = 1e-25


def _moba_items(n_blocks):
    items = []
    for qb in range(n_blocks):
        items.append((qb, qb, 0, 0))
        for j in range(qb):
            items.append((qb, j, min(qb - j, N_BIAS_TILES - 1), j + 1))
    n_groups = -(-len(items) // MOBA_GROUP)
    noop = (n_blocks - 1, 0, 0, MOBA_ROWB_ROWS - 1)
    items += [noop] * (n_groups * MOBA_GROUP + 2 - len(items))
    return n_groups, [jnp.asarray([it[c] for it in items], jnp.int32) for c in range(4)]


def _moba_kernel(it_q, it_blk, it_tile, it_row, qt_ref, k_ref, vt_ref, bias_ref, bound_ref, o_ref,
                 vt_s, kmean_s, qh_s, rowb_s, acc_s, st_a, st_b, p_a, p_b, *, n_blocks, n_groups):
    BS = MOBA_BLOCK
    DA = A_HEAD_DIM
    seq = n_blocks * BS

    ones_row = jnp.where(lax.broadcasted_iota(jnp.int32, (16, BS), 0) == 0, 1.0, 0.0).astype(BF16)
    for j in range(n_blocks):
        kmean_s[j:j + 1, :] = jnp.mean(k_ref[j * BS:(j + 1) * BS, :].astype(F32), axis=0, keepdims=True)
        for h in range(2):
            vt_s[j, h, 0:DA, :] = vt_ref[h * DA:(h + 1) * DA, j * BS:(j + 1) * BS]
            vt_s[j, h, DA:MOBA_V_ROWS, :] = ones_row

    qn = qt_ref[...].astype(F32)
    head0_q = lax.broadcasted_iota(jnp.int32, qn.shape, 0) < DA

    blk = lax.broadcasted_iota(jnp.int32, (n_blocks, BS), 0)
    lane_m = lax.broadcasted_iota(jnp.int32, (n_blocks, LANES), 1)
    kmean = kmean_s[...]
    kmean2 = jnp.concatenate([jnp.where(lane_m < DA, kmean, 0.0), jnp.where(lane_m < DA, 0.0, kmean)], axis=0)
    k_hi = kmean2.astype(BF16)
    k_mid = (kmean2 - k_hi.astype(F32)).astype(BF16)
    k_lo = (kmean2 - k_hi.astype(F32) - k_mid.astype(F32)).astype(BF16)
    split = jnp.dot(jnp.concatenate([k_hi, k_mid, k_lo], axis=0), qt_ref[...], preferred_element_type=F32)
    gates = split[0:2 * n_blocks] + split[2 * n_blocks:4 * n_blocks] + split[4 * n_blocks:6 * n_blocks]
    for h in range(2):
        bound = bound_ref[h, :, 0:1]
        hmask_q = head0_q if h == 0 else jnp.logical_not(head0_q)
        qh = jnp.where(hmask_q, qn, 0.0).astype(BF16)
        for qb in range(n_blocks):
            cols = slice(qb * BS, (qb + 1) * BS)
            past = blk < qb
            gate = jnp.where(past, gates[h * n_blocks:(h + 1) * n_blocks, cols], -jnp.inf)
            rank = jnp.zeros((n_blocks, BS), jnp.int32)
            for j2 in range(qb):
                other = gate[j2:j2 + 1, :]
                beats = (other > gate) | ((other == gate) & (j2 < blk))
                rank = rank + beats.astype(jnp.int32)
            selb = jnp.where(past & (rank < MOBA_TOPK), 0.0, NEG) - bound
            qh_s[qb, h] = qh[:, cols]
            rowb_s[qb, h, 0:1, :] = jnp.broadcast_to(-bound, (1, BS))
            rowb_s[qb, h, 1:n_blocks + 1, :] = selb
            rowb_s[qb, h, n_blocks + 1:, :] = jnp.full((MOBA_ROWB_ROWS - n_blocks - 1, BS), NEG, F32)
    acc_s[...] = jnp.zeros_like(acc_s)

    def scores(h, qb, j, tile, row):
        kj = k_ref[pl.ds(pl.multiple_of(j * BS, BS), BS), :]
        st = jnp.dot(kj, qh_s[qb, h], preferred_element_type=F32)
        return st + bias_ref[h, tile] + rowb_s[qb, h, pl.ds(row, 1), :]

    def pv(h, j, p):
        return jnp.dot(vt_s[j, h], p, preferred_element_type=F32)

    def stage_scores(i, st_ref):
        for h in range(2):
            st_ref[h] = scores(h, it_q[i], it_blk[i], it_tile[i], it_row[i])

    def stage_exp(st_ref, p_ref):
        for h in range(2):
            p_ref[h] = jnp.exp2(st_ref[h]).astype(BF16)

    def stage_pv(i, p_ref):
        for h in range(2):
            acc_s[it_q[i], h] += pv(h, it_blk[i], p_ref[h])

    def group(m, carry):
        for u in range(0, MOBA_GROUP, 2):
            i = MOBA_GROUP * m + u
            stage_pv(i, p_a)
            stage_scores(i + 2, st_a)
            stage_exp(st_b, p_b)
            stage_pv(i + 1, p_b)
            stage_scores(i + 3, st_b)
            stage_exp(st_a, p_a)
        return carry

    stage_scores(0, st_a)
    stage_exp(st_a, p_a)
    stage_scores(1, st_b)
    lax.fori_loop(0, n_groups, group, 0)

    def finish(qb, l_min):
        outs = []
        for h in range(2):
            acc = acc_s[qb, h]
            l = acc[DA:DA + 1, :]
            outs.append(acc[0:DA, :] / l)
            l_min = jnp.minimum(l_min, jnp.min(l))
        o_ref[qb * BS:(qb + 1) * BS, :] = jnp.concatenate(outs, axis=0).T.astype(BF16)
        return l_min

    l_min = jnp.float32(jnp.inf)
    for qb in range(n_blocks):
        l_min = finish(qb, l_min)

    def online(qb, carry):
        state = []
        for h in range(2):
            st = scores(h, qb, qb, 0, 0)
            m = jnp.max(st, axis=0, keepdims=True)
            p = jnp.exp2(st - m)
            state += [m, jnp.sum(p, axis=0, keepdims=True), pv(h, qb, p.astype(BF16))[0:DA, :]]

        def past_body(j, state):
            tile = jnp.minimum(qb - j, N_BIAS_TILES - 1)
            new = []
            for h in range(2):
                m_old, l_old, acc_old = state[3 * h:3 * h + 3]
                st = scores(h, qb, j, tile, j + 1)
                m_new = jnp.maximum(m_old, jnp.max(st, axis=0, keepdims=True))
                alpha = jnp.exp2(m_old - m_new)
                p = jnp.exp2(st - m_new)
                new += [m_new, alpha * l_old + jnp.sum(p, axis=0, keepdims=True),
                        alpha * acc_old + pv(h, j, p.astype(BF16))[0:DA, :]]
            return tuple(new)

        state = lax.fori_loop(0, qb, past_body, tuple(state))
        outs = [state[3 * h + 2] / state[3 * h + 1] for h in range(2)]
        o_ref[pl.ds(pl.multiple_of(qb * BS, BS), BS), :] = jnp.concatenate(outs, axis=0).T.astype(BF16)
        return carry

    @pl.when(l_min < FAST_SOFTMAX_MIN_DENOM)
    def _redo():
        lax.fori_loop(0, n_blocks, online, 0)


def _moba_logit_bound(rel_bias, q_norm_g, k_norm_g):
    qk = A_HEAD_DIM * jnp.max(jnp.abs(q_norm_g)) * jnp.max(jnp.abs(k_norm_g)) * (A_HEAD_DIM ** -0.5) * 1.02
    b = (qk + jnp.max(rel_bias, axis=0)) * LOG2E
    return jnp.broadcast_to(b[:, None, None], (A_HEADS, 1, LANES)).astype(F32)


def _moba(z, zt, bias_tiles, bound, batch, seq):
    n = batch * seq
    nb = seq // MOBA_BLOCK
    bs = MOBA_BLOCK
    hp = A_HEADS // 2
    w2 = 2 * A_HEAD_DIM
    assert nb + 2 <= MOBA_ROWB_ROWS
    n_groups, items = _moba_items(nb)
    grid_spec = pltpu.PrefetchScalarGridSpec(
        num_scalar_prefetch=len(items),
        grid=(batch, hp),
        in_specs=[
            pl.BlockSpec((w2, seq), lambda b, p, *_: (p, b)),
            pl.BlockSpec((seq, w2), lambda b, p, *_: (b, C_AK // w2 + p)),
            pl.BlockSpec((w2, seq), lambda b, p, *_: (hp + p, b)),
            pl.BlockSpec((2, N_BIAS_TILES, bs, bs), lambda b, p, *_: (p, 0, 0, 0)),
            pl.BlockSpec((2, 1, LANES), lambda b, p, *_: (p, 0, 0)),
        ],
        out_specs=pl.BlockSpec((seq, w2), lambda b, p, *_: (b, p)),
        scratch_shapes=[
            pltpu.VMEM((nb, 2, MOBA_V_ROWS, bs), BF16),
            pltpu.VMEM((nb, w2), F32),
            pltpu.VMEM((nb, 2, w2, bs), BF16),
            pltpu.VMEM((nb, 2, MOBA_ROWB_ROWS, bs), F32),
            pltpu.VMEM((nb, 2, MOBA_V_ROWS, bs), F32),
            pltpu.VMEM((2, bs, bs), F32),
            pltpu.VMEM((2, bs, bs), F32),
            pltpu.VMEM((2, bs, bs), BF16),
            pltpu.VMEM((2, bs, bs), BF16),
        ],
    )
    return pl.pallas_call(
        functools.partial(_moba_kernel, n_blocks=nb, n_groups=n_groups),
        grid_spec=grid_spec,
        out_shape=jax.ShapeDtypeStruct((n, A_WIDTH), BF16),
        compiler_params=_params("arbitrary", "arbitrary"),
        name="moba",
    )(*items, zt, z, zt, bias_tiles, bound)


def _t5_bucket(dist):
    n = jnp.maximum(dist, 0)
    max_exact = REL_BUCKETS // 2
    log_ratio = jnp.log(jnp.maximum(n, max_exact).astype(F32) / max_exact) / math.log(REL_MAX_DIST / max_exact)
    large = max_exact + (log_ratio * (REL_BUCKETS - max_exact)).astype(jnp.int32)
    large = jnp.minimum(large, REL_BUCKETS - 1)
    return jnp.where(n < max_exact, n, large)


def _bias_tiles_kernel(rb_ref, bucket_ref, o_ref):
    bucket = bucket_ref[0]
    hit = [bucket == b for b in range(REL_BUCKETS)]
    tk = lax.broadcasted_iota(jnp.int32, bucket.shape, 0)
    tq = lax.broadcasted_iota(jnp.int32, bucket.shape, 1)
    masked = (tk > tq) & (pl.program_id(0) == 0)
    for h in range(A_HEADS):
        acc = jnp.zeros(bucket.shape, F32)
        for b in range(REL_BUCKETS):
            acc = jnp.where(hit[b], rb_ref[b, h], acc)
        o_ref[h, 0] = jnp.where(masked, NEG, acc * LOG2E)


def _moba_bias_tiles(rel_bias):
    assert (N_BIAS_TILES - 1) * MOBA_BLOCK - (MOBA_BLOCK - 1) >= REL_MAX_DIST
    tk = jnp.arange(MOBA_BLOCK)[None, :, None]
    tq = jnp.arange(MOBA_BLOCK)[None, None, :]
    diff = jnp.arange(N_BIAS_TILES)[:, None, None]
    bucket = _t5_bucket(diff * MOBA_BLOCK + tq - tk).astype(jnp.int32)
    return pl.pallas_call(
        _bias_tiles_kernel,
        grid=(N_BIAS_TILES,),
        in_specs=[
            pl.BlockSpec(memory_space=pltpu.SMEM),
            pl.BlockSpec((1, MOBA_BLOCK, MOBA_BLOCK), lambda t: (t, 0, 0)),
        ],
        out_specs=pl.BlockSpec((A_HEADS, 1, MOBA_BLOCK, MOBA_BLOCK), lambda t: (0, t, 0, 0)),
        out_shape=jax.ShapeDtypeStruct((A_HEADS, N_BIAS_TILES, MOBA_BLOCK, MOBA_BLOCK), F32),
        compiler_params=_params("arbitrary"),
        name="moba_bias_tiles",
    )(rel_bias.astype(F32), bucket)


def _merge_kernel(x_ref, ha_ref, ob_ref, ga_ref, gb_ref, wa_ref, wb_ref, wo_ref, o_ref):
    ya = jnp.dot(ha_ref[...], wa_ref[...], preferred_element_type=F32)
    yb = jnp.dot(ob_ref[...], wb_ref[...], preferred_element_type=F32)
    y = jax.nn.sigmoid(ga_ref[...].astype(F32)) * ya + jax.nn.sigmoid(gb_ref[...].astype(F32)) * yb
    o_ref[...] = x_ref[...] + jnp.dot(y.astype(BF16), wo_ref[...], preferred_element_type=F32)


def _merge(x2, ha, ob, z, wa, wb, wo):
    n = x2.shape[0]
    tm = MERGE_TM
    return pl.pallas_call(
        _merge_kernel,
        grid=(n // tm,),
        in_specs=[
            pl.BlockSpec((tm, D_MODEL), lambda i: (i, 0)),
            pl.BlockSpec((tm, M_WIDTH), lambda i: (i, 0)),
            pl.BlockSpec((tm, A_WIDTH), lambda i: (i, 0)),
            pl.BlockSpec((tm, D_MODEL), lambda i: (i, C_GA // D_MODEL)),
            pl.BlockSpec((tm, D_MODEL), lambda i: (i, C_GB // D_MODEL)),
            _const_spec((M_WIDTH, D_MODEL)),
            _const_spec((A_WIDTH, D_MODEL)),
            _const_spec((D_MODEL, D_MODEL)),
        ],
        out_specs=pl.BlockSpec((tm, D_MODEL), lambda i: (i, 0)),
        out_shape=jax.ShapeDtypeStruct((n, D_MODEL), F32),
        compiler_params=_params("arbitrary"),
        name="merge",
    )(x2, ha, ob, z, z, wa, wb, wo)


def _ffn_kernel(x_ref, g_ref, wg_ref, wu_ref, wd_ref, o_ref, *, d_ff):
    x = x_ref[...]
    hn = _rms(x, g_ref[...]).astype(BF16)
    acc = x
    for c in range(0, d_ff, FFN_CHUNK):
        g = jnp.dot(hn, wg_ref[:, c:c + FFN_CHUNK], preferred_element_type=F32)
        u = jnp.dot(hn, wu_ref[:, c:c + FFN_CHUNK], preferred_element_type=F32)
        a = (g * jax.nn.sigmoid(g) * u).astype(BF16)
        acc = acc + jnp.dot(a, wd_ref[c:c + FFN_CHUNK, :], preferred_element_type=F32)
    o_ref[...] = acc


def _ffn(x2, g, wg, wu, wd):
    n = x2.shape[0]
    d_ff = wg.shape[1]
    assert d_ff % FFN_CHUNK == 0
    tm = FFN_TM
    return pl.pallas_call(
        functools.partial(_ffn_kernel, d_ff=d_ff),
        grid=(n // tm,),
        in_specs=[
            pl.BlockSpec((tm, D_MODEL), lambda i: (i, 0)),
            _const_spec((1, D_MODEL)),
            _const_spec((D_MODEL, d_ff)),
            _const_spec((D_MODEL, d_ff)),
            _const_spec((d_ff, D_MODEL)),
        ],
        out_specs=pl.BlockSpec((tm, D_MODEL), lambda i: (i, 0)),
        out_shape=jax.ShapeDtypeStruct((n, D_MODEL), F32),
        compiler_params=_params("arbitrary"),
        name="ffn_dense",
    )(x2, g, wg, wu, wd)


META_W0, META_W1, META_E0, META_E1, META_R0, META_R1 = range(6)


def _router_kernel(x_ref, g_ref, rw_ref, h_ref, meta_ref, cnt_ref, carry_s):
    @pl.when(pl.program_id(0) == 0)
    def _init():
        carry_s[...] = jnp.zeros_like(carry_s)

    h = _rms(x_ref[...], g_ref[...])
    h_ref[...] = h
    tm = h.shape[0]
    logits = jnp.dot(h, rw_ref[...], precision=HIGHEST, preferred_element_type=F32)
    lane = lax.broadcasted_iota(jnp.int32, (tm, LANES), 1)
    lg = jnp.where(lane < N_EXPERTS, logits, -jnp.inf)
    m1 = jnp.max(lg, axis=-1, keepdims=True)
    i1 = jnp.min(jnp.where(lg == m1, lane, LANES), axis=-1, keepdims=True)
    lg2 = jnp.where(lane == i1, -jnp.inf, lg)
    m2 = jnp.max(lg2, axis=-1, keepdims=True)
    i2 = jnp.min(jnp.where(lg2 == m2, lane, LANES), axis=-1, keepdims=True)
    e = jnp.exp(m2 - m1)
    w1 = 1.0 / (1.0 + e)
    w2 = e / (1.0 + e)
    hit1 = lane == i1
    hit2 = lane == i2
    onehot = jnp.where(hit1 | hit2, 1.0, 0.0)
    r = lax.broadcasted_iota(jnp.int32, (tm, tm), 0)
    c = lax.broadcasted_iota(jnp.int32, (tm, tm), 1)
    before = jnp.where(c < r, 1.0, 0.0).astype(BF16)
    carry = carry_s[0:1, :]
    pref = jnp.dot(before, onehot.astype(BF16), preferred_element_type=F32) + carry
    r1 = jnp.sum(jnp.where(hit1, pref, 0.0), axis=-1, keepdims=True)
    r2 = jnp.sum(jnp.where(hit2, pref, 0.0), axis=-1, keepdims=True)
    new_carry = carry + jnp.sum(onehot, axis=0, keepdims=True)
    carry_s[...] = jnp.broadcast_to(new_carry, carry_s.shape)
    cnt_ref[...] = jnp.broadcast_to(new_carry, cnt_ref.shape)
    meta = jnp.zeros((tm, LANES), F32)
    for idx, val in ((META_W0, w1), (META_W1, w2), (META_E0, i1.astype(F32)), (META_E1, i2.astype(F32)),
                     (META_R0, r1), (META_R1, r2)):
        meta = jnp.where(lane == idx, val, meta)
    meta_ref[...] = meta


def _router(x2, g, rw):
    n = x2.shape[0]
    tm = ROUTER_TM
    return pl.pallas_call(
        _router_kernel,
        grid=(n // tm,),
        in_specs=[
            pl.BlockSpec((tm, D_MODEL), lambda i: (i, 0)),
            _const_spec((1, D_MODEL)),
            _const_spec((D_MODEL, LANES)),
        ],
        out_specs=[
            pl.BlockSpec((tm, D_MODEL), lambda i: (i, 0)),
            pl.BlockSpec((tm, LANES), lambda i: (i, 0)),
            _const_spec((SUBLANES, LANES)),
        ],
        out_shape=[
            jax.ShapeDtypeStruct((n, D_MODEL), F32),
            jax.ShapeDtypeStruct((n, LANES), F32),
            jax.ShapeDtypeStruct((SUBLANES, LANES), F32),
        ],
        scratch_shapes=[pltpu.VMEM((SUBLANES, LANES), F32)],
        compiler_params=_params("arbitrary"),
        name="moe_router",
    )(x2, g, rw)


def _row_copy(src_ref, src_row, dst_ref, dst_row, sem):
    return pltpu.make_async_copy(src_ref.at[pl.ds(src_row, 1)], dst_ref.at[pl.ds(dst_row, 1)], sem)


def _scatter_kernel(pos_ref, pad_ref, h_ref, xs_ref, zero_s, sem, *, rows, pads):
    @pl.when(pl.program_id(0) == 0)
    def _init():
        zero_s[...] = jnp.zeros_like(zero_s)

    for r in range(rows):
        _row_copy(h_ref, r, xs_ref, pos_ref[0, 0, r], sem).start(priority=0)
        _row_copy(h_ref, r, xs_ref, pos_ref[0, 0, rows + r], sem).start(priority=1)
    for r in range(pads):
        _row_copy(zero_s, 0, xs_ref, pad_ref[0, 0, r], sem).start(priority=r % 2)
    for _ in range(2):
        pltpu.make_async_copy(h_ref, xs_ref.at[pl.ds(0, rows)], sem).wait()
    pltpu.make_async_copy(h_ref.at[pl.ds(0, pads)], xs_ref.at[pl.ds(0, pads)], sem).wait()


def _scatter(pos, pad_pos, h, n_rows):
    n = h.shape[0]
    r = SCATTER_R
    steps = n // r
    pads = pad_pos.shape[0] // steps
    assert pads * steps == pad_pos.shape[0] and 2 * n + pad_pos.shape[0] == n_rows
    return pl.pallas_call(
        functools.partial(_scatter_kernel, rows=r, pads=pads),
        grid=(steps,),
        in_specs=[
            pl.BlockSpec((1, 1, 2 * r), lambda i: (i, 0, 0), memory_space=pltpu.SMEM),
            pl.BlockSpec((1, 1, pads), lambda i: (i, 0, 0), memory_space=pltpu.SMEM),
            pl.BlockSpec((r, D_MODEL), lambda i: (i, 0)),
        ],
        out_specs=pl.BlockSpec(memory_space=pl.ANY),
        out_shape=jax.ShapeDtypeStruct((n_rows, D_MODEL), F32),
        scratch_shapes=[pltpu.VMEM((SUBLANES, D_MODEL), F32), pltpu.SemaphoreType.DMA(())],
        compiler_params=_params("arbitrary"),
        name="moe_scatter",
    )(pos, pad_pos.reshape(steps, 1, pads), h)


def _moe_kernel(blk_ref, exp_ref, valid_ref, xs_ref, wg_ref, wu_ref, wd_ref, y_ref, xb_s, acc_s, *, n_ff):
    del blk_ref, exp_ref
    i = pl.program_id(0)
    f = pl.program_id(1)

    @pl.when((valid_ref[i] == 0) & (f == n_ff - 1))
    def _idle():
        y_ref[...] = jnp.zeros_like(y_ref)

    @pl.when(valid_ref[i] == 1)
    def _tile():
        @pl.when(f == 0)
        def _cast():
            xb_s[...] = xs_ref[...].astype(BF16)

        xb = xb_s[...]
        d = None
        for c in range(0, wg_ref.shape[2], FFN_CHUNK):
            g = jnp.dot(xb, wg_ref[0, :, c:c + FFN_CHUNK], preferred_element_type=F32)
            u = jnp.dot(xb, wu_ref[0, :, c:c + FFN_CHUNK], preferred_element_type=F32)
            a = (g * jax.nn.sigmoid(g) * u).astype(BF16)
            dc = jnp.dot(a, wd_ref[0, c:c + FFN_CHUNK, :], preferred_element_type=F32)
            d = dc if d is None else d + dc

        @pl.when(f == 0)
        def _first():
            acc_s[...] = d

        @pl.when(f > 0)
        def _rest():
            acc_s[...] += d

        @pl.when(f == n_ff - 1)
        def _out():
            y_ref[...] = acc_s[...]


def _moe_experts(tile_blk, tile_exp, tile_valid, xs, wg, wu, wd):
    rows = xs.shape[0]
    d_ff = wg.shape[2]
    tm, tf = MOE_TM, MOE_TF
    n_ff = d_ff // tf
    assert rows % tm == 0 and d_ff % tf == 0
    grid_spec = pltpu.PrefetchScalarGridSpec(
        num_scalar_prefetch=3,
        grid=(rows // tm, n_ff),
        in_specs=[
            pl.BlockSpec((tm, D_MODEL), lambda i, f, blk, ex, va: (blk[i], 0)),
            pl.BlockSpec((1, D_MODEL, tf), lambda i, f, blk, ex, va: (ex[i], 0, f)),
            pl.BlockSpec((1, D_MODEL, tf), lambda i, f, blk, ex, va: (ex[i], 0, f)),
            pl.BlockSpec((1, tf, D_MODEL), lambda i, f, blk, ex, va: (ex[i], f, 0)),
        ],
        out_specs=pl.BlockSpec((tm, D_MODEL), lambda i, f, blk, ex, va: (i, 0)),
        scratch_shapes=[pltpu.VMEM((tm, D_MODEL), BF16), pltpu.VMEM((tm, D_MODEL), F32)],
    )
    return pl.pallas_call(
        functools.partial(_moe_kernel, n_ff=n_ff),
        grid_spec=grid_spec,
        out_shape=jax.ShapeDtypeStruct((rows, D_MODEL), F32),
        compiler_params=_params("arbitrary", "arbitrary"),
        name="moe_experts",
    )(tile_blk, tile_exp, tile_valid, xs, wg, wu, wd)


def _combine_kernel(pos_ref, x_ref, meta_ref, ys_ref, o_ref, ybuf, sem, *, rows):
    for r in range(rows):
        _row_copy(ys_ref, pos_ref[0, 0, r], ybuf, r, sem).start(priority=0)
        _row_copy(ys_ref, pos_ref[0, 0, rows + r], ybuf, rows + r, sem).start(priority=1)
    pltpu.make_async_copy(ys_ref.at[pl.ds(0, 2 * rows)], ybuf, sem).wait()
    meta = meta_ref[...]
    w0 = meta[:, META_W0:META_W0 + 1]
    w1 = meta[:, META_W1:META_W1 + 1]
    o_ref[...] = x_ref[...] + w0 * ybuf[0:rows, :] + w1 * ybuf[rows:2 * rows, :]


def _combine(pos, x2, meta, ys):
    n = x2.shape[0]
    r = SCATTER_R
    return pl.pallas_call(
        functools.partial(_combine_kernel, rows=r),
        grid=(n // r,),
        in_specs=[
            pl.BlockSpec((1, 1, 2 * r), lambda i: (i, 0, 0), memory_space=pltpu.SMEM),
            pl.BlockSpec((r, D_MODEL), lambda i: (i, 0)),
            pl.BlockSpec((r, LANES), lambda i: (i, 0)),
            pl.BlockSpec(memory_space=pl.ANY),
        ],
        out_specs=pl.BlockSpec((r, D_MODEL), lambda i: (i, 0)),
        out_shape=jax.ShapeDtypeStruct((n, D_MODEL), F32),
        scratch_shapes=[pltpu.VMEM((2 * r, D_MODEL), F32), pltpu.SemaphoreType.DMA(())],
        compiler_params=_params("arbitrary"),
        name="moe_combine",
    )(pos, x2, meta, ys)


def _moe(x2, g, router_w, wg, wu, wd):
    n = x2.shape[0]
    tm = MOE_TM
    rw = jnp.zeros((D_MODEL, LANES), F32).at[:, :N_EXPERTS].set(router_w)
    h, meta, cnt = _router(x2, g, rw)
    counts = cnt[0, :N_EXPERTS].astype(jnp.int32)
    padded = ((counts + tm - 1) // tm) * tm
    ends = jnp.cumsum(padded)
    offs = ends - padded
    n_tiles = (2 * n) // tm + N_EXPERTS
    e0 = meta[:, META_E0].astype(jnp.int32)
    e1 = meta[:, META_E1].astype(jnp.int32)
    pos0 = offs[e0] + meta[:, META_R0].astype(jnp.int32)
    pos1 = offs[e1] + meta[:, META_R1].astype(jnp.int32)
    r = SCATTER_R
    pos = jnp.concatenate([pos0.reshape(n // r, 1, r), pos1.reshape(n // r, 1, r)], axis=-1)
    starts = jnp.arange(n_tiles, dtype=jnp.int32) * tm
    n_valid = ends[-1] // tm
    tile_valid = (starts < ends[-1]).astype(jnp.int32)
    tile_blk = jnp.minimum(jnp.arange(n_tiles, dtype=jnp.int32), n_valid - 1)
    tile_exp = jnp.sum((tile_blk[:, None] * tm >= ends[None, :]).astype(jnp.int32), axis=1)
    n_rows = n_tiles * tm
    seg_base = jnp.concatenate([offs + counts, ends[-1:]])
    seg_len = jnp.concatenate([padded - counts, n_rows - ends[-1:]])
    seg_end = jnp.cumsum(seg_len)
    k = jnp.arange(n_rows - 2 * n, dtype=jnp.int32)
    seg = jnp.sum((k[:, None] >= seg_end[None, :]).astype(jnp.int32), axis=1)
    pad_pos = seg_base[seg] + k - (seg_end - seg_len)[seg]
    xs = _scatter(pos, pad_pos, h, n_rows)
    ys = _moe_experts(tile_blk, tile_exp, tile_valid, xs, wg, wu, wd)
    return _combine(pos, x2, meta, ys)


def _mixer(x2, batch, seq, rel_bias, bias_tiles, norm_g, w_in, conv_w, conv_b, igate_b, fgate_b, mlstm_norm_g,
           q_norm_g, k_norm_g, w_branch_a, w_branch_b, w_out):
    sizes = (M_WIDTH, M_WIDTH, M_WIDTH, M_WIDTH, M_HEADS, M_HEADS, A_WIDTH, A_WIDTH, A_WIDTH, D_MODEL, D_MODEL)
    cuts = [0]
    for s in sizes:
        cuts.append(cuts[-1] + s)
    mq, mk, mv, mo, mi, mf, aq, ak, av, ga, gb = [w_in[:, cuts[i]:cuts[i + 1]] for i in range(len(sizes))]
    wn = jnp.concatenate([ga, gb, mq, mk, mv, mo, ak], axis=1).astype(BF16)
    wt = jnp.concatenate([aq, av], axis=1).T.astype(BF16)
    wgate = jnp.zeros((D_MODEL, LANES), F32).at[:, :M_HEADS].set(mi).at[:, M_HEADS:2 * M_HEADS].set(mf).astype(BF16)
    gate_b = jnp.zeros((1, LANES), F32).at[0, :M_HEADS].set(igate_b).at[0, M_HEADS:2 * M_HEADS].set(fgate_b)
    z, zt, gates = _in_proj(x2, norm_g.reshape(1, D_MODEL), wn, wt, wgate, conv_w, conv_b.reshape(1, -1),
                            q_norm_g.reshape(1, A_HEAD_DIM, 1), jnp.tile(k_norm_g, A_HEADS).reshape(1, A_WIDTH),
                            seq // PROJ_TM)
    ha = _mlstm(z, gates, gate_b, mlstm_norm_g.reshape(1, -1), batch, seq)
    ob = _moba(z, zt, bias_tiles, _moba_logit_bound(rel_bias, q_norm_g, k_norm_g), batch, seq)
    return _merge(x2, ha, ob, z, w_branch_a.astype(BF16), w_branch_b.astype(BF16), w_out.astype(BF16))


def kernel(x, rel_bias, mix_norm_g, w_in, conv_w, conv_b, igate_b, fgate_b, mlstm_norm_g, q_norm_g, k_norm_g,
           w_branch_a, w_branch_b, w_out, ffn_norm_g, dense_w_gate, dense_w_up, dense_w_down, router_w,
           expert_w_gate, expert_w_up, expert_w_down):
    batch, seq, d = x.shape
    depth = w_in.shape[0]
    assert d == D_MODEL and seq % MLSTM_T == 0 and seq % MOBA_BLOCK == 0 and seq % PROJ_TM == 0
    assert (batch * seq) % MOE_TM == 0
    x2 = x.reshape(batch * seq, d)
    bias_tiles = _moba_bias_tiles(rel_bias)
    for layer in range(depth):
        x2 = _mixer(x2, batch, seq, rel_bias, bias_tiles, mix_norm_g[layer], w_in[layer], conv_w[layer], conv_b[layer],
                    igate_b[layer], fgate_b[layer], mlstm_norm_g[layer], q_norm_g[layer], k_norm_g[layer],
                    w_branch_a[layer], w_branch_b[layer], w_out[layer])
        g = ffn_norm_g[layer].reshape(1, d)
        j = layer // 2
        if layer % 2 == 0:
            x2 = _ffn(x2, g, dense_w_gate[j].astype(BF16), dense_w_up[j].astype(BF16), dense_w_down[j].astype(BF16))
        else:
            x2 = _moe(x2, g, router_w[j], expert_w_gate[j].astype(BF16), expert_w_up[j].astype(BF16),
                      expert_w_down[j].astype(BF16))
    return x2.reshape(batch, seq, d)
```

```python
import functools
import math

import jax
import jax.numpy as jnp
from jax import lax
from jax.experimental import pallas as pl
from jax.experimental.pallas import tpu as pltpu

F32 = jnp.float32
BF16 = jnp.bfloat16
HIGHEST = lax.Precision.HIGHEST

D_MODEL = 1024
M_HEADS = 4
M_HEAD_DIM = 128
M_WIDTH = M_HEADS * M_HEAD_DIM
CONV_WIDTH = 4
A_HEADS = 8
A_HEAD_DIM = 64
A_WIDTH = A_HEADS * A_HEAD_DIM
MOBA_BLOCK = 256
MOBA_TOPK = 3
REL_BUCKETS = 32
REL_MAX_DIST = 1024
N_EXPERTS = 8
EPS = 1e-6

LANES = 128
SUBLANES = 8
NEG = -1e30
VMEM_LIMIT = 56 * 1024 * 1024

C_GA = 0
C_GB = D_MODEL
C_MQ = 2 * D_MODEL
C_MK = C_MQ + M_WIDTH
C_MV = C_MK + M_WIDTH
C_MO = C_MV + M_WIDTH
C_AK = C_MO + M_WIDTH
NAT_WIDTH = C_AK + A_WIDTH
N_BIAS_TILES = 6

PROJ_TM = 512
MLSTM_T = 512
MLSTM_CHUNK = 128
MERGE_TM = 512
FFN_TM = 512
FFN_CHUNK = 256
ROUTER_TM = 512
MOE_TM = 512
MOE_TF = 1792
SCATTER_R = 1024


def _params(*sem):
    return pltpu.CompilerParams(dimension_semantics=sem, vmem_limit_bytes=VMEM_LIMIT)


def _const_spec(shape):
    nd = len(shape)
    return pl.BlockSpec(shape, lambda *_: (0,) * nd)


def _rms(x, g):
    return x * lax.rsqrt(jnp.mean(x * x, axis=-1, keepdims=True) + EPS) * g


ROW_SLAB = 128


def _in_proj_kernel(x_ref, g_ref, wn_ref, wt_ref, wg_ref, cw_ref, cb_ref, qg_ref, kg_ref, z_ref, zt_ref, gt_ref,
                    hn_s, qk_buf, k_buf, *, tiles_per_seq):
    tm = x_ref.shape[0]
    DA = A_HEAD_DIM

    @pl.when(pl.program_id(0) % tiles_per_seq == 0)
    def _sequence_start():
        qk_buf[0:SUBLANES, :] = jnp.zeros((SUBLANES, 2 * M_WIDTH), F32)

    hn_s[...] = _rms(x_ref[...], g_ref[...]).astype(BF16)

    def natural(c):
        return jnp.dot(hn_s[...], wn_ref[:, c:c + 512], preferred_element_type=F32)

    def transposed(c):
        return lax.dot_general(wt_ref[c:c + 256, :], hn_s[...], (((1,), (1,)), ((), ())),
                               preferred_element_type=F32)

    qk_buf[SUBLANES:SUBLANES + tm, 0:M_WIDTH] = natural(C_MQ)
    qk_buf[SUBLANES:SUBLANES + tm, M_WIDTH:] = natural(C_MK)
    for r in range(0, tm, ROW_SLAB):
        acc = cb_ref[...] + cw_ref[CONV_WIDTH - 1:CONV_WIDTH, :] * qk_buf[SUBLANES + r:SUBLANES + r + ROW_SLAB, :]
        for j in range(CONV_WIDTH - 1):
            off = SUBLANES - (CONV_WIDTH - 1) + j + r
            acc = acc + cw_ref[j:j + 1, :] * qk_buf[off:off + ROW_SLAB, :]
        qk = acc * jax.nn.sigmoid(acc)
        z_ref[r:r + ROW_SLAB, C_MQ:C_MQ + M_WIDTH] = qk[:, :M_WIDTH].astype(BF16)
        z_ref[r:r + ROW_SLAB, C_MK:C_MK + M_WIDTH] = (qk[:, M_WIDTH:] * (M_HEAD_DIM ** -0.5)).astype(BF16)
    qk_buf[0:SUBLANES, :] = qk_buf[tm:tm + SUBLANES, :]

    k_buf[...] = natural(C_AK)
    head0 = lax.broadcasted_iota(jnp.int32, (ROW_SLAB, LANES), 1) < DA
    for r in range(0, tm, ROW_SLAB):
        for t in range(A_WIDTH // LANES):
            cols = slice(t * LANES, (t + 1) * LANES)
            kt = k_buf[r:r + ROW_SLAB, cols]
            k2 = kt * kt
            s0 = jnp.sum(jnp.where(head0, k2, 0.0), axis=-1, keepdims=True)
            s1 = jnp.sum(jnp.where(head0, 0.0, k2), axis=-1, keepdims=True)
            kn = kt * jnp.where(head0, lax.rsqrt(s0 / DA + EPS), lax.rsqrt(s1 / DA + EPS)) * kg_ref[:, cols]
            z_ref[r:r + ROW_SLAB, C_AK + t * LANES:C_AK + (t + 1) * LANES] = kn.astype(BF16)

    for c in range(0, A_WIDTH, 256):
        q3 = transposed(c).reshape(256 // DA, DA, tm)
        inv = lax.rsqrt(jnp.mean(q3 * q3, axis=1, keepdims=True) + EPS)
        qn = q3 * inv * (qg_ref[...] * (DA ** -0.5 * LOG2E))
        zt_ref[c:c + 256, :] = qn.reshape(256, tm).astype(BF16)
    for c in (C_GA, C_GA + 512, C_GB, C_GB + 512, C_MV, C_MO):
        z_ref[:, c:c + 512] = natural(c).astype(BF16)
    for c in range(A_WIDTH, 2 * A_WIDTH, 256):
        zt_ref[c:c + 256, :] = transposed(c).astype(BF16)
    gt_ref[...] = jnp.dot(hn_s[...], wg_ref[...], preferred_element_type=F32)


def _in_proj(x2, g, wn, wt, wg, conv_w, conv_b, qg, kg, tiles_per_seq):
    n = x2.shape[0]
    tm = PROJ_TM
    return pl.pallas_call(
        functools.partial(_in_proj_kernel, tiles_per_seq=tiles_per_seq),
        grid=(n // tm,),
        in_specs=[
            pl.BlockSpec((tm, D_MODEL), lambda i: (i, 0)),
            _const_spec((1, D_MODEL)),
            _const_spec((D_MODEL, NAT_WIDTH)),
            _const_spec((2 * A_WIDTH, D_MODEL)),
            _const_spec((D_MODEL, LANES)),
            _const_spec((CONV_WIDTH, 2 * M_WIDTH)),
            _const_spec((1, 2 * M_WIDTH)),
            _const_spec((1, A_HEAD_DIM, 1)),
            _const_spec((1, A_WIDTH)),
        ],
        out_specs=[
            pl.BlockSpec((tm, NAT_WIDTH), lambda i: (i, 0)),
            pl.BlockSpec((2 * A_WIDTH, tm), lambda i: (0, i)),
            pl.BlockSpec((tm, LANES), lambda i: (i, 0)),
        ],
        out_shape=[
            jax.ShapeDtypeStruct((n, NAT_WIDTH), BF16),
            jax.ShapeDtypeStruct((2 * A_WIDTH, n), BF16),
            jax.ShapeDtypeStruct((n, LANES), F32),
        ],
        scratch_shapes=[pltpu.VMEM((tm, D_MODEL), BF16), pltpu.VMEM((tm + SUBLANES, 2 * M_WIDTH), F32),
                        pltpu.VMEM((tm, A_WIDTH), F32)],
        compiler_params=_params("arbitrary"),
        name="in_proj",
    )(x2, g, wn, wt, wg, conv_w, conv_b, qg, kg)


def _mlstm_kernel(zq_ref, zk_ref, zv_ref, zo_ref, gt_ref, gb_ref, ng_ref, o_ref, c_s, m_s, *, t_blk, chunk):
    L = chunk
    DH = M_HEAD_DIM
    assert L == LANES and DH == LANES

    @pl.when(pl.program_id(1) == 0)
    def _init():
        c_s[...] = jnp.zeros_like(c_s)
        m_s[...] = jnp.zeros_like(m_s)

    row = lax.broadcasted_iota(jnp.int32, (L, L), 0)
    col = lax.broadcasted_iota(jnp.int32, (L, L), 1)
    causal = col <= row
    tri = causal.astype(F32)
    lane = lax.broadcasted_iota(jnp.int32, (L, LANES), 1)
    ones_blk = jnp.ones((L, LANES), BF16)

    def chunk_body(c, carry):
        r0 = c * L
        g_pre = gt_ref[pl.ds(r0, L), :] + gb_ref[...]
        log_f = jnp.minimum(g_pre, 0.0) - jnp.log1p(jnp.exp(-jnp.abs(g_pre)))
        bcum = jnp.dot(tri, log_f, precision=HIGHEST, preferred_element_type=F32)
        gm = jnp.where(lane < M_HEADS, g_pre, bcum)
        gm_t = gm.T
        c_all = [c_s[h] for h in range(M_HEADS)]
        m_all = m_s[...]
        heads = range(M_HEADS)
        hs = [slice(h * DH, (h + 1) * DH) for h in heads]
        qh = [zq_ref[pl.ds(r0, L), hs[h]] for h in heads]
        kb = [zk_ref[pl.ds(r0, L), hs[h]] for h in heads]
        kf = [k.astype(F32) for k in kb]
        v_ext = [jnp.concatenate([zv_ref[pl.ds(r0, L), hs[h]], ones_blk], axis=1) for h in heads]
        qk = [lax.dot_general(qh[h], kb[h], (((1,), (1,)), ((), ())), preferred_element_type=F32)
              for h in heads]
        inter = [jnp.dot(qh[h], c_all[h].astype(BF16), preferred_element_type=F32) for h in heads]
        mt, w_state, s_bf, kw_t, decay, m_new = [], [], [], [], [], []
        for h in heads:
            bb = jnp.broadcast_to(gm[:, M_HEADS + h:M_HEADS + h + 1], (L, LANES))
            ii = jnp.broadcast_to(gm[:, h:h + 1], (L, LANES))
            bb_row = gm_t[M_HEADS + h:M_HEADS + h + 1, :]
            ii_row = gm_t[h:h + 1, :]
            m_old = m_all[h:h + 1, :]
            dlog = jnp.where(causal, bb - (bb_row - ii_row), -jnp.inf)
            a = bb + m_old
            mt.append(jnp.maximum(a, jnp.broadcast_to(jnp.max(dlog, axis=-1, keepdims=True), (L, LANES))))
            w_state.append(jnp.exp(a - mt[h]))
            s_bf.append((qk[h] * jnp.exp(dlog - mt[h])).astype(BF16))
            b_last = bb[L - 1:L, :]
            g = b_last - bb + ii
            m_new.append(jnp.maximum(b_last + m_old, jnp.max(g, axis=0, keepdims=True)))
            decay.append(jnp.exp(b_last + m_old - m_new[h]))
            kw_t.append((kf[h] * jnp.exp(g - m_new[h])).T.astype(BF16))
        intra = [jnp.dot(s_bf[h], v_ext[h], preferred_element_type=F32) for h in heads]
        upd = [jnp.dot(kw_t[h], v_ext[h], preferred_element_type=F32) for h in heads]
        for h in heads:
            num = intra[h][:, :DH] + w_state[h] * inter[h][:, :DH]
            den = intra[h][:, DH:] + w_state[h] * inter[h][:, DH:]
            h_t = num / jnp.maximum(jnp.abs(den), jnp.exp(-mt[h]))
            hc = jax.nn.sigmoid(zo_ref[pl.ds(r0, L), hs[h]].astype(F32)) * h_t
            o_ref[pl.ds(r0, L), hs[h]] = _rms(hc, ng_ref[:, hs[h]]).astype(BF16)
        for h in heads:
            c_s[h] = jnp.concatenate([decay[h], decay[h]], axis=1) * c_all[h] + upd[h]
            m_s[h:h + 1, :] = m_new[h]
        return carry

    for c in range(t_blk // L):
        chunk_body(c, 0)


def _mlstm(z, gates, gate_b, norm_g, batch, seq):
    n = batch * seq
    t = MLSTM_T
    nt = seq // t
    row_blk = lambda b, s: b * nt + s
    zspec = lambda cb: pl.BlockSpec((t, M_WIDTH), lambda b, s: (row_blk(b, s), cb))
    return pl.pallas_call(
        functools.partial(_mlstm_kernel, t_blk=t, chunk=MLSTM_CHUNK),
        grid=(batch, nt),
        in_specs=[
            zspec(C_MQ // M_WIDTH), zspec(C_MK // M_WIDTH), zspec(C_MV // M_WIDTH), zspec(C_MO // M_WIDTH),
            pl.BlockSpec((t, LANES), lambda b, s: (row_blk(b, s), 0)),
            _const_spec((1, LANES)),
            _const_spec((1, M_WIDTH)),
        ],
        out_specs=pl.BlockSpec((t, M_WIDTH), lambda b, s: (row_blk(b, s), 0)),
        out_shape=jax.ShapeDtypeStruct((n, M_WIDTH), BF16),
        scratch_shapes=[
            pltpu.VMEM((M_HEADS, M_HEAD_DIM, M_HEAD_DIM + LANES), F32),
            pltpu.VMEM((SUBLANES, LANES), F32),
        ],
        compiler_params=_params("arbitrary", "arbitrary"),
        name="mlstm",
    )(z, z, z, z, gates, gate_b, norm_g)


MOBA_V_ROWS = A_HEAD_DIM + 16
MOBA_ROWB_ROWS = 24
MOBA_GROUP = 34
LOG2E = math.log2(math.e)
FAST_SOFTMAX_MIN_DENOM = 1e-25


def _moba_items(n_blocks):
    items = []
    for qb in range(n_blocks):
        items.append((qb, qb, 0, 0))
        for j in range(qb):
            items.append((qb, j, min(qb - j, N_BIAS_TILES - 1), j + 1))
    n_groups = -(-len(items) // MOBA_GROUP)
    noop = (n_blocks - 1, 0, 0, MOBA_ROWB_ROWS - 1)
    items += [noop] * (n_groups * MOBA_GROUP + 2 - len(items))
    return n_groups, [jnp.asarray([it[c] for it in items], jnp.int32) for c in range(4)]


def _moba_kernel(it_q, it_blk, it_tile, it_row, qt_ref, k_ref, vt_ref, bias_ref, bound_ref, o_ref,
                 vt_s, kmean_s, qh_s, rowb_s, acc_s, st_a, st_b, p_a, p_b, *, n_blocks, n_groups):
    BS = MOBA_BLOCK
    DA = A_HEAD_DIM
    seq = n_blocks * BS

    ones_row = jnp.where(lax.broadcasted_iota(jnp.int32, (16, BS), 0) == 0, 1.0, 0.0).astype(BF16)
    for j in range(n_blocks):
        kmean_s[j:j + 1, :] = jnp.mean(k_ref[j * BS:(j + 1) * BS, :].astype(F32), axis=0, keepdims=True)
        for h in range(2):
            vt_s[j, h, 0:DA, :] = vt_ref[h * DA:(h + 1) * DA, j * BS:(j + 1) * BS]
            vt_s[j, h, DA:MOBA_V_ROWS, :] = ones_row

    qn = qt_ref[...].astype(F32)
    head0_q = lax.broadcasted_iota(jnp.int32, qn.shape, 0) < DA

    blk = lax.broadcasted_iota(jnp.int32, (n_blocks, BS), 0)
    lane_m = lax.broadcasted_iota(jnp.int32, (n_blocks, LANES), 1)
    kmean = kmean_s[...]
    kmean2 = jnp.concatenate([jnp.where(lane_m < DA, kmean, 0.0), jnp.where(lane_m < DA, 0.0, kmean)], axis=0)
    k_hi = kmean2.astype(BF16)
    k_mid = (kmean2 - k_hi.astype(F32)).astype(BF16)
    k_lo = (kmean2 - k_hi.astype(F32) - k_mid.astype(F32)).astype(BF16)
    split = jnp.dot(jnp.concatenate([k_hi, k_mid, k_lo], axis=0), qt_ref[...], preferred_element_type=F32)
    gates = split[0:2 * n_blocks] + split[2 * n_blocks:4 * n_blocks] + split[4 * n_blocks:6 * n_blocks]
    for h in range(2):
        bound = bound_ref[h, :, 0:1]
        hmask_q = head0_q if h == 0 else jnp.logical_not(head0_q)
        qh = jnp.where(hmask_q, qn, 0.0).astype(BF16)
        for qb in range(n_blocks):
            cols = slice(qb * BS, (qb + 1) * BS)
            past = blk < qb
            gate = jnp.where(past, gates[h * n_blocks:(h + 1) * n_blocks, cols], -jnp.inf)
            rank = jnp.zeros((n_blocks, BS), jnp.int32)
            for j2 in range(qb):
                other = gate[j2:j2 + 1, :]
                beats = (other > gate) | ((other == gate) & (j2 < blk))
                rank = rank + beats.astype(jnp.int32)
            selb = jnp.where(past & (rank < MOBA_TOPK), 0.0, NEG) - bound
            qh_s[qb, h] = qh[:, cols]
            rowb_s[qb, h, 0:1, :] = jnp.broadcast_to(-bound, (1, BS))
            rowb_s[qb, h, 1:n_blocks + 1, :] = selb
            rowb_s[qb, h, n_blocks + 1:, :] = jnp.full((MOBA_ROWB_ROWS - n_blocks - 1, BS), NEG, F32)
    acc_s[...] = jnp.zeros_like(acc_s)

    def scores(h, qb, j, tile, row):
        kj = k_ref[pl.ds(pl.multiple_of(j * BS, BS), BS), :]
        st = jnp.dot(kj, qh_s[qb, h], preferred_element_type=F32)
        return st + bias_ref[h, tile] + rowb_s[qb, h, pl.ds(row, 1), :]

    def pv(h, j, p):
        return jnp.dot(vt_s[j, h], p, preferred_element_type=F32)

    def stage_scores(i, st_ref):
        for h in range(2):
            st_ref[h] = scores(h, it_q[i], it_blk[i], it_tile[i], it_row[i])

    def stage_exp(st_ref, p_ref):
        for h in range(2):
            p_ref[h] = jnp.exp2(st_ref[h]).astype(BF16)

    def stage_pv(i, p_ref):
        for h in range(2):
            acc_s[it_q[i], h] += pv(h, it_blk[i], p_ref[h])

    def group(m, carry):
        for u in range(0, MOBA_GROUP, 2):
            i = MOBA_GROUP * m + u
            stage_pv(i, p_a)
            stage_scores(i + 2, st_a)
            stage_exp(st_b, p_b)
            stage_pv(i + 1, p_b)
            stage_scores(i + 3, st_b)
            stage_exp(st_a, p_a)
        return carry

    stage_scores(0, st_a)
    stage_exp(st_a, p_a)
    stage_scores(1, st_b)
    lax.fori_loop(0, n_groups, group, 0)

    def finish(qb, l_min):
        outs = []
        for h in range(2):
            acc = acc_s[qb, h]
            l = acc[DA:DA + 1, :]
            outs.append(acc[0:DA, :] / l)
            l_min = jnp.minimum(l_min, jnp.min(l))
        o_ref[qb * BS:(qb + 1) * BS, :] = jnp.concatenate(outs, axis=0).T.astype(BF16)
        return l_min

    l_min = jnp.float32(jnp.inf)
    for qb in range(n_blocks):
        l_min = finish(qb, l_min)

    def online(qb, carry):
        state = []
        for h in range(2):
            st = scores(h, qb, qb, 0, 0)
            m = jnp.max(st, axis=0, keepdims=True)
            p = jnp.exp2(st - m)
            state += [m, jnp.sum(p, axis=0, keepdims=True), pv(h, qb, p.astype(BF16))[0:DA, :]]

        def past_body(j, state):
            tile = jnp.minimum(qb - j, N_BIAS_TILES - 1)
            new = []
            for h in range(2):
                m_old, l_old, acc_old = state[3 * h:3 * h + 3]
                st = scores(h, qb, j, tile, j + 1)
                m_new = jnp.maximum(m_old, jnp.max(st, axis=0, keepdims=True))
                alpha = jnp.exp2(m_old - m_new)
                p = jnp.exp2(st - m_new)
                new += [m_new, alpha * l_old + jnp.sum(p, axis=0, keepdims=True),
                        alpha * acc_old + pv(h, j, p.astype(BF16))[0:DA, :]]
            return tuple(new)

        state = lax.fori_loop(0, qb, past_body, tuple(state))
        outs = [state[3 * h + 2] / state[3 * h + 1] for h in range(2)]
        o_ref[pl.ds(pl.multiple_of(qb * BS, BS), BS), :] = jnp.concatenate(outs, axis=0).T.astype(BF16)
        return carry

    @pl.when(l_min < FAST_SOFTMAX_MIN_DENOM)
    def _redo():
        lax.fori_loop(0, n_blocks, online, 0)


def _moba_logit_bound(rel_bias, q_norm_g, k_norm_g):
    qk = A_HEAD_DIM * jnp.max(jnp.abs(q_norm_g)) * jnp.max(jnp.abs(k_norm_g)) * (A_HEAD_DIM ** -0.5) * 1.02
    b = (qk + jnp.max(rel_bias, axis=0)) * LOG2E
    return jnp.broadcast_to(b[:, None, None], (A_HEADS, 1, LANES)).astype(F32)


def _moba(z, zt, bias_tiles, bound, batch, seq):
    n = batch * seq
    nb = seq // MOBA_BLOCK
    bs = MOBA_BLOCK
    hp = A_HEADS // 2
    w2 = 2 * A_HEAD_DIM
    assert nb + 2 <= MOBA_ROWB_ROWS
    n_groups, items = _moba_items(nb)
    grid_spec = pltpu.PrefetchScalarGridSpec(
        num_scalar_prefetch=len(items),
        grid=(batch, hp),
        in_specs=[
            pl.BlockSpec((w2, seq), lambda b, p, *_: (p, b)),
            pl.BlockSpec((seq, w2), lambda b, p, *_: (b, C_AK // w2 + p)),
            pl.BlockSpec((w2, seq), lambda b, p, *_: (hp + p, b)),
            pl.BlockSpec((2, N_BIAS_TILES, bs, bs), lambda b, p, *_: (p, 0, 0, 0)),
            pl.BlockSpec((2, 1, LANES), lambda b, p, *_: (p, 0, 0)),
        ],
        out_specs=pl.BlockSpec((seq, w2), lambda b, p, *_: (b, p)),
        scratch_shapes=[
            pltpu.VMEM((nb, 2, MOBA_V_ROWS, bs), BF16),
            pltpu.VMEM((nb, w2), F32),
            pltpu.VMEM((nb, 2, w2, bs), BF16),
            pltpu.VMEM((nb, 2, MOBA_ROWB_ROWS, bs), F32),
            pltpu.VMEM((nb, 2, MOBA_V_ROWS, bs), F32),
            pltpu.VMEM((2, bs, bs), F32),
            pltpu.VMEM((2, bs, bs), F32),
            pltpu.VMEM((2, bs, bs), BF16),
            pltpu.VMEM((2, bs, bs), BF16),
        ],
    )
    return pl.pallas_call(
        functools.partial(_moba_kernel, n_blocks=nb, n_groups=n_groups),
        grid_spec=grid_spec,
        out_shape=jax.ShapeDtypeStruct((n, A_WIDTH), BF16),
        compiler_params=_params("arbitrary", "arbitrary"),
        name="moba",
    )(*items, zt, z, zt, bias_tiles, bound)


def _t5_bucket(dist):
    n = jnp.maximum(dist, 0)
    max_exact = REL_BUCKETS // 2
    log_ratio = jnp.log(jnp.maximum(n, max_exact).astype(F32) / max_exact) / math.log(REL_MAX_DIST / max_exact)
    large = max_exact + (log_ratio * (REL_BUCKETS - max_exact)).astype(jnp.int32)
    large = jnp.minimum(large, REL_BUCKETS - 1)
    return jnp.where(n < max_exact, n, large)


def _bias_tiles_kernel(rb_ref, bucket_ref, o_ref):
    bucket = bucket_ref[0]
    hit = [bucket == b for b in range(REL_BUCKETS)]
    tk = lax.broadcasted_iota(jnp.int32, bucket.shape, 0)
    tq = lax.broadcasted_iota(jnp.int32, bucket.shape, 1)
    masked = (tk > tq) & (pl.program_id(0) == 0)
    for h in range(A_HEADS):
        acc = jnp.zeros(bucket.shape, F32)
        for b in range(REL_BUCKETS):
            acc = jnp.where(hit[b], rb_ref[b, h], acc)
        o_ref[h, 0] = jnp.where(masked, NEG, acc * LOG2E)


def _moba_bias_tiles(rel_bias):
    assert (N_BIAS_TILES - 1) * MOBA_BLOCK - (MOBA_BLOCK - 1) >= REL_MAX_DIST
    tk = jnp.arange(MOBA_BLOCK)[None, :, None]
    tq = jnp.arange(MOBA_BLOCK)[None, None, :]
    diff = jnp.arange(N_BIAS_TILES)[:, None, None]
    bucket = _t5_bucket(diff * MOBA_BLOCK + tq - tk).astype(jnp.int32)
    return pl.pallas_call(
        _bias_tiles_kernel,
        grid=(N_BIAS_TILES,),
        in_specs=[
            pl.BlockSpec(memory_space=pltpu.SMEM),
            pl.BlockSpec((1, MOBA_BLOCK, MOBA_BLOCK), lambda t: (t, 0, 0)),
        ],
        out_specs=pl.BlockSpec((A_HEADS, 1, MOBA_BLOCK, MOBA_BLOCK), lambda t: (0, t, 0, 0)),
        out_shape=jax.ShapeDtypeStruct((A_HEADS, N_BIAS_TILES, MOBA_BLOCK, MOBA_BLOCK), F32),
        compiler_params=_params("arbitrary"),
        name="moba_bias_tiles",
    )(rel_bias.astype(F32), bucket)


def _merge_kernel(x_ref, ha_ref, ob_ref, ga_ref, gb_ref, wa_ref, wb_ref, wo_ref, o_ref):
    ya = jnp.dot(ha_ref[...], wa_ref[...], preferred_element_type=F32)
    yb = jnp.dot(ob_ref[...], wb_ref[...], preferred_element_type=F32)
    y = jax.nn.sigmoid(ga_ref[...].astype(F32)) * ya + jax.nn.sigmoid(gb_ref[...].astype(F32)) * yb
    o_ref[...] = x_ref[...] + jnp.dot(y.astype(BF16), wo_ref[...], preferred_element_type=F32)


def _merge(x2, ha, ob, z, wa, wb, wo):
    n = x2.shape[0]
    tm = MERGE_TM
    return pl.pallas_call(
        _merge_kernel,
        grid=(n // tm,),
        in_specs=[
            pl.BlockSpec((tm, D_MODEL), lambda i: (i, 0)),
            pl.BlockSpec((tm, M_WIDTH), lambda i: (i, 0)),
            pl.BlockSpec((tm, A_WIDTH), lambda i: (i, 0)),
            pl.BlockSpec((tm, D_MODEL), lambda i: (i, C_GA // D_MODEL)),
            pl.BlockSpec((tm, D_MODEL), lambda i: (i, C_GB // D_MODEL)),
            _const_spec((M_WIDTH, D_MODEL)),
            _const_spec((A_WIDTH, D_MODEL)),
            _const_spec((D_MODEL, D_MODEL)),
        ],
        out_specs=pl.BlockSpec((tm, D_MODEL), lambda i: (i, 0)),
        out_shape=jax.ShapeDtypeStruct((n, D_MODEL), F32),
        compiler_params=_params("arbitrary"),
        name="merge",
    )(x2, ha, ob, z, z, wa, wb, wo)


def _ffn_kernel(x_ref, g_ref, wg_ref, wu_ref, wd_ref, o_ref, *, d_ff):
    x = x_ref[...]
    hn = _rms(x, g_ref[...]).astype(BF16)
    acc = x
    for c in range(0, d_ff, FFN_CHUNK):
        g = jnp.dot(hn, wg_ref[:, c:c + FFN_CHUNK], preferred_element_type=F32)
        u = jnp.dot(hn, wu_ref[:, c:c + FFN_CHUNK], preferred_element_type=F32)
        a = (g * jax.nn.sigmoid(g) * u).astype(BF16)
        acc = acc + jnp.dot(a, wd_ref[c:c + FFN_CHUNK, :], preferred_element_type=F32)
    o_ref[...] = acc


def _ffn(x2, g, wg, wu, wd):
    n = x2.shape[0]
    d_ff = wg.shape[1]
    assert d_ff % FFN_CHUNK == 0
    tm = FFN_TM
    return pl.pallas_call(
        functools.partial(_ffn_kernel, d_ff=d_ff),
        grid=(n // tm,),
        in_specs=[
            pl.BlockSpec((tm, D_MODEL), lambda i: (i, 0)),
            _const_spec((1, D_MODEL)),
            _const_spec((D_MODEL, d_ff)),
            _const_spec((D_MODEL, d_ff)),
            _const_spec((d_ff, D_MODEL)),
        ],
        out_specs=pl.BlockSpec((tm, D_MODEL), lambda i: (i, 0)),
        out_shape=jax.ShapeDtypeStruct((n, D_MODEL), F32),
        compiler_params=_params("arbitrary"),
        name="ffn_dense",
    )(x2, g, wg, wu, wd)


META_W0, META_W1, META_E0, META_E1, META_R0, META_R1 = range(6)


def _router_kernel(x_ref, g_ref, rw_ref, h_ref, meta_ref, cnt_ref, carry_s):
    @pl.when(pl.program_id(0) == 0)
    def _init():
        carry_s[...] = jnp.zeros_like(carry_s)

    h = _rms(x_ref[...], g_ref[...])
    h_ref[...] = h
    tm = h.shape[0]
    logits = jnp.dot(h, rw_ref[...], precision=HIGHEST, preferred_element_type=F32)
    lane = lax.broadcasted_iota(jnp.int32, (tm, LANES), 1)
    lg = jnp.where(lane < N_EXPERTS, logits, -jnp.inf)
    m1 = jnp.max(lg, axis=-1, keepdims=True)
    i1 = jnp.min(jnp.where(lg == m1, lane, LANES), axis=-1, keepdims=True)
    lg2 = jnp.where(lane == i1, -jnp.inf, lg)
    m2 = jnp.max(lg2, axis=-1, keepdims=True)
    i2 = jnp.min(jnp.where(lg2 == m2, lane, LANES), axis=-1, keepdims=True)
    e = jnp.exp(m2 - m1)
    w1 = 1.0 / (1.0 + e)
    w2 = e / (1.0 + e)
    hit1 = lane == i1
    hit2 = lane == i2
    onehot = jnp.where(hit1 | hit2, 1.0, 0.0)
    r = lax.broadcasted_iota(jnp.int32, (tm, tm), 0)
    c = lax.broadcasted_iota(jnp.int32, (tm, tm), 1)
    before = jnp.where(c < r, 1.0, 0.0).astype(BF16)
    carry = carry_s[0:1, :]
    pref = jnp.dot(before, onehot.astype(BF16), preferred_element_type=F32) + carry
    r1 = jnp.sum(jnp.where(hit1, pref, 0.0), axis=-1, keepdims=True)
    r2 = jnp.sum(jnp.where(hit2, pref, 0.0), axis=-1, keepdims=True)
    new_carry = carry + jnp.sum(onehot, axis=0, keepdims=True)
    carry_s[...] = jnp.broadcast_to(new_carry, carry_s.shape)
    cnt_ref[...] = jnp.broadcast_to(new_carry, cnt_ref.shape)
    meta = jnp.zeros((tm, LANES), F32)
    for idx, val in ((META_W0, w1), (META_W1, w2), (META_E0, i1.astype(F32)), (META_E1, i2.astype(F32)),
                     (META_R0, r1), (META_R1, r2)):
        meta = jnp.where(lane == idx, val, meta)
    meta_ref[...] = meta


def _router(x2, g, rw):
    n = x2.shape[0]
    tm = ROUTER_TM
    return pl.pallas_call(
        _router_kernel,
        grid=(n // tm,),
        in_specs=[
            pl.BlockSpec((tm, D_MODEL), lambda i: (i, 0)),
            _const_spec((1, D_MODEL)),
            _const_spec((D_MODEL, LANES)),
        ],
        out_specs=[
            pl.BlockSpec((tm, D_MODEL), lambda i: (i, 0)),
            pl.BlockSpec((tm, LANES), lambda i: (i, 0)),
            _const_spec((SUBLANES, LANES)),
        ],
        out_shape=[
            jax.ShapeDtypeStruct((n, D_MODEL), F32),
            jax.ShapeDtypeStruct((n, LANES), F32),
            jax.ShapeDtypeStruct((SUBLANES, LANES), F32),
        ],
        scratch_shapes=[pltpu.VMEM((SUBLANES, LANES), F32)],
        compiler_params=_params("arbitrary"),
        name="moe_router",
    )(x2, g, rw)


def _row_copy(src_ref, src_row, dst_ref, dst_row, sem):
    return pltpu.make_async_copy(src_ref.at[pl.ds(src_row, 1)], dst_ref.at[pl.ds(dst_row, 1)], sem)


def _scatter_kernel(pos_ref, pad_ref, h_ref, xs_ref, zero_s, sem, *, rows, pads):
    @pl.when(pl.program_id(0) == 0)
    def _init():
        zero_s[...] = jnp.zeros_like(zero_s)

    for r in range(rows):
        _row_copy(h_ref, r, xs_ref, pos_ref[0, 0, r], sem).start(priority=0)
        _row_copy(h_ref, r, xs_ref, pos_ref[0, 0, rows + r], sem).start(priority=1)
    for r in range(pads):
        _row_copy(zero_s, 0, xs_ref, pad_ref[0, 0, r], sem).start(priority=r % 2)
    for _ in range(2):
        pltpu.make_async_copy(h_ref, xs_ref.at[pl.ds(0, rows)], sem).wait()
    pltpu.make_async_copy(h_ref.at[pl.ds(0, pads)], xs_ref.at[pl.ds(0, pads)], sem).wait()


def _scatter(pos, pad_pos, h, n_rows):
    n = h.shape[0]
    r = SCATTER_R
    steps = n // r
    pads = pad_pos.shape[0] // steps
    assert pads * steps == pad_pos.shape[0] and 2 * n + pad_pos.shape[0] == n_rows
    return pl.pallas_call(
        functools.partial(_scatter_kernel, rows=r, pads=pads),
        grid=(steps,),
        in_specs=[
            pl.BlockSpec((1, 1, 2 * r), lambda i: (i, 0, 0), memory_space=pltpu.SMEM),
            pl.BlockSpec((1, 1, pads), lambda i: (i, 0, 0), memory_space=pltpu.SMEM),
            pl.BlockSpec((r, D_MODEL), lambda i: (i, 0)),
        ],
        out_specs=pl.BlockSpec(memory_space=pl.ANY),
        out_shape=jax.ShapeDtypeStruct((n_rows, D_MODEL), F32),
        scratch_shapes=[pltpu.VMEM((SUBLANES, D_MODEL), F32), pltpu.SemaphoreType.DMA(())],
        compiler_params=_params("arbitrary"),
        name="moe_scatter",
    )(pos, pad_pos.reshape(steps, 1, pads), h)


def _moe_kernel(blk_ref, exp_ref, valid_ref, xs_ref, wg_ref, wu_ref, wd_ref, y_ref, xb_s, acc_s, *, n_ff):
    del blk_ref, exp_ref
    i = pl.program_id(0)
    f = pl.program_id(1)

    @pl.when((valid_ref[i] == 0) & (f == n_ff - 1))
    def _idle():
        y_ref[...] = jnp.zeros_like(y_ref)

    @pl.when(valid_ref[i] == 1)
    def _tile():
        @pl.when(f == 0)
        def _cast():
            xb_s[...] = xs_ref[...].astype(BF16)

        xb = xb_s[...]
        d = None
        for c in range(0, wg_ref.shape[2], FFN_CHUNK):
            g = jnp.dot(xb, wg_ref[0, :, c:c + FFN_CHUNK], preferred_element_type=F32)
            u = jnp.dot(xb, wu_ref[0, :, c:c + FFN_CHUNK], preferred_element_type=F32)
            a = (g * jax.nn.sigmoid(g) * u).astype(BF16)
            dc = jnp.dot(a, wd_ref[0, c:c + FFN_CHUNK, :], preferred_element_type=F32)
            d = dc if d is None else d + dc

        @pl.when(f == 0)
        def _first():
            acc_s[...] = d

        @pl.when(f > 0)
        def _rest():
            acc_s[...] += d

        @pl.when(f == n_ff - 1)
        def _out():
            y_ref[...] = acc_s[...]


def _moe_experts(tile_blk, tile_exp, tile_valid, xs, wg, wu, wd):
    rows = xs.shape[0]
    d_ff = wg.shape[2]
    tm, tf = MOE_TM, MOE_TF
    n_ff = d_ff // tf
    assert rows % tm == 0 and d_ff % tf == 0
    grid_spec = pltpu.PrefetchScalarGridSpec(
        num_scalar_prefetch=3,
        grid=(rows // tm, n_ff),
        in_specs=[
            pl.BlockSpec((tm, D_MODEL), lambda i, f, blk, ex, va: (blk[i], 0)),
            pl.BlockSpec((1, D_MODEL, tf), lambda i, f, blk, ex, va: (ex[i], 0, f)),
            pl.BlockSpec((1, D_MODEL, tf), lambda i, f, blk, ex, va: (ex[i], 0, f)),
            pl.BlockSpec((1, tf, D_MODEL), lambda i, f, blk, ex, va: (ex[i], f, 0)),
        ],
        out_specs=pl.BlockSpec((tm, D_MODEL), lambda i, f, blk, ex, va: (i, 0)),
        scratch_shapes=[pltpu.VMEM((tm, D_MODEL), BF16), pltpu.VMEM((tm, D_MODEL), F32)],
    )
    return pl.pallas_call(
        functools.partial(_moe_kernel, n_ff=n_ff),
        grid_spec=grid_spec,
        out_shape=jax.ShapeDtypeStruct((rows, D_MODEL), F32),
        compiler_params=_params("arbitrary", "arbitrary"),
        name="moe_experts",
    )(tile_blk, tile_exp, tile_valid, xs, wg, wu, wd)


def _combine_kernel(pos_ref, x_ref, meta_ref, ys_ref, o_ref, ybuf, sem, *, rows):
    for r in range(rows):
        _row_copy(ys_ref, pos_ref[0, 0, r], ybuf, r, sem).start(priority=0)
        _row_copy(ys_ref, pos_ref[0, 0, rows + r], ybuf, rows + r, sem).start(priority=1)
    pltpu.make_async_copy(ys_ref.at[pl.ds(0, 2 * rows)], ybuf, sem).wait()
    meta = meta_ref[...]
    w0 = meta[:, META_W0:META_W0 + 1]
    w1 = meta[:, META_W1:META_W1 + 1]
    o_ref[...] = x_ref[...] + w0 * ybuf[0:rows, :] + w1 * ybuf[rows:2 * rows, :]


def _combine(pos, x2, meta, ys):
    n = x2.shape[0]
    r = SCATTER_R
    return pl.pallas_call(
        functools.partial(_combine_kernel, rows=r),
        grid=(n // r,),
        in_specs=[
            pl.BlockSpec((1, 1, 2 * r), lambda i: (i, 0, 0), memory_space=pltpu.SMEM),
            pl.BlockSpec((r, D_MODEL), lambda i: (i, 0)),
            pl.BlockSpec((r, LANES), lambda i: (i, 0)),
            pl.BlockSpec(memory_space=pl.ANY),
        ],
        out_specs=pl.BlockSpec((r, D_MODEL), lambda i: (i, 0)),
        out_shape=jax.ShapeDtypeStruct((n, D_MODEL), F32),
        scratch_shapes=[pltpu.VMEM((2 * r, D_MODEL), F32), pltpu.SemaphoreType.DMA(())],
        compiler_params=_params("arbitrary"),
        name="moe_combine",
    )(pos, x2, meta, ys)


def _moe(x2, g, router_w, wg, wu, wd):
    n = x2.shape[0]
    tm = MOE_TM
    rw = jnp.zeros((D_MODEL, LANES), F32).at[:, :N_EXPERTS].set(router_w)
    h, meta, cnt = _router(x2, g, rw)
    counts = cnt[0, :N_EXPERTS].astype(jnp.int32)
    padded = ((counts + tm - 1) // tm) * tm
    ends = jnp.cumsum(padded)
    offs = ends - padded
    n_tiles = (2 * n) // tm + N_EXPERTS
    e0 = meta[:, META_E0].astype(jnp.int32)
    e1 = meta[:, META_E1].astype(jnp.int32)
    pos0 = offs[e0] + meta[:, META_R0].astype(jnp.int32)
    pos1 = offs[e1] + meta[:, META_R1].astype(jnp.int32)
    r = SCATTER_R
    pos = jnp.concatenate([pos0.reshape(n // r, 1, r), pos1.reshape(n // r, 1, r)], axis=-1)
    starts = jnp.arange(n_tiles, dtype=jnp.int32) * tm
    n_valid = ends[-1] // tm
    tile_valid = (starts < ends[-1]).astype(jnp.int32)
    tile_blk = jnp.minimum(jnp.arange(n_tiles, dtype=jnp.int32), n_valid - 1)
    tile_exp = jnp.sum((tile_blk[:, None] * tm >= ends[None, :]).astype(jnp.int32), axis=1)
    n_rows = n_tiles * tm
    seg_base = jnp.concatenate([offs + counts, ends[-1:]])
    seg_len = jnp.concatenate([padded - counts, n_rows - ends[-1:]])
    seg_end = jnp.cumsum(seg_len)
    k = jnp.arange(n_rows - 2 * n, dtype=jnp.int32)
    seg = jnp.sum((k[:, None] >= seg_end[None, :]).astype(jnp.int32), axis=1)
    pad_pos = seg_base[seg] + k - (seg_end - seg_len)[seg]
    xs = _scatter(pos, pad_pos, h, n_rows)
    ys = _moe_experts(tile_blk, tile_exp, tile_valid, xs, wg, wu, wd)
    return _combine(pos, x2, meta, ys)


def _mixer(x2, batch, seq, rel_bias, bias_tiles, norm_g, w_in, conv_w, conv_b, igate_b, fgate_b, mlstm_norm_g,
           q_norm_g, k_norm_g, w_branch_a, w_branch_b, w_out):
    sizes = (M_WIDTH, M_WIDTH, M_WIDTH, M_WIDTH, M_HEADS, M_HEADS, A_WIDTH, A_WIDTH, A_WIDTH, D_MODEL, D_MODEL)
    cuts = [0]
    for s in sizes:
        cuts.append(cuts[-1] + s)
    mq, mk, mv, mo, mi, mf, aq, ak, av, ga, gb = [w_in[:, cuts[i]:cuts[i + 1]] for i in range(len(sizes))]
    wn = jnp.concatenate([ga, gb, mq, mk, mv, mo, ak], axis=1).astype(BF16)
    wt = jnp.concatenate([aq, av], axis=1).T.astype(BF16)
    wgate = jnp.zeros((D_MODEL, LANES), F32).at[:, :M_HEADS].set(mi).at[:, M_HEADS:2 * M_HEADS].set(mf).astype(BF16)
    gate_b = jnp.zeros((1, LANES), F32).at[0, :M_HEADS].set(igate_b).at[0, M_HEADS:2 * M_HEADS].set(fgate_b)
    z, zt, gates = _in_proj(x2, norm_g.reshape(1, D_MODEL), wn, wt, wgate, conv_w, conv_b.reshape(1, -1),
                            q_norm_g.reshape(1, A_HEAD_DIM, 1), jnp.tile(k_norm_g, A_HEADS).reshape(1, A_WIDTH),
                            seq // PROJ_TM)
    ha = _mlstm(z, gates, gate_b, mlstm_norm_g.reshape(1, -1), batch, seq)
    ob = _moba(z, zt, bias_tiles, _moba_logit_bound(rel_bias, q_norm_g, k_norm_g), batch, seq)
    return _merge(x2, ha, ob, z, w_branch_a.astype(BF16), w_branch_b.astype(BF16), w_out.astype(BF16))


def kernel(x, rel_bias, mix_norm_g, w_in, conv_w, conv_b, igate_b, fgate_b, mlstm_norm_g, q_norm_g, k_norm_g,
           w_branch_a, w_branch_b, w_out, ffn_norm_g, dense_w_gate, dense_w_up, dense_w_down, router_w,
           expert_w_gate, expert_w_up, expert_w_down):
    batch, seq, d = x.shape
    depth = w_in.shape[0]
    assert d == D_MODEL and seq % MLSTM_T == 0 and seq % MOBA_BLOCK == 0 and seq % PROJ_TM == 0
    assert (batch * seq) % MOE_TM == 0
    x2 = x.reshape(batch * seq, d)
    bias_tiles = _moba_bias_tiles(rel_bias)
    for layer in range(depth):
        x2 = _mixer(x2, batch, seq, rel_bias, bias_tiles, mix_norm_g[layer], w_in[layer], conv_w[layer], conv_b[layer],
                    igate_b[layer], fgate_b[layer], mlstm_norm_g[layer], q_norm_g[layer], k_norm_g[layer],
                    w_branch_a[layer], w_branch_b[layer], w_out[layer])
        g = ffn_norm_g[layer].reshape(1, d)
        j = layer // 2
        if layer % 2 == 0:
            x2 = _ffn(x2, g, dense_w_gate[j].astype(BF16), dense_w_up[j].astype(BF16), dense_w_down[j].astype(BF16))
        else:
            x2 = _moe(x2, g, router_w[j], expert_w_gate[j].astype(BF16), expert_w_up[j].astype(BF16),
                      expert_w_down[j].astype(BF16))
    return x2.reshape(batch, seq, d)
```

```python
import functools
import math

import jax
import jax.numpy as jnp
from jax import lax
from jax.experimental import pallas as pl
from jax.experimental.pallas import tpu as pltpu

F32 = jnp.float32
BF16 = jnp.bfloat16
HIGHEST = lax.Precision.HIGHEST

D_MODEL = 1024
M_HEADS = 4
M_HEAD_DIM = 128
M_WIDTH = M_HEADS * M_HEAD_DIM
CONV_WIDTH = 4
A_HEADS = 8
A_HEAD_DIM = 64
A_WIDTH = A_HEADS * A_HEAD_DIM
MOBA_BLOCK = 256
MOBA_TOPK = 3
REL_BUCKETS = 32
REL_MAX_DIST = 1024
N_EXPERTS = 8
EPS = 1e-6

LANES = 128
SUBLANES = 8
NEG = -1e30
VMEM_LIMIT = 56 * 1024 * 1024

C_GA = 0
C_GB = D_MODEL
C_MQ = 2 * D_MODEL
C_MK = C_MQ + M_WIDTH
C_MV = C_MK + M_WIDTH
C_MO = C_MV + M_WIDTH
C_AK = C_MO + M_WIDTH
NAT_WIDTH = C_AK + A_WIDTH
N_BIAS_TILES = 6

PROJ_TM = 512
MLSTM_T = 512
MLSTM_CHUNK = 128
MERGE_TM = 512
FFN_TM = 512
FFN_CHUNK = 256
ROUTER_TM = 512
MOE_TM = 512
MOE_TF = 1792
SCATTER_R = 1024


def _params(*sem):
    return pltpu.CompilerParams(dimension_semantics=sem, vmem_limit_bytes=VMEM_LIMIT)


def _const_spec(shape):
    nd = len(shape)
    return pl.BlockSpec(shape, lambda *_: (0,) * nd, pipeline_mode=pl.Buffered(1))


def _rms(x, g):
    return x * lax.rsqrt(jnp.mean(x * x, axis=-1, keepdims=True) + EPS) * g


ROW_SLAB = 128


def _in_proj_kernel(x_ref, g_ref, wn_ref, wt_ref, wg_ref, cw_ref, cb_ref, qg_ref, kg_ref, z_ref, zt_ref, gt_ref,
                    hn_s, qk_buf, k_buf, *, tiles_per_seq):
    tm = x_ref.shape[0]
    DA = A_HEAD_DIM

    @pl.when(pl.program_id(0) % tiles_per_seq == 0)
    def _sequence_start():
        qk_buf[0:SUBLANES, :] = jnp.zeros((SUBLANES, 2 * M_WIDTH), F32)

    hn_s[...] = _rms(x_ref[...], g_ref[...]).astype(BF16)

    def natural(c):
        return jnp.dot(hn_s[...], wn_ref[:, c:c + 512], preferred_element_type=F32)

    def transposed(c):
        return lax.dot_general(wt_ref[c:c + 256, :], hn_s[...], (((1,), (1,)), ((), ())),
                               preferred_element_type=F32)

    qk_buf[SUBLANES:SUBLANES + tm, 0:M_WIDTH] = natural(C_MQ)
    qk_buf[SUBLANES:SUBLANES + tm, M_WIDTH:] = natural(C_MK)
    for r in range(0, tm, ROW_SLAB):
        acc = cb_ref[...] + cw_ref[CONV_WIDTH - 1:CONV_WIDTH, :] * qk_buf[SUBLANES + r:SUBLANES + r + ROW_SLAB, :]
        for j in range(CONV_WIDTH - 1):
            off = SUBLANES - (CONV_WIDTH - 1) + j + r
            acc = acc + cw_ref[j:j + 1, :] * qk_buf[off:off + ROW_SLAB, :]
        qk = acc * jax.nn.sigmoid(acc)
        z_ref[r:r + ROW_SLAB, C_MQ:C_MQ + M_WIDTH] = qk[:, :M_WIDTH].astype(BF16)
        z_ref[r:r + ROW_SLAB, C_MK:C_MK + M_WIDTH] = (qk[:, M_WIDTH:] * (M_HEAD_DIM ** -0.5)).astype(BF16)
    qk_buf[0:SUBLANES, :] = qk_buf[tm:tm + SUBLANES, :]

    k_buf[...] = natural(C_AK)
    head0 = lax.broadcasted_iota(jnp.int32, (ROW_SLAB, LANES), 1) < DA
    for r in range(0, tm, ROW_SLAB):
        for t in range(A_WIDTH // LANES):
            cols = slice(t * LANES, (t + 1) * LANES)
            kt = k_buf[r:r + ROW_SLAB, cols]
            k2 = kt * kt
            s0 = jnp.sum(jnp.where(head0, k2, 0.0), axis=-1, keepdims=True)
            s1 = jnp.sum(jnp.where(head0, 0.0, k2), axis=-1, keepdims=True)
            kn = kt * jnp.where(head0, lax.rsqrt(s0 / DA + EPS), lax.rsqrt(s1 / DA + EPS)) * kg_ref[:, cols]
            z_ref[r:r + ROW_SLAB, C_AK + t * LANES:C_AK + (t + 1) * LANES] = kn.astype(BF16)

    for c in range(0, A_WIDTH, 256):
        q3 = transposed(c).reshape(256 // DA, DA, tm)
        inv = lax.rsqrt(jnp.mean(q3 * q3, axis=1, keepdims=True) + EPS)
        qn = q3 * inv * (qg_ref[...] * (DA ** -0.5 * LOG2E))
        zt_ref[c:c + 256, :] = qn.reshape(256, tm).astype(BF16)
    for c in (C_GA, C_GA + 512, C_GB, C_GB + 512, C_MV, C_MO):
        z_ref[:, c:c + 512] = natural(c).astype(BF16)
    for c in range(A_WIDTH, 2 * A_WIDTH, 256):
        zt_ref[c:c + 256, :] = transposed(c).astype(BF16)
    gt_ref[...] = jnp.dot(hn_s[...], wg_ref[...], preferred_element_type=F32)


def _in_proj(x2, g, wn, wt, wg, conv_w, conv_b, qg, kg, tiles_per_seq):
    n = x2.shape[0]
    tm = PROJ_TM
    return pl.pallas_call(
        functools.partial(_in_proj_kernel, tiles_per_seq=tiles_per_seq),
        grid=(n // tm,),
        in_specs=[
            pl.BlockSpec((tm, D_MODEL), lambda i: (i, 0)),
            _const_spec((1, D_MODEL)),
            _const_spec((D_MODEL, NAT_WIDTH)),
            _const_spec((2 * A_WIDTH, D_MODEL)),
            _const_spec((D_MODEL, LANES)),
            _const_spec((CONV_WIDTH, 2 * M_WIDTH)),
            _const_spec((1, 2 * M_WIDTH)),
            _const_spec((1, A_HEAD_DIM, 1)),
            _const_spec((1, A_WIDTH)),
        ],
        out_specs=[
            pl.BlockSpec((tm, NAT_WIDTH), lambda i: (i, 0)),
            pl.BlockSpec((2 * A_WIDTH, tm), lambda i: (0, i)),
            pl.BlockSpec((tm, LANES), lambda i: (i, 0)),
        ],
        out_shape=[
            jax.ShapeDtypeStruct((n, NAT_WIDTH), BF16),
            jax.ShapeDtypeStruct((2 * A_WIDTH, n), BF16),
            jax.ShapeDtypeStruct((n, LANES), F32),
        ],
        scratch_shapes=[pltpu.VMEM((tm, D_MODEL), BF16), pltpu.VMEM((tm + SUBLANES, 2 * M_WIDTH), F32),
                        pltpu.VMEM((tm, A_WIDTH), F32)],
        compiler_params=_params("arbitrary"),
        name="in_proj",
    )(x2, g, wn, wt, wg, conv_w, conv_b, qg, kg)


def _mlstm_kernel(zq_ref, zk_ref, zv_ref, zo_ref, gt_ref, gb_ref, ng_ref, o_ref, c_s, m_s, *, t_blk, chunk):
    L = chunk
    DH = M_HEAD_DIM
    assert L == LANES and DH == LANES

    @pl.when(pl.program_id(1) == 0)
    def _init():
        c_s[...] = jnp.zeros_like(c_s)
        m_s[...] = jnp.zeros_like(m_s)

    row = lax.broadcasted_iota(jnp.int32, (L, L), 0)
    col = lax.broadcasted_iota(jnp.int32, (L, L), 1)
    causal = col <= row
    tri = causal.astype(F32)
    lane = lax.broadcasted_iota(jnp.int32, (L, LANES), 1)
    ones_blk = jnp.ones((L, LANES), BF16)

    def chunk_body(c, carry):
        r0 = c * L
        g_pre = gt_ref[pl.ds(r0, L), :] + gb_ref[...]
        log_f = jnp.minimum(g_pre, 0.0) - jnp.log1p(jnp.exp(-jnp.abs(g_pre)))
        bcum = jnp.dot(tri, log_f, precision=HIGHEST, preferred_element_type=F32)
        gm = jnp.where(lane < M_HEADS, g_pre, bcum)
        gm_t = gm.T
        c_all = [c_s[h] for h in range(M_HEADS)]
        m_all = m_s[...]
        heads = range(M_HEADS)
        hs = [slice(h * DH, (h + 1) * DH) for h in heads]
        qh = [zq_ref[pl.ds(r0, L), hs[h]] for h in heads]
        kb = [zk_ref[pl.ds(r0, L), hs[h]] for h in heads]
        kf = [k.astype(F32) for k in kb]
        v_ext = [jnp.concatenate([zv_ref[pl.ds(r0, L), hs[h]], ones_blk], axis=1) for h in heads]
        qk = [lax.dot_general(qh[h], kb[h], (((1,), (1,)), ((), ())), preferred_element_type=F32)
              for h in heads]
        inter = [jnp.dot(qh[h], c_all[h].astype(BF16), preferred_element_type=F32) for h in heads]
        mt, w_state, s_bf, kw_t, decay, m_new = [], [], [], [], [], []
        for h in heads:
            bb = jnp.broadcast_to(gm[:, M_HEADS + h:M_HEADS + h + 1], (L, LANES))
            ii = jnp.broadcast_to(gm[:, h:h + 1], (L, LANES))
            bb_row = gm_t[M_HEADS + h:M_HEADS + h + 1, :]
            ii_row = gm_t[h:h + 1, :]
            m_old = m_all[h:h + 1, :]
            dlog = jnp.where(causal, bb - (bb_row - ii_row), -jnp.inf)
            a = bb + m_old
            mt.append(jnp.maximum(a, jnp.broadcast_to(jnp.max(dlog, axis=-1, keepdims=True), (L, LANES))))
            w_state.append(jnp.exp(a - mt[h]))
            s_bf.append((qk[h] * jnp.exp(dlog - mt[h])).astype(BF16))
            b_last = bb[L - 1:L, :]
            g = b_last - bb + ii
            m_new.append(jnp.maximum(b_last + m_old, jnp.max(g, axis=0, keepdims=True)))
            decay.append(jnp.exp(b_last + m_old - m_new[h]))
            kw_t.append((kf[h] * jnp.exp(g - m_new[h])).T.astype(BF16))
        intra = [jnp.dot(s_bf[h], v_ext[h], preferred_element_type=F32) for h in heads]
        upd = [jnp.dot(kw_t[h], v_ext[h], preferred_element_type=F32) for h in heads]
        for h in heads:
            num = intra[h][:, :DH] + w_state[h] * inter[h][:, :DH]
            den = intra[h][:, DH:] + w_state[h] * inter[h][:, DH:]
            h_t = num / jnp.maximum(jnp.abs(den), jnp.exp(-mt[h]))
            hc = jax.nn.sigmoid(zo_ref[pl.ds(r0, L), hs[h]].astype(F32)) * h_t
            o_ref[pl.ds(r0, L), hs[h]] = _rms(hc, ng_ref[:, hs[h]]).astype(BF16)
        for h in heads:
            c_s[h] = jnp.concatenate([decay[h], decay[h]], axis=1) * c_all[h] + upd[h]
            m_s[h:h + 1, :] = m_new[h]
        return carry

    for c in range(t_blk // L):
        chunk_body(c, 0)


def _mlstm(z, gates, gate_b, norm_g, batch, seq):
    n = batch * seq
    t = MLSTM_T
    nt = seq // t
    row_blk = lambda b, s: b * nt + s
    zspec = lambda cb: pl.BlockSpec((t, M_WIDTH), lambda b, s: (row_blk(b, s), cb))
    return pl.pallas_call(
        functools.partial(_mlstm_kernel, t_blk=t, chunk=MLSTM_CHUNK),
        grid=(batch, nt),
        in_specs=[
            zspec(C_MQ // M_WIDTH), zspec(C_MK // M_WIDTH), zspec(C_MV // M_WIDTH), zspec(C_MO // M_WIDTH),
            pl.BlockSpec((t, LANES), lambda b, s: (row_blk(b, s), 0)),
            _const_spec((1, LANES)),
            _const_spec((1, M_WIDTH)),
        ],
        out_specs=pl.BlockSpec((t, M_WIDTH), lambda b, s: (row_blk(b, s), 0)),
        out_shape=jax.ShapeDtypeStruct((n, M_WIDTH), BF16),
        scratch_shapes=[
            pltpu.VMEM((M_HEADS, M_HEAD_DIM, M_HEAD_DIM + LANES), F32),
            pltpu.VMEM((SUBLANES, LANES), F32),
        ],
        compiler_params=_params("arbitrary", "arbitrary"),
        name="mlstm",
    )(z, z, z, z, gates, gate_b, norm_g)


MOBA_V_ROWS = A_HEAD_DIM + 16
MOBA_ROWB_ROWS = 24
MOBA_GROUP = 34
LOG2E = math.log2(math.e)
FAST_SOFTMAX_MIN_DENOM = 1e-25


def _moba_items(n_blocks):
    items = []
    for qb in range(n_blocks):
        items.append((qb, qb, 0, 0))
        for j in range(qb):
            items.append((qb, j, min(qb - j, N_BIAS_TILES - 1), j + 1))
    n_groups = -(-len(items) // MOBA_GROUP)
    noop = (n_blocks - 1, 0, 0, MOBA_ROWB_ROWS - 1)
    items += [noop] * (n_groups * MOBA_GROUP + 2 - len(items))
    return n_groups, [jnp.asarray([it[c] for it in items], jnp.int32) for c in range(4)]


def _moba_kernel(it_q, it_blk, it_tile, it_row, qt_ref, k_ref, vt_ref, bias_ref, bound_ref, o_ref,
                 vt_s, kmean_s, qh_s, rowb_s, acc_s, st_a, st_b, p_a, p_b, *, n_blocks, n_groups):
    BS = MOBA_BLOCK
    DA = A_HEAD_DIM
    seq = n_blocks * BS

    ones_row = jnp.where(lax.broadcasted_iota(jnp.int32, (16, BS), 0) == 0, 1.0, 0.0).astype(BF16)
    for j in range(n_blocks):
        kmean_s[j:j + 1, :] = jnp.mean(k_ref[j * BS:(j + 1) * BS, :].astype(F32), axis=0, keepdims=True)
        for h in range(2):
            vt_s[j, h, 0:DA, :] = vt_ref[h * DA:(h + 1) * DA, j * BS:(j + 1) * BS]
            vt_s[j, h, DA:MOBA_V_ROWS, :] = ones_row

    qn = qt_ref[...].astype(F32)
    head0_q = lax.broadcasted_iota(jnp.int32, qn.shape, 0) < DA

    blk = lax.broadcasted_iota(jnp.int32, (n_blocks, BS), 0)
    lane_m = lax.broadcasted_iota(jnp.int32, (n_blocks, LANES), 1)
    kmean = kmean_s[...]
    kmean2 = jnp.concatenate([jnp.where(lane_m < DA, kmean, 0.0), jnp.where(lane_m < DA, 0.0, kmean)], axis=0)
    k_hi = kmean2.astype(BF16)
    k_mid = (kmean2 - k_hi.astype(F32)).astype(BF16)
    k_lo = (kmean2 - k_hi.astype(F32) - k_mid.astype(F32)).astype(BF16)
    split = jnp.dot(jnp.concatenate([k_hi, k_mid, k_lo], axis=0), qt_ref[...], preferred_element_type=F32)
    gates = split[0:2 * n_blocks] + split[2 * n_blocks:4 * n_blocks] + split[4 * n_blocks:6 * n_blocks]
    for h in range(2):
        bound = bound_ref[h, :, 0:1]
        hmask_q = head0_q if h == 0 else jnp.logical_not(head0_q)
        qh = jnp.where(hmask_q, qn, 0.0).astype(BF16)
        for qb in range(n_blocks):
            cols = slice(qb * BS, (qb + 1) * BS)
            past = blk < qb
            gate = jnp.where(past, gates[h * n_blocks:(h + 1) * n_blocks, cols], -jnp.inf)
            rank = jnp.zeros((n_blocks, BS), jnp.int32)
            for j2 in range(qb):
                other = gate[j2:j2 + 1, :]
                beats = (other > gate) | ((other == gate) & (j2 < blk))
                rank = rank + beats.astype(jnp.int32)
            selb = jnp.where(past & (rank < MOBA_TOPK), 0.0, NEG) - bound
            qh_s[qb, h] = qh[:, cols]
            rowb_s[qb, h, 0:1, :] = jnp.broadcast_to(-bound, (1, BS))
            rowb_s[qb, h, 1:n_blocks + 1, :] = selb
            rowb_s[qb, h, n_blocks + 1:, :] = jnp.full((MOBA_ROWB_ROWS - n_blocks - 1, BS), NEG, F32)
    acc_s[...] = jnp.zeros_like(acc_s)

    def scores(h, qb, j, tile, row):
        kj = k_ref[pl.ds(pl.multiple_of(j * BS, BS), BS), :]
        st = jnp.dot(kj, qh_s[qb, h], preferred_element_type=F32)
        return st + bias_ref[h, tile] + rowb_s[qb, h, pl.ds(row, 1), :]

    def pv(h, j, p):
        return jnp.dot(vt_s[j, h], p, preferred_element_type=F32)

    def stage_scores(i, st_ref):
        for h in range(2):
            st_ref[h] = scores(h, it_q[i], it_blk[i], it_tile[i], it_row[i])

    def stage_exp(st_ref, p_ref):
        for h in range(2):
            p_ref[h] = jnp.exp2(st_ref[h]).astype(BF16)

    def stage_pv(i, p_ref):
        for h in range(2):
            acc_s[it_q[i], h] += pv(h, it_blk[i], p_ref[h])

    def group(m, carry):
        for u in range(0, MOBA_GROUP, 2):
            i = MOBA_GROUP * m + u
            stage_pv(i, p_a)
            stage_scores(i + 2, st_a)
            stage_exp(st_b, p_b)
            stage_pv(i + 1, p_b)
            stage_scores(i + 3, st_b)
            stage_exp(st_a, p_a)
        return carry

    stage_scores(0, st_a)
    stage_exp(st_a, p_a)
    stage_scores(1, st_b)
    lax.fori_loop(0, n_groups, group, 0)

    def finish(qb, l_min):
        outs = []
        for h in range(2):
            acc = acc_s[qb, h]
            l = acc[DA:DA + 1, :]
            outs.append(acc[0:DA, :] / l)
            l_min = jnp.minimum(l_min, jnp.min(l))
        o_ref[qb * BS:(qb + 1) * BS, :] = jnp.concatenate(outs, axis=0).T.astype(BF16)
        return l_min

    l_min = jnp.float32(jnp.inf)
    for qb in range(n_blocks):
        l_min = finish(qb, l_min)

    def online(qb, carry):
        state = []
        for h in range(2):
            st = scores(h, qb, qb, 0, 0)
            m = jnp.max(st, axis=0, keepdims=True)
            p = jnp.exp2(st - m)
            state += [m, jnp.sum(p, axis=0, keepdims=True), pv(h, qb, p.astype(BF16))[0:DA, :]]

        def past_body(j, state):
            tile = jnp.minimum(qb - j, N_BIAS_TILES - 1)
            new = []
            for h in range(2):
                m_old, l_old, acc_old = state[3 * h:3 * h + 3]
                st = scores(h, qb, j, tile, j + 1)
                m_new = jnp.maximum(m_old, jnp.max(st, axis=0, keepdims=True))
                alpha = jnp.exp2(m_old - m_new)
                p = jnp.exp2(st - m_new)
                new += [m_new, alpha * l_old + jnp.sum(p, axis=0, keepdims=True),
                        alpha * acc_old + pv(h, j, p.astype(BF16))[0:DA, :]]
            return tuple(new)

        state = lax.fori_loop(0, qb, past_body, tuple(state))
        outs = [state[3 * h + 2] / state[3 * h + 1] for h in range(2)]
        o_ref[pl.ds(pl.multiple_of(qb * BS, BS), BS), :] = jnp.concatenate(outs, axis=0).T.astype(BF16)
        return carry

    @pl.when(l_min < FAST_SOFTMAX_MIN_DENOM)
    def _redo():
        lax.fori_loop(0, n_blocks, online, 0)


def _moba_logit_bound(rel_bias, q_norm_g, k_norm_g):
    qk = A_HEAD_DIM * jnp.max(jnp.abs(q_norm_g)) * jnp.max(jnp.abs(k_norm_g)) * (A_HEAD_DIM ** -0.5) * 1.02
    b = (qk + jnp.max(rel_bias, axis=0)) * LOG2E
    return jnp.broadcast_to(b[:, None, None], (A_HEADS, 1, LANES)).astype(F32)


def _moba(z, zt, bias_tiles, bound, batch, seq):
    n = batch * seq
    nb = seq // MOBA_BLOCK
    bs = MOBA_BLOCK
    hp = A_HEADS // 2
    w2 = 2 * A_HEAD_DIM
    assert nb + 2 <= MOBA_ROWB_ROWS
    n_groups, items = _moba_items(nb)
    grid_spec = pltpu.PrefetchScalarGridSpec(
        num_scalar_prefetch=len(items),
        grid=(batch, hp),
        in_specs=[
            pl.BlockSpec((w2, seq), lambda b, p, *_: (p, b)),
            pl.BlockSpec((seq, w2), lambda b, p, *_: (b, C_AK // w2 + p)),
            pl.BlockSpec((w2, seq), lambda b, p, *_: (hp + p, b)),
            pl.BlockSpec((2, N_BIAS_TILES, bs, bs), lambda b, p, *_: (p, 0, 0, 0)),
            pl.BlockSpec((2, 1, LANES), lambda b, p, *_: (p, 0, 0)),
        ],
        out_specs=pl.BlockSpec((seq, w2), lambda b, p, *_: (b, p)),
        scratch_shapes=[
            pltpu.VMEM((nb, 2, MOBA_V_ROWS, bs), BF16),
            pltpu.VMEM((nb, w2), F32),
            pltpu.VMEM((nb, 2, w2, bs), BF16),
            pltpu.VMEM((nb, 2, MOBA_ROWB_ROWS, bs), F32),
            pltpu.VMEM((nb, 2, MOBA_V_ROWS, bs), F32),
            pltpu.VMEM((2, bs, bs), F32),
            pltpu.VMEM((2, bs, bs), F32),
            pltpu.VMEM((2, bs, bs), BF16),
            pltpu.VMEM((2, bs, bs), BF16),
        ],
    )
    return pl.pallas_call(
        functools.partial(_moba_kernel, n_blocks=nb, n_groups=n_groups),
        grid_spec=grid_spec,
        out_shape=jax.ShapeDtypeStruct((n, A_WIDTH), BF16),
        compiler_params=_params("arbitrary", "arbitrary"),
        name="moba",
    )(*items, zt, z, zt, bias_tiles, bound)


def _t5_bucket(dist):
    n = jnp.maximum(dist, 0)
    max_exact = REL_BUCKETS // 2
    log_ratio = jnp.log(jnp.maximum(n, max_exact).astype(F32) / max_exact) / math.log(REL_MAX_DIST / max_exact)
    large = max_exact + (log_ratio * (REL_BUCKETS - max_exact)).astype(jnp.int32)
    large = jnp.minimum(large, REL_BUCKETS - 1)
    return jnp.where(n < max_exact, n, large)


def _bias_tiles_kernel(rb_ref, bucket_ref, o_ref):
    bucket = bucket_ref[0]
    hit = [bucket == b for b in range(REL_BUCKETS)]
    tk = lax.broadcasted_iota(jnp.int32, bucket.shape, 0)
    tq = lax.broadcasted_iota(jnp.int32, bucket.shape, 1)
    masked = (tk > tq) & (pl.program_id(0) == 0)
    for h in range(A_HEADS):
        acc = jnp.zeros(bucket.shape, F32)
        for b in range(REL_BUCKETS):
            acc = jnp.where(hit[b], rb_ref[b, h], acc)
        o_ref[h, 0] = jnp.where(masked, NEG, acc * LOG2E)


def _moba_bias_tiles(rel_bias):
    assert (N_BIAS_TILES - 1) * MOBA_BLOCK - (MOBA_BLOCK - 1) >= REL_MAX_DIST
    tk = jnp.arange(MOBA_BLOCK)[None, :, None]
    tq = jnp.arange(MOBA_BLOCK)[None, None, :]
    diff = jnp.arange(N_BIAS_TILES)[:, None, None]
    bucket = _t5_bucket(diff * MOBA_BLOCK + tq - tk).astype(jnp.int32)
    return pl.pallas_call(
        _bias_tiles_kernel,
        grid=(N_BIAS_TILES,),
        in_specs=[
            pl.BlockSpec(memory_space=pltpu.SMEM),
            pl.BlockSpec((1, MOBA_BLOCK, MOBA_BLOCK), lambda t: (t, 0, 0)),
        ],
        out_specs=pl.BlockSpec((A_HEADS, 1, MOBA_BLOCK, MOBA_BLOCK), lambda t: (0, t, 0, 0)),
        out_shape=jax.ShapeDtypeStruct((A_HEADS, N_BIAS_TILES, MOBA_BLOCK, MOBA_BLOCK), F32),
        compiler_params=_params("arbitrary"),
        name="moba_bias_tiles",
    )(rel_bias.astype(F32), bucket)


def _merge_kernel(x_ref, ha_ref, ob_ref, ga_ref, gb_ref, wa_ref, wb_ref, wo_ref, o_ref):
    ya = jnp.dot(ha_ref[...], wa_ref[...], preferred_element_type=F32)
    yb = jnp.dot(ob_ref[...], wb_ref[...], preferred_element_type=F32)
    y = jax.nn.sigmoid(ga_ref[...].astype(F32)) * ya + jax.nn.sigmoid(gb_ref[...].astype(F32)) * yb
    o_ref[...] = x_ref[...] + jnp.dot(y.astype(BF16), wo_ref[...], preferred_element_type=F32)


def _merge(x2, ha, ob, z, wa, wb, wo):
    n = x2.shape[0]
    tm = MERGE_TM
    return pl.pallas_call(
        _merge_kernel,
        grid=(n // tm,),
        in_specs=[
            pl.BlockSpec((tm, D_MODEL), lambda i: (i, 0)),
            pl.BlockSpec((tm, M_WIDTH), lambda i: (i, 0)),
            pl.BlockSpec((tm, A_WIDTH), lambda i: (i, 0)),
            pl.BlockSpec((tm, D_MODEL), lambda i: (i, C_GA // D_MODEL)),
            pl.BlockSpec((tm, D_MODEL), lambda i: (i, C_GB // D_MODEL)),
            _const_spec((M_WIDTH, D_MODEL)),
            _const_spec((A_WIDTH, D_MODEL)),
            _const_spec((D_MODEL, D_MODEL)),
        ],
        out_specs=pl.BlockSpec((tm, D_MODEL), lambda i: (i, 0)),
        out_shape=jax.ShapeDtypeStruct((n, D_MODEL), F32),
        compiler_params=_params("arbitrary"),
        name="merge",
    )(x2, ha, ob, z, z, wa, wb, wo)


def _ffn_kernel(x_ref, g_ref, wg_ref, wu_ref, wd_ref, o_ref, *, d_ff):
    x = x_ref[...]
    hn = _rms(x, g_ref[...]).astype(BF16)
    acc = x
    for c in range(0, d_ff, FFN_CHUNK):
        g = jnp.dot(hn, wg_ref[:, c:c + FFN_CHUNK], preferred_element_type=F32)
        u = jnp.dot(hn, wu_ref[:, c:c + FFN_CHUNK], preferred_element_type=F32)
        a = (g * jax.nn.sigmoid(g) * u).astype(BF16)
        acc = acc + jnp.dot(a, wd_ref[c:c + FFN_CHUNK, :], preferred_element_type=F32)
    o_ref[...] = acc


def _ffn(x2, g, wg, wu, wd):
    n = x2.shape[0]
    d_ff = wg.shape[1]
    assert d_ff % FFN_CHUNK == 0
    tm = FFN_TM
    return pl.pallas_call(
        functools.partial(_ffn_kernel, d_ff=d_ff),
        grid=(n // tm,),
        in_specs=[
            pl.BlockSpec((tm, D_MODEL), lambda i: (i, 0)),
            _const_spec((1, D_MODEL)),
            _const_spec((D_MODEL, d_ff)),
            _const_spec((D_MODEL, d_ff)),
            _const_spec((d_ff, D_MODEL)),
        ],
        out_specs=pl.BlockSpec((tm, D_MODEL), lambda i: (i, 0)),
        out_shape=jax.ShapeDtypeStruct((n, D_MODEL), F32),
        compiler_params=_params("arbitrary"),
        name="ffn_dense",
    )(x2, g, wg, wu, wd)


META_W0, META_W1, META_E0, META_E1, META_R0, META_R1 = range(6)


def _router_kernel(x_ref, g_ref, rw_ref, h_ref, meta_ref, cnt_ref, carry_s):
    @pl.when(pl.program_id(0) == 0)
    def _init():
        carry_s[...] = jnp.zeros_like(carry_s)

    h = _rms(x_ref[...], g_ref[...])
    h_ref[...] = h
    tm = h.shape[0]
    logits = jnp.dot(h, rw_ref[...], precision=HIGHEST, preferred_element_type=F32)
    lane = lax.broadcasted_iota(jnp.int32, (tm, LANES), 1)
    lg = jnp.where(lane < N_EXPERTS, logits, -jnp.inf)
    m1 = jnp.max(lg, axis=-1, keepdims=True)
    i1 = jnp.min(jnp.where(lg == m1, lane, LANES), axis=-1, keepdims=True)
    lg2 = jnp.where(lane == i1, -jnp.inf, lg)
    m2 = jnp.max(lg2, axis=-1, keepdims=True)
    i2 = jnp.min(jnp.where(lg2 == m2, lane, LANES), axis=-1, keepdims=True)
    e = jnp.exp(m2 - m1)
    w1 = 1.0 / (1.0 + e)
    w2 = e / (1.0 + e)
    hit1 = lane == i1
    hit2 = lane == i2
    onehot = jnp.where(hit1 | hit2, 1.0, 0.0)
    r = lax.broadcasted_iota(jnp.int32, (tm, tm), 0)
    c = lax.broadcasted_iota(jnp.int32, (tm, tm), 1)
    before = jnp.where(c < r, 1.0, 0.0).astype(BF16)
    carry = carry_s[0:1, :]
    pref = jnp.dot(before, onehot.astype(BF16), preferred_element_type=F32) + carry
    r1 = jnp.sum(jnp.where(hit1, pref, 0.0), axis=-1, keepdims=True)
    r2 = jnp.sum(jnp.where(hit2, pref, 0.0), axis=-1, keepdims=True)
    new_carry = carry + jnp.sum(onehot, axis=0, keepdims=True)
    carry_s[...] = jnp.broadcast_to(new_carry, carry_s.shape)
    cnt_ref[...] = jnp.broadcast_to(new_carry, cnt_ref.shape)
    meta = jnp.zeros((tm, LANES), F32)
    for idx, val in ((META_W0, w1), (META_W1, w2), (META_E0, i1.astype(F32)), (META_E1, i2.astype(F32)),
                     (META_R0, r1), (META_R1, r2)):
        meta = jnp.where(lane == idx, val, meta)
    meta_ref[...] = meta


def _router(x2, g, rw):
    n = x2.shape[0]
    tm = ROUTER_TM
    return pl.pallas_call(
        _router_kernel,
        grid=(n // tm,),
        in_specs=[
            pl.BlockSpec((tm, D_MODEL), lambda i: (i, 0)),
            _const_spec((1, D_MODEL)),
            _const_spec((D_MODEL, LANES)),
        ],
        out_specs=[
            pl.BlockSpec((tm, D_MODEL), lambda i: (i, 0)),
            pl.BlockSpec((tm, LANES), lambda i: (i, 0)),
            _const_spec((SUBLANES, LANES)),
        ],
        out_shape=[
            jax.ShapeDtypeStruct((n, D_MODEL), F32),
            jax.ShapeDtypeStruct((n, LANES), F32),
            jax.ShapeDtypeStruct((SUBLANES, LANES), F32),
        ],
        scratch_shapes=[pltpu.VMEM((SUBLANES, LANES), F32)],
        compiler_params=_params("arbitrary"),
        name="moe_router",
    )(x2, g, rw)


def _row_copy(src_ref, src_row, dst_ref, dst_row, sem):
    return pltpu.make_async_copy(src_ref.at[pl.ds(src_row, 1)], dst_ref.at[pl.ds(dst_row, 1)], sem)


def _scatter_kernel(pos_ref, pad_ref, h_ref, xs_ref, zero_s, sem, *, rows, pads):
    @pl.when(pl.program_id(0) == 0)
    def _init():
        zero_s[...] = jnp.zeros_like(zero_s)

    for r in range(rows):
        _row_copy(h_ref, r, xs_ref, pos_ref[0, 0, r], sem).start(priority=0)
        _row_copy(h_ref, r, xs_ref, pos_ref[0, 0, rows + r], sem).start(priority=1)
    for r in range(pads):
        _row_copy(zero_s, 0, xs_ref, pad_ref[0, 0, r], sem).start(priority=r % 2)
    for _ in range(2):
        pltpu.make_async_copy(h_ref, xs_ref.at[pl.ds(0, rows)], sem).wait()
    pltpu.make_async_copy(h_ref.at[pl.ds(0, pads)], xs_ref.at[pl.ds(0, pads)], sem).wait()


def _scatter(pos, pad_pos, h, n_rows):
    n = h.shape[0]
    r = SCATTER_R
    steps = n // r
    pads = pad_pos.shape[0] // steps
    assert pads * steps == pad_pos.shape[0] and 2 * n + pad_pos.shape[0] == n_rows
    return pl.pallas_call(
        functools.partial(_scatter_kernel, rows=r, pads=pads),
        grid=(steps,),
        in_specs=[
            pl.BlockSpec((1, 1, 2 * r), lambda i: (i, 0, 0), memory_space=pltpu.SMEM),
            pl.BlockSpec((1, 1, pads), lambda i: (i, 0, 0), memory_space=pltpu.SMEM),
            pl.BlockSpec((r, D_MODEL), lambda i: (i, 0)),
        ],
        out_specs=pl.BlockSpec(memory_space=pl.ANY),
        out_shape=jax.ShapeDtypeStruct((n_rows, D_MODEL), F32),
        scratch_shapes=[pltpu.VMEM((SUBLANES, D_MODEL), F32), pltpu.SemaphoreType.DMA(())],
        compiler_params=_params("arbitrary"),
        name="moe_scatter",
    )(pos, pad_pos.reshape(steps, 1, pads), h)


def _moe_kernel(blk_ref, exp_ref, valid_ref, xs_ref, wg_ref, wu_ref, wd_ref, y_ref, xb_s, acc_s, *, n_ff):
    del blk_ref, exp_ref
    i = pl.program_id(0)
    f = pl.program_id(1)

    @pl.when((valid_ref[i] == 0) & (f == n_ff - 1))
    def _idle():
        y_ref[...] = jnp.zeros_like(y_ref)

    @pl.when(valid_ref[i] == 1)
    def _tile():
        @pl.when(f == 0)
        def _cast():
            xb_s[...] = xs_ref[...].astype(BF16)

        xb = xb_s[...]
        d = None
        for c in range(0, wg_ref.shape[2], FFN_CHUNK):
            g = jnp.dot(xb, wg_ref[0, :, c:c + FFN_CHUNK], preferred_element_type=F32)
            u = jnp.dot(xb, wu_ref[0, :, c:c + FFN_CHUNK], preferred_element_type=F32)
            a = (g * jax.nn.sigmoid(g) * u).astype(BF16)
            dc = jnp.dot(a, wd_ref[0, c:c + FFN_CHUNK, :], preferred_element_type=F32)
            d = dc if d is None else d + dc

        @pl.when(f == 0)
        def _first():
            acc_s[...] = d

        @pl.when(f > 0)
        def _rest():
            acc_s[...] += d

        @pl.when(f == n_ff - 1)
        def _out():
            y_ref[...] = acc_s[...]


def _moe_experts(tile_blk, tile_exp, tile_valid, xs, wg, wu, wd):
    rows = xs.shape[0]
    d_ff = wg.shape[2]
    tm, tf = MOE_TM, MOE_TF
    n_ff = d_ff // tf
    assert rows % tm == 0 and d_ff % tf == 0
    grid_spec = pltpu.PrefetchScalarGridSpec(
        num_scalar_prefetch=3,
        grid=(rows // tm, n_ff),
        in_specs=[
            pl.BlockSpec((tm, D_MODEL), lambda i, f, blk, ex, va: (blk[i], 0)),
            pl.BlockSpec((1, D_MODEL, tf), lambda i, f, blk, ex, va: (ex[i], 0, f)),
            pl.BlockSpec((1, D_MODEL, tf), lambda i, f, blk, ex, va: (ex[i], 0, f)),
            pl.BlockSpec((1, tf, D_MODEL), lambda i, f, blk, ex, va: (ex[i], f, 0)),
        ],
        out_specs=pl.BlockSpec((tm, D_MODEL), lambda i, f, blk, ex, va: (i, 0)),
        scratch_shapes=[pltpu.VMEM((tm, D_MODEL), BF16), pltpu.VMEM((tm, D_MODEL), F32)],
    )
    return pl.pallas_call(
        functools.partial(_moe_kernel, n_ff=n_ff),
        grid_spec=grid_spec,
        out_shape=jax.ShapeDtypeStruct((rows, D_MODEL), F32),
        compiler_params=_params("arbitrary", "arbitrary"),
        name="moe_experts",
    )(tile_blk, tile_exp, tile_valid, xs, wg, wu, wd)


def _combine_kernel(pos_ref, x_ref, meta_ref, ys_ref, o_ref, ybuf, sem, *, rows):
    for r in range(rows):
        _row_copy(ys_ref, pos_ref[0, 0, r], ybuf, r, sem).start(priority=0)
        _row_copy(ys_ref, pos_ref[0, 0, rows + r], ybuf, rows + r, sem).start(priority=1)
    pltpu.make_async_copy(ys_ref.at[pl.ds(0, 2 * rows)], ybuf, sem).wait()
    meta = meta_ref[...]
    w0 = meta[:, META_W0:META_W0 + 1]
    w1 = meta[:, META_W1:META_W1 + 1]
    o_ref[...] = x_ref[...] + w0 * ybuf[0:rows, :] + w1 * ybuf[rows:2 * rows, :]


def _combine(pos, x2, meta, ys):
    n = x2.shape[0]
    r = SCATTER_R
    return pl.pallas_call(
        functools.partial(_combine_kernel, rows=r),
        grid=(n // r,),
        in_specs=[
            pl.BlockSpec((1, 1, 2 * r), lambda i: (i, 0, 0), memory_space=pltpu.SMEM),
            pl.BlockSpec((r, D_MODEL), lambda i: (i, 0)),
            pl.BlockSpec((r, LANES), lambda i: (i, 0)),
            pl.BlockSpec(memory_space=pl.ANY),
        ],
        out_specs=pl.BlockSpec((r, D_MODEL), lambda i: (i, 0)),
        out_shape=jax.ShapeDtypeStruct((n, D_MODEL), F32),
        scratch_shapes=[pltpu.VMEM((2 * r, D_MODEL), F32), pltpu.SemaphoreType.DMA(())],
        compiler_params=_params("arbitrary"),
        name="moe_combine",
    )(pos, x2, meta, ys)


def _moe(x2, g, router_w, wg, wu, wd):
    n = x2.shape[0]
    tm = MOE_TM
    rw = jnp.zeros((D_MODEL, LANES), F32).at[:, :N_EXPERTS].set(router_w)
    h, meta, cnt = _router(x2, g, rw)
    counts = cnt[0, :N_EXPERTS].astype(jnp.int32)
    padded = ((counts + tm - 1) // tm) * tm
    ends = jnp.cumsum(padded)
    offs = ends - padded
    n_tiles = (2 * n) // tm + N_EXPERTS
    e0 = meta[:, META_E0].astype(jnp.int32)
    e1 = meta[:, META_E1].astype(jnp.int32)
    pos0 = offs[e0] + meta[:, META_R0].astype(jnp.int32)
    pos1 = offs[e1] + meta[:, META_R1].astype(jnp.int32)
    r = SCATTER_R
    pos = jnp.concatenate([pos0.reshape(n // r, 1, r), pos1.reshape(n // r, 1, r)], axis=-1)
    starts = jnp.arange(n_tiles, dtype=jnp.int32) * tm
    n_valid = ends[-1] // tm
    tile_valid = (starts < ends[-1]).astype(jnp.int32)
    tile_blk = jnp.minimum(jnp.arange(n_tiles, dtype=jnp.int32), n_valid - 1)
    tile_exp = jnp.sum((tile_blk[:, None] * tm >= ends[None, :]).astype(jnp.int32), axis=1)
    n_rows = n_tiles * tm
    seg_base = jnp.concatenate([offs + counts, ends[-1:]])
    seg_len = jnp.concatenate([padded - counts, n_rows - ends[-1:]])
    seg_end = jnp.cumsum(seg_len)
    k = jnp.arange(n_rows - 2 * n, dtype=jnp.int32)
    seg = jnp.sum((k[:, None] >= seg_end[None, :]).astype(jnp.int32), axis=1)
    pad_pos = seg_base[seg] + k - (seg_end - seg_len)[seg]
    xs = _scatter(pos, pad_pos, h, n_rows)
    ys = _moe_experts(tile_blk, tile_exp, tile_valid, xs, wg, wu, wd)
    return _combine(pos, x2, meta, ys)


def _mixer(x2, batch, seq, rel_bias, bias_tiles, norm_g, w_in, conv_w, conv_b, igate_b, fgate_b, mlstm_norm_g,
           q_norm_g, k_norm_g, w_branch_a, w_branch_b, w_out):
    sizes = (M_WIDTH, M_WIDTH, M_WIDTH, M_WIDTH, M_HEADS, M_HEADS, A_WIDTH, A_WIDTH, A_WIDTH, D_MODEL, D_MODEL)
    cuts = [0]
    for s in sizes:
        cuts.append(cuts[-1] + s)
    mq, mk, mv, mo, mi, mf, aq, ak, av, ga, gb = [w_in[:, cuts[i]:cuts[i + 1]] for i in range(len(sizes))]
    wn = jnp.concatenate([ga, gb, mq, mk, mv, mo, ak], axis=1).astype(BF16)
    wt = jnp.concatenate([aq, av], axis=1).T.astype(BF16)
    wgate = jnp.zeros((D_MODEL, LANES), F32).at[:, :M_HEADS].set(mi).at[:, M_HEADS:2 * M_HEADS].set(mf).astype(BF16)
    gate_b = jnp.zeros((1, LANES), F32).at[0, :M_HEADS].set(igate_b).at[0, M_HEADS:2 * M_HEADS].set(fgate_b)
    z, zt, gates = _in_proj(x2, norm_g.reshape(1, D_MODEL), wn, wt, wgate, conv_w, conv_b.reshape(1, -1),
                            q_norm_g.reshape(1, A_HEAD_DIM, 1), jnp.tile(k_norm_g, A_HEADS).reshape(1, A_WIDTH),
                            seq // PROJ_TM)
    ha = _mlstm(z, gates, gate_b, mlstm_norm_g.reshape(1, -1), batch, seq)
    ob = _moba(z, zt, bias_tiles, _moba_logit_bound(rel_bias, q_norm_g, k_norm_g), batch, seq)
    return _merge(x2, ha, ob, z, w_branch_a.astype(BF16), w_branch_b.astype(BF16), w_out.astype(BF16))


def kernel(x, rel_bias, mix_norm_g, w_in, conv_w, conv_b, igate_b, fgate_b, mlstm_norm_g, q_norm_g, k_norm_g,
           w_branch_a, w_branch_b, w_out, ffn_norm_g, dense_w_gate, dense_w_up, dense_w_down, router_w,
           expert_w_gate, expert_w_up, expert_w_down):
    batch, seq, d = x.shape
    depth = w_in.shape[0]
    assert d == D_MODEL and seq % MLSTM_T == 0 and seq % MOBA_BLOCK == 0 and seq % PROJ_TM == 0
    assert (batch * seq) % MOE_TM == 0
    x2 = x.reshape(batch * seq, d)
    bias_tiles = _moba_bias_tiles(rel_bias)
    for layer in range(depth):
        x2 = _mixer(x2, batch, seq, rel_bias, bias_tiles, mix_norm_g[layer], w_in[layer], conv_w[layer], conv_b[layer],
                    igate_b[layer], fgate_b[layer], mlstm_norm_g[layer], q_norm_g[layer], k_norm_g[layer],
                    w_branch_a[layer], w_branch_b[layer], w_out[layer])
        g = ffn_norm_g[layer].reshape(1, d)
        j = layer // 2
        if layer % 2 == 0:
            x2 = _ffn(x2, g, dense_w_gate[j].astype(BF16), dense_w_up[j].astype(BF16), dense_w_down[j].astype(BF16))
        else:
            x2 = _moe(x2, g, router_w[j], expert_w_gate[j].astype(BF16), expert_w_up[j].astype(BF16),
                      expert_w_down[j].astype(BF16))
    return x2.reshape(batch, seq, d)
```
